```python
import math
import jax
import jax.numpy as jnp
from jax import lax
import numpy as np

D_MODEL = 1024
BATCH = 32
SEQ = 2048
DEPTH = 2

GRID_W = 64
CTX_LEN = 256
EPS = 1e-6
F32 = jnp.float32

D_MIX = D_MODEL
GROUP_W = D_MIX // 4

GLA_HEADS = 4
GLA_DK = 32
GLA_DV = GROUP_W // GLA_HEADS
GLA_LOWRANK = 16
GLA_TAU = 16.0
GLA_CHUNK = 64

CONV_C = GROUP_W
CONV_K = 31

S5_C = GROUP_W
S5_GROUP = 16
S5_NG = S5_C // S5_GROUP
S5_P = 64

MLA_HEADS = 4
MLA_NOPE = 64
MLA_ROPE = 32
MLA_V = GROUP_W // MLA_HEADS
MLA_Q_LORA = 256
MLA_KV_LORA = 128
ROPE_AXIS = MLA_ROPE // 2
ROPE_BASE = 10000.0
Q_BLOCK = 128

N_EXPERTS = 64
TOP_K = 8
N_GROUPS = 8
TOPK_GROUPS = 4
EXPERT_FF = 256
SHARED_FF = 256
ROUTED_SCALE = 2.5
EXPERT_BLOCK = 256

IN_SPLITS = (GLA_HEADS * GLA_DK, GLA_HEADS * GLA_DK, GLA_HEADS * GLA_DV, GLA_HEADS * GLA_DV, GLA_LOWRANK, GLA_LOWRANK,
             2 * CONV_C, S5_C, MLA_Q_LORA, MLA_KV_LORA, MLA_ROPE)
D_IN = sum(IN_SPLITS)

kernel_name = "hybrid_dit_gla_conv_s5_mla_moe"


def rmsnorm(x, g):
    xf = x.astype(F32)
    return xf * lax.rsqrt(jnp.mean(xf * xf, axis=-1, keepdims=True) + EPS) * g.astype(F32)


def layernorm(x, g, b):
    xf = x.astype(F32)
    mu = jnp.mean(xf, axis=-1, keepdims=True)
    var = jnp.mean(jnp.square(xf - mu), axis=-1, keepdims=True)
    return (xf - mu) * lax.rsqrt(var + EPS) * g + b


def modulate(x, g, shift, scale):
    return rmsnorm(x, g) * (1.0 + scale) + shift


def swiglu(x, w_gate, w_up, w_down):
    return (jax.nn.silu(x @ w_gate) * (x @ w_up)) @ w_down


def gla_log_decay(a_low, w_a2, b_a):
    z = (a_low @ w_a2 + b_a).astype(F32)
    return (jax.nn.log_sigmoid(z) / GLA_TAU).reshape(z.shape[:-1] + (GLA_HEADS, GLA_DK))


def gla_chunked(q, k, v, log_a, s0):
    bsz, seq_len = q.shape[:2]
    n_chunks = seq_len // GLA_CHUNK
    chunk = lambda t: t.reshape((bsz, n_chunks, GLA_CHUNK) + t.shape[2:])
    q, k, v, log_a = chunk(q), chunk(k), chunk(v), chunk(log_a)
    b = jnp.cumsum(log_a, axis=2)
    b_end = b[:, :, -1:]
    q_dec = q * jnp.exp(b)
    k_inv = k * jnp.exp(-b)
    k_tail = k * jnp.exp(b_end - b)
    causal = jnp.tril(jnp.ones((GLA_CHUNK, GLA_CHUNK), dtype=bool))
    att = jnp.where(causal, jnp.einsum("bnihd,bnjhd->bnhij", q_dec, k_inv), 0.0)
    o_intra = jnp.einsum("bnhij,bnjhv->bnihv", att, v)
    u = jnp.einsum("bnjhd,bnjhv->bnhdv", k_tail, v)
    decay = jnp.exp(b_end[:, :, 0])

    def step(s, inp):
        dec, un = inp
        return dec[..., None] * s + un, s

    s_last, s_prev = lax.scan(step, s0, (jnp.moveaxis(decay, 1, 0), jnp.moveaxis(u, 1, 0)))
    o_inter = jnp.einsum("bnihd,nbhdv->bnihv", q_dec, s_prev)
    return (o_intra + o_inter).reshape(bsz, seq_len, GLA_HEADS, -1), s_last


def gla_mixer(seg, seg_c, w_a2, b_a, norm_g):
    def prep(s):
        q, k, v, r, a_f, a_b = [t.astype(F32) for t in s]
        heads = lambda t, d: t.reshape(t.shape[:-1] + (GLA_HEADS, d))
        return (heads(q, GLA_DK) * GLA_DK ** -0.5, heads(k, GLA_DK), heads(v, GLA_DV), r,
                gla_log_decay(a_f, w_a2[0], b_a[0]), gla_log_decay(a_b, w_a2[1], b_a[1]))

    q, k, v, r, la_f, la_b = prep(seg)
    qc, kc, vc, rc, lac_f, lac_b = prep(seg_c)
    s0 = jnp.zeros((q.shape[0], GLA_HEADS, GLA_DK, GLA_DV), F32)
    flip = lambda t: jnp.flip(t, axis=1)
    oc_f, sc_f = gla_chunked(qc, kc, vc, lac_f, s0)
    o_f, _ = gla_chunked(q, k, v, la_f, sc_f)
    oc_b, sc_b = gla_chunked(flip(qc), flip(kc), flip(vc), flip(lac_b), s0)
    o_b, _ = gla_chunked(flip(q), flip(k), flip(v), flip(la_b), sc_b)
    g = norm_g.reshape(GLA_HEADS, GLA_DV)

    def out(o, gate):
        o = rmsnorm(o, g)
        return o.reshape(o.shape[:2] + (GLA_HEADS * GLA_DV,)) * jax.nn.silu(gate)

    return out(o_f + flip(o_b), r), out(oc_f + flip(oc_b), rc)


def conformer_conv(u, dw_w, dw_b, ln_g, ln_b, pw_w, pw_b):
    a, gate = jnp.split(u.astype(F32), 2, axis=-1)
    h = a * jax.nn.sigmoid(gate)
    h = lax.conv_general_dilated(h, dw_w.astype(F32)[:, None, :], window_strides=(1,),
                                 padding=[(CONV_K // 2, CONV_K // 2)],
                                 dimension_numbers=("NWC", "WIO", "NWC"),
                                 feature_group_count=CONV_C) + dw_b
    h = jax.nn.silu(layernorm(h, ln_g, ln_b))
    return h @ pw_w + pw_b


def cplx_affine_combine(e1, e2):
    a1r, a1i, b1r, b1i = e1
    a2r, a2i, b2r, b2i = e2
    return (a2r * a1r - a2i * a1i, a2r * a1i + a2i * a1r,
            a2r * b1r - a2i * b1i + b2r, a2r * b1i + a2i * b1r + b2i)


def s5_discretize(lam_re, lam_im, log_dt, b_re, b_im):
    lam_re, lam_im, b_re, b_im = [t.astype(F32) for t in (lam_re, lam_im, b_re, b_im)]
    dt = jnp.exp(log_dt.astype(F32))[:, None]
    mag = jnp.exp(lam_re * dt)
    a_re, a_im = mag * jnp.cos(lam_im * dt), mag * jnp.sin(lam_im * dt)
    den = lam_re * lam_re + lam_im * lam_im
    f_re = ((a_re - 1.0) * lam_re + a_im * lam_im) / den
    f_im = (a_im * lam_re - (a_re - 1.0) * lam_im) / den
    bb_re = f_re[..., None] * b_re - f_im[..., None] * b_im
    bb_im = f_re[..., None] * b_im + f_im[..., None] * b_re
    return a_re, a_im, bb_re, bb_im


def s5_scan(a_re, a_im, bb_re, bb_im, u, h0_re, h0_im):
    seq_len = u.shape[1]
    bu_re = jnp.einsum("gph,blgh->blgp", bb_re, u)
    bu_im = jnp.einsum("gph,blgh->blgp", bb_im, u)
    shape = (1, seq_len) + a_re.shape
    ap_re, ap_im, h_re, h_im = lax.associative_scan(
        cplx_affine_combine,
        (jnp.broadcast_to(a_re, shape), jnp.broadcast_to(a_im, shape), bu_re, bu_im), axis=1)
    h_re = h_re + ap_re * h0_re[:, None] - ap_im * h0_im[:, None]
    h_im = h_im + ap_re * h0_im[:, None] + ap_im * h0_re[:, None]
    return h_re, h_im


def s5_mixer(u, uc, lam_re, lam_im, log_dt, b_re, b_im, c_re, c_im, d_skip, glu_w, glu_b):
    grp = lambda t: t.astype(F32).reshape(t.shape[:-1] + (S5_NG, S5_GROUP))
    ug, ucg = grp(u), grp(uc)
    flip = lambda t: jnp.flip(t, axis=1)
    zeros = jnp.zeros((u.shape[0], S5_NG, S5_P), F32)
    d = d_skip.astype(F32).reshape(S5_NG, S5_GROUP)
    y, yc = d * ug, d * ucg
    for direction in range(2):
        a_re, a_im, bb_re, bb_im = s5_discretize(lam_re[direction], lam_im[direction], log_dt[direction],
                                                 b_re[direction], b_im[direction])
        src, src_c = (ug, ucg) if direction == 0 else (flip(ug), flip(ucg))
        hc_re, hc_im = s5_scan(a_re, a_im, bb_re, bb_im, src_c, zeros, zeros)
        h_re, h_im = s5_scan(a_re, a_im, bb_re, bb_im, src, hc_re[:, -1], hc_im[:, -1])
        c_r, c_i = c_re[direction], c_im[direction]
        read = lambda hr, hi: (jnp.einsum("ghp,blgp->blgh", c_r, hr) - jnp.einsum("ghp,blgp->blgh", c_i, hi))
        y_dir, yc_dir = read(h_re, h_im), read(hc_re, hc_im)
        if direction == 1:
            y_dir, yc_dir = flip(y_dir), flip(yc_dir)
        y, yc = y + y_dir, yc + yc_dir

    def glu(t):
        z = t.reshape(t.shape[:2] + (S5_C,)) @ glu_w + glu_b
        return z[..., :S5_C] * jax.nn.sigmoid(z[..., S5_C:])

    return glu(y), glu(yc)


def rope_2d_tables(seq_len):
    rows = seq_len // GRID_W
    row = jnp.broadcast_to(jnp.arange(rows, dtype=F32)[:, None], (rows, GRID_W)).reshape(seq_len)
    col = jnp.broadcast_to(jnp.arange(GRID_W, dtype=F32)[None, :], (rows, GRID_W)).reshape(seq_len)
    inv_freq = ROPE_BASE ** (-jnp.arange(ROPE_AXIS // 2, dtype=F32) / (ROPE_AXIS // 2))
    ang = jnp.stack([row[:, None] * inv_freq, col[:, None] * inv_freq], axis=1)
    return jnp.cos(ang), jnp.sin(ang)


def apply_rope_2d(x, cos, sin):
    xs = x.reshape(x.shape[:-1] + (2, 2, ROPE_AXIS // 2))
    x1, x2 = xs[..., 0, :], xs[..., 1, :]
    return jnp.stack([x1 * cos - x2 * sin, x2 * cos + x1 * sin], axis=-2).reshape(x.shape)


def mla_attend(q_nope, q_rope, k_nope, k_rope, v):
    scores = (jnp.einsum("bqhd,bkhd->bhqk", q_nope, k_nope)
              + jnp.einsum("bqhr,bkr->bhqk", q_rope, k_rope)) * (MLA_NOPE + MLA_ROPE) ** -0.5
    p = jax.nn.softmax(scores.astype(F32), axis=-1)
    return jnp.einsum("bhqk,bkhd->bqhd", p, v)


def mla_mixer(cq, ckv, k_rope, cq_c, ckv_c, k_rope_c, qn_g, kvn_g, w_uq, w_ukv, with_ctx_out):
    bsz, seq_len = cq.shape[:2]

    def project(cq_, ckv_):
        q = (rmsnorm(cq_, qn_g) @ w_uq).reshape(cq_.shape[:2] + (MLA_HEADS, MLA_NOPE + MLA_ROPE))
        kv = (rmsnorm(ckv_, kvn_g) @ w_ukv).reshape(ckv_.shape[:2] + (MLA_HEADS, MLA_NOPE + MLA_V))
        return q[..., :MLA_NOPE], q[..., MLA_NOPE:], kv[..., :MLA_NOPE], kv[..., MLA_NOPE:]

    qn, qr, kn, v = project(cq, ckv)
    qn_c, qr_c, kn_c, v_c = project(cq_c, ckv_c)
    cos, sin = rope_2d_tables(seq_len)
    qr = apply_rope_2d(qr, cos[:, None], sin[:, None])
    kr = apply_rope_2d(k_rope.astype(F32), cos, sin)
    kr_c = k_rope_c.astype(F32)
    k_nope_all = jnp.concatenate([kn, kn_c], axis=1)
    k_rope_all = jnp.concatenate([kr, kr_c], axis=1)
    v_all = jnp.concatenate([v, v_c], axis=1)
    n_blocks = seq_len // Q_BLOCK
    blocks = lambda t: jnp.moveaxis(t.reshape((bsz, n_blocks, Q_BLOCK) + t.shape[2:]), 1, 0)
    o = lax.map(lambda qb: mla_attend(qb[0], qb[1], k_nope_all, k_rope_all, v_all), (blocks(qn), blocks(qr)))
    o = jnp.moveaxis(o, 0, 1).reshape(bsz, seq_len, MLA_HEADS * MLA_V)
    if not with_ctx_out:
        return o, None
    oc = mla_attend(qn_c, qr_c, kn_c, kr_c, v_c)
    return o, oc.reshape(oc.shape[:2] + (MLA_HEADS * MLA_V,))


def route(xf, router_w, router_b):
    n_tok = xf.shape[0]
    scores = jax.nn.sigmoid((xf @ router_w).astype(F32))
    biased = scores + router_b.astype(F32)
    grp_score = lax.top_k(biased.reshape(n_tok, N_GROUPS, N_EXPERTS // N_GROUPS), 2)[0].sum(-1)
    _, grp_idx = lax.top_k(grp_score, TOPK_GROUPS)
    grp_mask = jax.nn.one_hot(grp_idx, N_GROUPS, dtype=F32).sum(axis=1) > 0
    exp_mask = jnp.repeat(grp_mask, N_EXPERTS // N_GROUPS, axis=1)
    _, idx = lax.top_k(jnp.where(exp_mask, biased, -jnp.inf), TOP_K)
    w = jnp.take_along_axis(scores, idx, axis=1)
    return idx, w / jnp.sum(w, axis=-1, keepdims=True) * ROUTED_SCALE


def routed_experts(xf, idx, w, w_gate, w_up, w_down):
    n_tok, dm = xf.shape
    n_assign = n_tok * TOP_K
    e_flat = idx.reshape(n_assign)
    order = jnp.argsort(e_flat)
    e_sorted = e_flat[order]
    counts = jnp.bincount(e_flat, length=N_EXPERTS)
    padded = (counts + EXPERT_BLOCK - 1) // EXPERT_BLOCK * EXPERT_BLOCK
    pad_end = jnp.cumsum(padded)
    pad_start = pad_end - padded
    grp_start = jnp.cumsum(counts) - counts
    dest = pad_start[e_sorted] + jnp.arange(n_assign) - grp_start[e_sorted]
    n_blocks = -(-n_assign // EXPERT_BLOCK) + N_EXPERTS
    n_rows = n_blocks * EXPERT_BLOCK
    row_tok = jnp.zeros((n_rows,), jnp.int32).at[dest].set((order // TOP_K).astype(jnp.int32))
    row_w = jnp.zeros((n_rows,), F32).at[dest].set(w.reshape(n_assign)[order])
    blk_expert = jnp.minimum(jnp.searchsorted(pad_end, jnp.arange(n_blocks) * EXPERT_BLOCK, side="right"),
                             N_EXPERTS - 1)

    def body(acc, blk):
        tok, wt, e = blk
        yb = swiglu(xf[tok], w_gate[e], w_up[e], w_down[e]) * wt[:, None]
        return acc.at[tok].add(yb.astype(acc.dtype)), None

    y, _ = lax.scan(body, jnp.zeros((n_tok, dm), F32),
                    (row_tok.reshape(n_blocks, EXPERT_BLOCK), row_w.reshape(n_blocks, EXPERT_BLOCK), blk_expert))
    return y


def moe_ffn(xf, router_w, router_b, w_gate, w_up, w_down, sh_gate, sh_up, sh_down):
    idx, w = route(xf, router_w, router_b)
    return routed_experts(xf, idx, w, w_gate, w_up, w_down) + swiglu(xf, sh_gate, sh_up, sh_down)


def setup_inputs(seed: int = 0) -> dict:
    key = jax.random.key(seed)
    keys = iter(jax.random.split(key, 64))

    def nrm(shape, scale):
        return jax.random.normal(next(keys), shape, F32) * scale

    def gain(shape):
        return 1.0 + nrm(shape, 0.02)

    s5_shape = (DEPTH, 2, S5_NG, S5_P)
    return {
        "x": nrm((BATCH, SEQ, D_MODEL), 1.0),
        "c": nrm((BATCH, D_MODEL), 1.0),
        "ctx": nrm((BATCH, CTX_LEN, D_MODEL), 1.0),
        "c_ctx": nrm((D_MODEL,), 1.0),
        "ada_w": nrm((DEPTH, D_MODEL, 6 * D_MODEL), 0.5 * D_MODEL ** -0.5),
        "ada_b": nrm((DEPTH, 6 * D_MODEL), 0.02),
        "norm1_g": gain((DEPTH, D_MODEL)),
        "norm2_g": gain((DEPTH, D_MODEL)),
        "w_in": nrm((DEPTH, D_MODEL, D_IN), D_MODEL ** -0.5),
        "w_out": nrm((DEPTH, D_MIX, D_MODEL), D_MIX ** -0.5),
        "gla_w_a2": nrm((DEPTH, 2, GLA_LOWRANK, GLA_HEADS * GLA_DK), GLA_LOWRANK ** -0.5),
        "gla_b_a": nrm((DEPTH, 2, GLA_HEADS * GLA_DK), 0.1),
        "gla_norm_g": gain((DEPTH, GLA_HEADS * GLA_DV)),
        "conv_dw_w": nrm((DEPTH, CONV_K, CONV_C), CONV_K ** -0.5),
        "conv_dw_b": nrm((DEPTH, CONV_C), 0.02),
        "conv_ln_g": gain((DEPTH, CONV_C)),
        "conv_ln_b": nrm((DEPTH, CONV_C), 0.02),
        "conv_pw_w": nrm((DEPTH, CONV_C, CONV_C), CONV_C ** -0.5),
        "conv_pw_b": nrm((DEPTH, CONV_C), 0.02),
        "s5_lam_re": -0.5 + nrm(s5_shape, 0.01),
        "s5_lam_im": math.pi * jnp.arange(S5_P, dtype=F32) + nrm(s5_shape, 0.01),
        "s5_log_dt": jax.random.uniform(next(keys), (DEPTH, 2, S5_NG), F32, math.log(1e-3), math.log(1e-1)),
        "s5_b_re": nrm((DEPTH, 2, S5_NG, S5_P, S5_GROUP), (2 * S5_GROUP) ** -0.5),
        "s5_b_im": nrm((DEPTH, 2, S5_NG, S5_P, S5_GROUP), (2 * S5_GROUP) ** -0.5),
        "s5_c_re": nrm((DEPTH, 2, S5_NG, S5_GROUP, S5_P), S5_P ** -0.5),
        "s5_c_im": nrm((DEPTH, 2, S5_NG, S5_GROUP, S5_P), S5_P ** -0.5),
        "s5_d": nrm((DEPTH, S5_C), 1.0),
        "s5_glu_w": nrm((DEPTH, S5_C, 2 * S5_C), S5_C ** -0.5),
        "s5_glu_b": nrm((DEPTH, 2 * S5_C), 0.02),
        "mla_qn_g": gain((DEPTH, MLA_Q_LORA)),
        "mla_kvn_g": gain((DEPTH, MLA_KV_LORA)),
        "mla_w_uq": nrm((DEPTH, MLA_Q_LORA, MLA_HEADS * (MLA_NOPE + MLA_ROPE)), MLA_Q_LORA ** -0.5),
        "mla_w_ukv": nrm((DEPTH, MLA_KV_LORA, MLA_HEADS * (MLA_NOPE + MLA_V)), MLA_KV_LORA ** -0.5),
        "moe_router_w": nrm((DEPTH, D_MODEL, N_EXPERTS), D_MODEL ** -0.5),
        "moe_router_b": nrm((DEPTH, N_EXPERTS), 0.01),
        "moe_w_gate": nrm((DEPTH, N_EXPERTS, D_MODEL, EXPERT_FF), D_MODEL ** -0.5),
        "moe_w_up": nrm((DEPTH, N_EXPERTS, D_MODEL, EXPERT_FF), D_MODEL ** -0.5),
        "moe_w_down": nrm((DEPTH, N_EXPERTS, EXPERT_FF, D_MODEL), EXPERT_FF ** -0.5),
        "shared_w_gate": nrm((DEPTH, D_MODEL, SHARED_FF), D_MODEL ** -0.5),
        "shared_w_up": nrm((DEPTH, D_MODEL, SHARED_FF), D_MODEL ** -0.5),
        "shared_w_down": nrm((DEPTH, SHARED_FF, D_MODEL), SHARED_FF ** -0.5),
        "final_g": gain((D_MODEL,)),
    }


def reference(x, c, ctx, c_ctx, ada_w, ada_b, norm1_g, norm2_g, w_in, w_out,
              gla_w_a2, gla_b_a, gla_norm_g,
              conv_dw_w, conv_dw_b, conv_ln_g, conv_ln_b, conv_pw_w, conv_pw_b,
              s5_lam_re, s5_lam_im, s5_log_dt, s5_b_re, s5_b_im, s5_c_re, s5_c_im, s5_d, s5_glu_w, s5_glu_b,
              mla_qn_g, mla_kvn_g, mla_w_uq, mla_w_ukv,
              moe_router_w, moe_router_b, moe_w_gate, moe_w_up, moe_w_down,
              shared_w_gate, shared_w_up, shared_w_down, final_g):
    bsz, seq_len, dm = x.shape
    ctx_len = ctx.shape[1]
    split_at = np.cumsum(IN_SPLITS)[:-1].tolist()
    xc = ctx
    for l in range(DEPTH):
        last = l == DEPTH - 1
        mod = jax.nn.silu(c.astype(F32)) @ ada_w[l] + ada_b[l]
        mod_c = jax.nn.silu(c_ctx.astype(F32)) @ ada_w[l] + ada_b[l]
        sh1, sc1, g1, sh2, sc2, g2 = jnp.split(mod[:, None, :], 6, axis=-1)
        sh1c, sc1c, g1c, sh2c, sc2c, g2c = jnp.split(mod_c, 6)
        z = jnp.split(modulate(x, norm1_g[l], sh1, sc1) @ w_in[l], split_at, axis=-1)
        zc = jnp.split(modulate(xc, norm1_g[l], sh1c, sc1c) @ w_in[l], split_at, axis=-1)
        conv_p = (conv_dw_w[l], conv_dw_b[l], conv_ln_g[l], conv_ln_b[l], conv_pw_w[l], conv_pw_b[l])
        s5_p = (s5_lam_re[l], s5_lam_im[l], s5_log_dt[l], s5_b_re[l], s5_b_im[l], s5_c_re[l], s5_c_im[l],
                s5_d[l], s5_glu_w[l], s5_glu_b[l])
        moe_p = (moe_router_w[l], moe_router_b[l], moe_w_gate[l], moe_w_up[l], moe_w_down[l],
                 shared_w_gate[l], shared_w_up[l], shared_w_down[l])
        o_gla, oc_gla = gla_mixer(z[:6], zc[:6], gla_w_a2[l], gla_b_a[l], gla_norm_g[l])
        o_conv = conformer_conv(z[6], *conv_p)
        o_s5, oc_s5 = s5_mixer(z[7], zc[7], *s5_p)
        o_mla, oc_mla = mla_mixer(z[8], z[9], z[10], zc[8], zc[9], zc[10],
                                  mla_qn_g[l], mla_kvn_g[l], mla_w_uq[l], mla_w_ukv[l], not last)
        mix = jnp.concatenate([o_gla, o_conv, o_s5, o_mla], axis=-1) @ w_out[l]
        x = x + (g1 * mix).astype(x.dtype)
        h2 = modulate(x, norm2_g[l], sh2, sc2).reshape(bsz * seq_len, dm)
        if last:
            y = moe_ffn(h2, *moe_p)
        else:
            oc_conv = conformer_conv(zc[6], *conv_p)
            mix_c = jnp.concatenate([oc_gla, oc_conv, oc_s5, oc_mla], axis=-1) @ w_out[l]
            xc = xc + (g1c * mix_c).astype(xc.dtype)
            h2c = modulate(xc, norm2_g[l], sh2c, sc2c).reshape(bsz * ctx_len, dm)
            y_all = moe_ffn(jnp.concatenate([h2, h2c], axis=0), *moe_p)
            y = y_all[: bsz * seq_len]
            xc = xc + (g2c * y_all[bsz * seq_len:].reshape(bsz, ctx_len, dm)).astype(xc.dtype)
        x = x + (g2 * y.reshape(bsz, seq_len, dm)).astype(x.dtype)
    return rmsnorm(x, final_g).astype(x.dtype)
```

```python
import functools
import math

import numpy as np
import jax
import jax.numpy as jnp
from jax import lax
from jax.experimental import pallas as pl
from jax.experimental.pallas import tpu as pltpu

F32 = jnp.float32
BF16 = jnp.bfloat16
HIGHEST = lax.Precision.HIGHEST
EPS = 1e-6

GRID_W = 64

GLA_HEADS = 4
GLA_DK = 32
GLA_DV = 64
GLA_LOWRANK = 16
GLA_TAU = 16.0
GLA_CHUNK = 64

CONV_C = 256
CONV_K = 31
CONV_HALO = 16

S5_C = 256
S5_GROUP = 16
S5_NG = 16
S5_P = 64
S5_STATE = S5_NG * S5_P
S5_STRIP = 256

MLA_HEADS = 4
MLA_NOPE = 64
MLA_ROPE = 32
MLA_V = 64
MLA_Q_LORA = 256
MLA_KV_LORA = 128
MLA_HEAD_PAD = 128
ROPE_AXIS = MLA_ROPE // 2
ROPE_BASE = 10000.0

N_EXPERTS = 64
TOP_K = 8
N_GROUPS = 8
TOPK_GROUPS = 4
ROUTED_SCALE = 2.5
GATE_LANES = 128

IN_SPLITS = (128, 128, 256, 256, 16, 16, 512, 256, 256, 128, 32)
W_GLA, W_CONV, W_S5, W_MLA = 1024, 512, 256, 512

VMEM_LIMIT = 48 * 1024 * 1024

NT_DIMS = (((1,), (1,)), ((), ()))
TN_DIMS = (((0,), (0,)), ((), ()))


def _params(*sem, vmem=VMEM_LIMIT):
    return pltpu.CompilerParams(dimension_semantics=sem, vmem_limit_bytes=vmem)


def _silu(v):
    return v * jax.nn.sigmoid(v)


def _dot(a, b, **kw):
    return jnp.dot(a, b, preferred_element_type=F32, **kw)


def _tile(n, pref):
    return pref if n % pref == 0 else n


def _ada_kernel(c_ref, w_ref, b_ref, o_ref):
    o_ref[...] = _dot(_silu(c_ref[...]), w_ref[...], precision=HIGHEST) + b_ref[...]


def ada_mod(cc, w, b):
    rows, dm = cc.shape
    n = w.shape[1]
    tn = _tile(n, 512)
    return pl.pallas_call(
        _ada_kernel, grid=(n // tn,),
        in_specs=[pl.BlockSpec((rows, dm), lambda j: (0, 0)),
                  pl.BlockSpec((dm, tn), lambda j: (0, j)),
                  pl.BlockSpec((1, tn), lambda j: (0, j))],
        out_specs=pl.BlockSpec((rows, tn), lambda j: (0, j)),
        out_shape=jax.ShapeDtypeStruct((rows, n), F32),
        compiler_params=_params("arbitrary"), name="ada_mod")(cc, w, b.reshape(1, n))


def _inproj_kernel(x_ref, sh_ref, sc_ref, g_ref, w_ref, *o_refs):
    x = x_ref[0]
    ms = jnp.mean(x * x, axis=-1, keepdims=True)
    h = (x * lax.rsqrt(ms + EPS) * g_ref[...] * (1.0 + sc_ref[0]) + sh_ref[0]).astype(BF16)
    off = 0
    for o_ref in o_refs:
        w = o_ref.shape[-1]
        o_ref[0] = _dot(h, w_ref[:, off:off + w])
        off += w


def in_projection(x, shift, scale, g, w):
    bsz, seq, dm = x.shape
    tm = _tile(seq, 512)
    widths = (W_GLA, W_CONV, W_S5, W_MLA)
    vec = pl.BlockSpec((1, 1, dm), lambda b, i: (b, 0, 0))
    return pl.pallas_call(
        _inproj_kernel, grid=(bsz, seq // tm),
        in_specs=[pl.BlockSpec((1, tm, dm), lambda b, i: (b, i, 0)), vec, vec,
                  pl.BlockSpec((1, dm), lambda b, i: (0, 0)),
                  pl.BlockSpec(w.shape, lambda b, i: (0, 0))],
        out_specs=[pl.BlockSpec((1, tm, wd), lambda b, i: (b, i, 0)) for wd in widths],
        out_shape=[jax.ShapeDtypeStruct((bsz, seq, wd), F32) for wd in widths],
        compiler_params=_params("parallel", "parallel"), name="in_projection")(x, shift, scale, g.reshape(1, dm), w)


def _rope_swap(t):
    s = t.reshape(t.shape[:-1] + (2, 2, ROPE_AXIS // 2))
    return jnp.stack([-s[..., 1, :], s[..., 0, :]], axis=-2).reshape(t.shape)


def assemble_w_in(w_in):
    edges = np.cumsum((0,) + IN_SPLITS)
    q, k, v, r, a_f, a_b, conv, s5, cq, ckv, kr = [w_in[:, edges[i]:edges[i + 1]] for i in range(len(IN_SPLITS))]
    pad = lambda t, w: jnp.pad(t, ((0, 0), (0, w - t.shape[1])))
    cols = [q, k, v, r, pad(a_f, 128), pad(a_b, 128), conv, s5, cq, ckv, pad(jnp.concatenate([kr, _rope_swap(kr)], 1), 128)]
    return jnp.concatenate(cols, axis=1).astype(BF16)


def _gla_kernel(z_ref, zc_ref, wa_ref, ba_ref, g_ref, o_ref, oc_ref, of_scr, st_scr, *, seq, ctx_len):
    c = GLA_CHUNK
    nh = GLA_HEADS
    kw = nh * GLA_DK
    vw = nh * GLA_DV
    iota = lambda shape, d: lax.broadcasted_iota(jnp.int32, shape, d)
    tri = [(iota((c, c), 1) <= iota((c, c), 0)).astype(F32), (iota((c, c), 1) >= iota((c, c), 0)).astype(F32)]
    r4 = iota((nh * c, c), 0) & (c - 1)
    c4 = iota((nh * c, c), 1)
    causal = [c4 <= r4, c4 >= r4]
    k_of = lambda t: t >> int(math.log2(GLA_DK))
    v_of = lambda t: t >> int(math.log2(GLA_DV))
    k_head = [(k_of(iota((1, kw), 1)) == h).astype(F32) for h in range(nh)]
    v_head = [(v_of(iota((1, vw), 1)) == h).astype(F32) for h in range(nh)]
    st_mask = (v_of(iota((vw, kw), 0)) == k_of(iota((vw, kw), 1))).astype(F32)
    head_mean = (v_of(iota((vw, vw), 0)) == v_of(iota((vw, vw), 1))).astype(F32) * (1.0 / GLA_DV)

    def chunk(ref, s, d):
        blk = ref[0, pl.ds(s, c), :]
        q = blk[:, 0:kw] * (GLA_DK ** -0.5)
        k = blk[:, kw:2 * kw]
        v = blk[:, 2 * kw:2 * kw + vw]
        r = blk[:, 2 * kw + vw:2 * kw + 2 * vw]
        a0 = 2 * kw + 2 * vw + 128 * d
        zl = _dot(blk[:, a0:a0 + 128], wa_ref[d], precision=HIGHEST) + ba_ref[d]
        la = (jnp.minimum(zl, 0.0) - jnp.log1p(jnp.exp(-jnp.abs(zl)))) / GLA_TAU
        b = _dot(tri[d], la, precision=HIGHEST)
        b_end = b[c - 1:c, :] if d == 0 else b[0:1, :]
        q_dec = q * jnp.exp(b)
        k_inv = (k * jnp.exp(-b)).astype(BF16)
        k_tail = (k * jnp.exp(b_end - b)).astype(BF16)
        vb = v.astype(BF16)
        qs = jnp.concatenate([q_dec * k_head[h] for h in range(nh)], axis=0).astype(BF16)
        att = lax.dot_general(qs, k_inv, NT_DIMS, preferred_element_type=F32)
        att = jnp.where(causal[d], att, 0.0).astype(BF16)
        o_all = _dot(att, vb)
        o = o_all[0:c] * v_head[0]
        for h in range(1, nh):
            o = o + o_all[h * c:(h + 1) * c] * v_head[h]
        st = st_scr[...]
        o = o + lax.dot_general(q_dec.astype(BF16), st.astype(BF16), NT_DIMS, preferred_element_type=F32)
        upd = lax.dot_general(vb, k_tail, TN_DIMS, preferred_element_type=F32)
        st_scr[...] = st * jnp.exp(b_end) + upd * st_mask
        return o, r

    def finish(o, r):
        ms = _dot(o * o, head_mean, precision=HIGHEST)
        return o * lax.rsqrt(ms + EPS) * g_ref[...] * _silu(r)

    n_ctx, n_seq = ctx_len // c, seq // c

    st_scr[...] = jnp.zeros_like(st_scr)

    @pl.loop(0, n_ctx)
    def _(n):
        s = pl.multiple_of(n * c, c)
        of_scr[pl.ds(s, c), :] = chunk(zc_ref, s, 0)[0]

    @pl.loop(0, n_seq)
    def _(n):
        s = pl.multiple_of(n * c, c)
        of_scr[pl.ds(pl.multiple_of(ctx_len + s, c), c), :] = chunk(z_ref, s, 0)[0]

    st_scr[...] = jnp.zeros_like(st_scr)

    @pl.loop(0, n_ctx)
    def _(n):
        s = pl.multiple_of((n_ctx - 1 - n) * c, c)
        o, r = chunk(zc_ref, s, 1)
        oc_ref[0, pl.ds(s, c), :] = finish(o + of_scr[pl.ds(s, c), :], r)

    @pl.loop(0, n_seq)
    def _(n):
        s = pl.multiple_of((n_seq - 1 - n) * c, c)
        o, r = chunk(z_ref, s, 1)
        o_ref[0, pl.ds(s, c), :] = finish(o + of_scr[pl.ds(pl.multiple_of(ctx_len + s, c), c), :], r)


def gla_mixer(zg, zgc, w_a2, b_a, norm_g):
    bsz, seq, wd = zg.shape
    ctx_len = zgc.shape[1]
    vw = GLA_HEADS * GLA_DV
    kw = GLA_HEADS * GLA_DK
    wa = jnp.pad(w_a2, ((0, 0), (0, 128 - GLA_LOWRANK), (0, 0)))
    full = lambda shape: pl.BlockSpec(shape, lambda b: (0,) * len(shape))
    return pl.pallas_call(
        functools.partial(_gla_kernel, seq=seq, ctx_len=ctx_len), grid=(bsz,),
        in_specs=[pl.BlockSpec((1, seq, wd), lambda b: (b, 0, 0)),
                  pl.BlockSpec((1, ctx_len, wd), lambda b: (b, 0, 0)),
                  full((2, 128, kw)), full((2, 1, kw)), full((1, vw))],
        out_specs=[pl.BlockSpec((1, seq, vw), lambda b: (b, 0, 0)),
                   pl.BlockSpec((1, ctx_len, vw), lambda b: (b, 0, 0))],
        out_shape=[jax.ShapeDtypeStruct((bsz, seq, vw), F32), jax.ShapeDtypeStruct((bsz, ctx_len, vw), F32)],
        scratch_shapes=[pltpu.VMEM((seq + ctx_len, vw), F32), pltpu.VMEM((vw, kw), F32)],
        compiler_params=_params("parallel"), name="gla_mixer")(
            zg, zgc, wa, b_a.reshape(2, 1, kw), norm_g.reshape(1, vw))


def _conv_kernel(u_ref, dw_ref, dwb_ref, lng_ref, lnb_ref, pw_ref, pwb_ref, o_ref, h_scr, *, seq, rows):
    halo = CONV_HALO
    h_scr[0:halo, :] = jnp.zeros((halo, CONV_C), F32)
    h_scr[halo + seq:2 * halo + seq, :] = jnp.zeros((halo, CONV_C), F32)

    @pl.loop(0, seq // rows)
    def _(t):
        s = pl.multiple_of(t * rows, rows)
        u = u_ref[0, pl.ds(s, rows), :]
        h_scr[pl.ds(pl.multiple_of(halo + s, 8), rows), :] = u[:, :CONV_C] * jax.nn.sigmoid(u[:, CONV_C:])

    @pl.loop(0, seq // rows)
    def _(t):
        s = pl.multiple_of(t * rows, rows)
        win = h_scr[pl.ds(s, rows + 2 * halo), :]
        first = halo - CONV_K // 2
        acc = win[first:first + rows] * dw_ref[0:1, :]
        for k in range(1, CONV_K):
            acc = acc + win[first + k:first + k + rows] * dw_ref[k:k + 1, :]
        acc = acc + dwb_ref[...]
        mu = jnp.mean(acc, axis=-1, keepdims=True)
        var = jnp.mean(jnp.square(acc - mu), axis=-1, keepdims=True)
        y = _silu((acc - mu) * lax.rsqrt(var + EPS) * lng_ref[...] + lnb_ref[...])
        o_ref[0, pl.ds(s, rows), :] = _dot(y.astype(BF16), pw_ref[...]) + pwb_ref[...]


def conformer_conv(u, dw_w, dw_b, ln_g, ln_b, pw_w, pw_b):
    bsz, seq, _ = u.shape
    rows = _tile(seq, 128)
    full = lambda shape: pl.BlockSpec(shape, lambda b: (0,) * len(shape))
    row = lambda t: t.reshape(1, CONV_C)
    return pl.pallas_call(
        functools.partial(_conv_kernel, seq=seq, rows=rows), grid=(bsz,),
        in_specs=[pl.BlockSpec((1, seq, 2 * CONV_C), lambda b: (b, 0, 0)),
                  full((CONV_K + 1, CONV_C)), full((1, CONV_C)), full((1, CONV_C)), full((1, CONV_C)),
                  full((CONV_C, CONV_C)), full((1, CONV_C))],
        out_specs=pl.BlockSpec((1, seq, CONV_C), lambda b: (b, 0, 0)),
        out_shape=jax.ShapeDtypeStruct((bsz, seq, CONV_C), F32),
        scratch_shapes=[pltpu.VMEM((seq + 2 * CONV_HALO, CONV_C), F32)],
        compiler_params=_params("parallel"), name="conformer_conv")(
            u, jnp.pad(dw_w, ((0, 1), (0, 0))), row(dw_b), row(ln_g), row(ln_b), pw_w.astype(BF16), row(pw_b))


def s5_matrices(lam_re, lam_im, log_dt, b_re, b_im, c_re, c_im):
    dt = jnp.exp(log_dt)[:, None]
    mag = jnp.exp(lam_re * dt)
    a_re, a_im = mag * jnp.cos(lam_im * dt), mag * jnp.sin(lam_im * dt)
    den = lam_re * lam_re + lam_im * lam_im
    f_re = ((a_re - 1.0) * lam_re + a_im * lam_im) / den
    f_im = (a_im * lam_re - (a_re - 1.0) * lam_im) / den
    bb_re = f_re[..., None] * b_re - f_im[..., None] * b_im
    bb_im = f_re[..., None] * b_im + f_im[..., None] * b_re
    eye = jnp.eye(S5_NG, dtype=F32)
    blk_b = lambda t: jnp.einsum("gph,gk->ghkp", t, eye).reshape(S5_C, S5_STATE)
    blk_c = lambda t: jnp.einsum("ghp,gk->gpkh", t, eye).reshape(S5_STATE, S5_C)
    a = jnp.stack([a_re.reshape(S5_STATE), a_im.reshape(S5_STATE)])
    b_mat = jnp.concatenate([blk_b(bb_re), blk_b(bb_im)], axis=1).astype(BF16)
    c_mat = jnp.concatenate([blk_c(c_re), -blk_c(c_im)], axis=0).astype(BF16)
    return a, b_mat, c_mat


def _s5_scan_kernel(u_ref, a_ref, b_ref, c_ref, h0_ref, y_ref, hl_ref, hs_scr, st_scr, *, steps, bsz, reverse):
    i = pl.program_id(0)
    ns = S5_STATE

    @pl.when(i == 0)
    def _():
        st_scr[...] = h0_ref[...]

    u = u_ref[...].reshape(steps * bsz, S5_C).astype(BF16)
    hs_scr[...] = _dot(u, b_ref[...])
    for s0 in range(0, ns, S5_STRIP):
        a_re = a_ref[0:1, s0:s0 + S5_STRIP]
        a_im = a_ref[1:2, s0:s0 + S5_STRIP]

        def step(j, carry):
            h_re, h_im = carry
            t = (steps - 1 - j) if reverse else j
            r0 = pl.multiple_of(t * bsz, bsz)
            n_re = a_re * h_re - a_im * h_im + hs_scr[pl.ds(r0, bsz), s0:s0 + S5_STRIP]
            n_im = a_re * h_im + a_im * h_re + hs_scr[pl.ds(r0, bsz), ns + s0:ns + s0 + S5_STRIP]
            hs_scr[pl.ds(r0, bsz), s0:s0 + S5_STRIP] = n_re
            hs_scr[pl.ds(r0, bsz), ns + s0:ns + s0 + S5_STRIP] = n_im
            return n_re, n_im

        h_re, h_im = lax.fori_loop(0, steps, step, (st_scr[:, s0:s0 + S5_STRIP], st_scr[:, ns + s0:ns + s0 + S5_STRIP]))
        st_scr[:, s0:s0 + S5_STRIP] = h_re
        st_scr[:, ns + s0:ns + s0 + S5_STRIP] = h_im
    y_ref[...] = _dot(hs_scr[...].astype(BF16), c_ref[...]).reshape(steps, bsz, S5_C)
    hl_ref[...] = st_scr[...]


def s5_scan(u_tm, mats, h0, reverse):
    a, b_mat, c_mat = mats
    seq, bsz, _ = u_tm.shape
    steps = _tile(seq, 32)
    n = seq // steps
    order = (lambda i: (n - 1 - i, 0, 0)) if reverse else (lambda i: (i, 0, 0))
    full = lambda shape: pl.BlockSpec(shape, lambda i: (0,) * len(shape))
    return pl.pallas_call(
        functools.partial(_s5_scan_kernel, steps=steps, bsz=bsz, reverse=reverse), grid=(n,),
        in_specs=[pl.BlockSpec((steps, bsz, S5_C), order), full(a.shape), full(b_mat.shape), full(c_mat.shape),
                  full(h0.shape)],
        out_specs=[pl.BlockSpec((steps, bsz, S5_C), order), full(h0.shape)],
        out_shape=[jax.ShapeDtypeStruct((seq, bsz, S5_C), F32), jax.ShapeDtypeStruct(h0.shape, F32)],
        scratch_shapes=[pltpu.VMEM((steps * bsz, 2 * S5_STATE), F32), pltpu.VMEM(h0.shape, F32)],
        compiler_params=_params("arbitrary"), name="s5_scan")(u_tm, a, b_mat, c_mat, h0)


def _s5_out_kernel(u_ref, yf_ref, yb_ref, d_ref, w_ref, b_ref, o_ref):
    y = d_ref[...] * u_ref[...] + yf_ref[...] + yb_ref[...]
    z = _dot(y.astype(BF16), w_ref[...]) + b_ref[...]
    o_ref[...] = z[:, :S5_C] * jax.nn.sigmoid(z[:, S5_C:])


def s5_output(u, y_f, y_b, d_skip, glu_w, glu_b):
    n = u.shape[0]
    tm = _tile(n, 1024)
    rows = pl.BlockSpec((tm, S5_C), lambda i: (i, 0))
    full = lambda shape: pl.BlockSpec(shape, lambda i: (0,) * len(shape))
    return pl.pallas_call(
        _s5_out_kernel, grid=(n // tm,),
        in_specs=[rows, rows, rows, full((1, S5_C)), full((S5_C, 2 * S5_C)), full((1, 2 * S5_C))],
        out_specs=rows, out_shape=jax.ShapeDtypeStruct((n, S5_C), F32),
        compiler_params=_params("parallel"), name="s5_output")(
            u, y_f, y_b, d_skip.reshape(1, S5_C), glu_w.astype(BF16), glu_b.reshape(1, 2 * S5_C))


def s5_mixer(zs, zsc, lam_re, lam_im, log_dt, b_re, b_im, c_re, c_im, d_skip, glu_w, glu_b):
    bsz, seq, _ = zs.shape
    ctx_len = zsc.shape[1]
    u = jnp.transpose(zs, (1, 0, 2))
    uc = jnp.transpose(zsc, (1, 0, 2))
    zeros = jnp.zeros((bsz, 2 * S5_STATE), F32)
    ys, ycs = [], []
    for d in range(2):
        mats = s5_matrices(lam_re[d], lam_im[d], log_dt[d], b_re[d], b_im[d], c_re[d], c_im[d])
        yc, hc = s5_scan(uc, mats, zeros, reverse=d == 1)
        y, _ = s5_scan(u, mats, hc, reverse=d == 1)
        ys.append(y)
        ycs.append(yc)
    flat = lambda t: t.reshape(-1, S5_C)
    o = s5_output(flat(u), flat(ys[0]), flat(ys[1]), d_skip, glu_w, glu_b).reshape(seq, bsz, S5_C)
    oc = s5_output(flat(uc), flat(ycs[0]), flat(ycs[1]), d_skip, glu_w, glu_b).reshape(ctx_len, bsz, S5_C)
    return jnp.transpose(o, (1, 0, 2)), jnp.transpose(oc, (1, 0, 2))


def rope_tables(seq_len, rotate):
    if not rotate:
        return jnp.ones((seq_len, MLA_ROPE), F32), jnp.zeros((seq_len, MLA_ROPE), F32)
    rows = seq_len // GRID_W
    row = jnp.broadcast_to(jnp.arange(rows, dtype=F32)[:, None], (rows, GRID_W)).reshape(seq_len)
    col = jnp.broadcast_to(jnp.arange(GRID_W, dtype=F32)[None, :], (rows, GRID_W)).reshape(seq_len)
    inv_freq = ROPE_BASE ** (-jnp.arange(ROPE_AXIS // 2, dtype=F32) / (ROPE_AXIS // 2))
    ang = jnp.stack([row[:, None] * inv_freq, col[:, None] * inv_freq], axis=1)
    full = lambda t: jnp.broadcast_to(t[:, :, None, :], (seq_len, 2, 2, ROPE_AXIS // 2)).reshape(seq_len, MLA_ROPE)
    return full(jnp.cos(ang)), full(jnp.sin(ang))


def mla_tables(seq_len, rotate):
    cos, sin = rope_tables(seq_len, rotate)
    head = lambda rope, fill: jnp.concatenate(
        [jnp.full((seq_len, MLA_NOPE), fill, F32), rope, jnp.zeros((seq_len, MLA_HEAD_PAD - MLA_NOPE - MLA_ROPE), F32)], 1)
    cos_q = jnp.tile(head(cos, 1.0), (1, MLA_HEADS))
    sin_q = jnp.tile(head(sin, 0.0), (1, MLA_HEADS))
    cs_k = jnp.concatenate([cos, sin, jnp.zeros((seq_len, 128 - 2 * MLA_ROPE), F32)], 1)
    return cos_q, sin_q, cs_k


def mla_weights(w_uq, w_ukv):
    hd = MLA_NOPE + MLA_ROPE
    zq = jnp.zeros((MLA_Q_LORA, MLA_HEAD_PAD - hd), F32)
    zn = jnp.zeros((MLA_Q_LORA, MLA_NOPE), F32)
    wq, wq_sw, wk, wv = [], [], [], []
    for h in range(MLA_HEADS):
        qh = w_uq[:, h * hd:(h + 1) * hd]
        wq += [qh, zq]
        wq_sw += [zn, _rope_swap(qh[:, MLA_NOPE:]), zq]
        kvh = w_ukv[:, h * (MLA_NOPE + MLA_V):(h + 1) * (MLA_NOPE + MLA_V)]
        wk += [kvh[:, :MLA_NOPE], jnp.zeros((MLA_KV_LORA, MLA_HEAD_PAD - MLA_NOPE), F32)]
        wv += [kvh[:, MLA_NOPE:]]
    place = np.zeros((128, MLA_HEADS * MLA_HEAD_PAD), np.float32)
    for h in range(MLA_HEADS):
        for j in range(MLA_ROPE):
            place[j, h * MLA_HEAD_PAD + MLA_NOPE + j] = 1.0
            place[MLA_ROPE + j, h * MLA_HEAD_PAD + MLA_NOPE + j] = 1.0
    cat = lambda ts: jnp.concatenate(ts, axis=1).astype(BF16)
    return cat(wq), cat(wq_sw), cat(wk), cat(wv), jnp.asarray(place)


def _mla_prep_kernel(z_ref, qg_ref, kg_ref, wq_ref, wqs_ref, wk_ref, wv_ref, pl_ref, cq_ref, sq_ref, csk_ref,
                     q_ref, k_ref, v_ref):
    z = z_ref[0]
    norm = lambda t, g: (t * lax.rsqrt(jnp.mean(t * t, axis=-1, keepdims=True) + EPS) * g).astype(BF16)
    cq = norm(z[:, :MLA_Q_LORA], qg_ref[...])
    ckv = norm(z[:, MLA_Q_LORA:MLA_Q_LORA + MLA_KV_LORA], kg_ref[...])
    scale = (MLA_NOPE + MLA_ROPE) ** -0.5
    q = _dot(cq, wq_ref[...]) * cq_ref[...] + _dot(cq, wqs_ref[...]) * sq_ref[...]
    q_ref[0] = (q * scale).astype(BF16)
    kr = z[:, MLA_Q_LORA + MLA_KV_LORA:] * csk_ref[...]
    k_ref[0] = (_dot(ckv, wk_ref[...]) + _dot(kr, pl_ref[...], precision=HIGHEST)).astype(BF16)
    v_ref[0] = _dot(ckv, wv_ref[...]).astype(BF16)


def mla_prep(zm, qn_g, kvn_g, weights, tables):
    bsz, seq, wd = zm.shape
    tm = _tile(seq, 512)
    wq, wq_sw, wk, wv, place = weights
    cos_q, sin_q, cs_k = tables
    qw, vw = MLA_HEADS * MLA_HEAD_PAD, MLA_HEADS * MLA_V
    full = lambda t: pl.BlockSpec(t.shape, lambda b, i: (0,) * t.ndim)
    pos = lambda t: pl.BlockSpec((tm, t.shape[1]), lambda b, i: (i, 0))
    out = lambda w: pl.BlockSpec((1, tm, w), lambda b, i: (b, i, 0))
    qg, kg = qn_g.reshape(1, -1), kvn_g.reshape(1, -1)
    return pl.pallas_call(
        _mla_prep_kernel, grid=(bsz, seq // tm),
        in_specs=[pl.BlockSpec((1, tm, wd), lambda b, i: (b, i, 0)), full(qg), full(kg), full(wq), full(wq_sw),
                  full(wk), full(wv), full(place), pos(cos_q), pos(sin_q), pos(cs_k)],
        out_specs=[out(qw), out(qw), out(vw)],
        out_shape=[jax.ShapeDtypeStruct((bsz, seq, qw), BF16), jax.ShapeDtypeStruct((bsz, seq, qw), BF16),
                   jax.ShapeDtypeStruct((bsz, seq, vw), BF16)],
        compiler_params=_params("parallel", "parallel"), name="mla_prep")(
            zm, qg, kg, wq, wq_sw, wk, wv, place, cos_q, sin_q, cs_k)


def _mla_attn_kernel(q_ref, *refs, n_seg):
    k_refs, v_refs, o_ref = refs[:n_seg], refs[n_seg:2 * n_seg], refs[2 * n_seg]
    outs = []
    for h in range(MLA_HEADS):
        q = q_ref[0, :, h * MLA_HEAD_PAD:(h + 1) * MLA_HEAD_PAD]
        s = [lax.dot_general(q, k_ref[0, :, h * MLA_HEAD_PAD:(h + 1) * MLA_HEAD_PAD], NT_DIMS,
                             preferred_element_type=F32) for k_ref in k_refs]
        m = functools.reduce(jnp.maximum, [jnp.max(t, axis=-1, keepdims=True) for t in s])
        p = [jnp.exp(t - m) for t in s]
        den = functools.reduce(jnp.add, [jnp.sum(t, axis=-1, keepdims=True) for t in p])
        o = functools.reduce(jnp.add, [_dot(t.astype(BF16), v_ref[0, :, h * MLA_V:(h + 1) * MLA_V])
                                       for t, v_ref in zip(p, v_refs)])
        outs.append(o / den)
    o_ref[0] = jnp.concatenate(outs, axis=-1)


def mla_attention(q, ks, vs):
    bsz, seq, qw = q.shape
    tq = _tile(seq, 256)
    vw = MLA_HEADS * MLA_V
    seg = lambda t: pl.BlockSpec((1,) + t.shape[1:], lambda b, i: (b, 0, 0))
    return pl.pallas_call(
        functools.partial(_mla_attn_kernel, n_seg=len(ks)), grid=(bsz, seq // tq),
        in_specs=[pl.BlockSpec((1, tq, qw), lambda b, i: (b, i, 0))] + [seg(t) for t in ks] + [seg(t) for t in vs],
        out_specs=pl.BlockSpec((1, tq, vw), lambda b, i: (b, i, 0)),
        out_shape=jax.ShapeDtypeStruct((bsz, seq, vw), F32),
        compiler_params=_params("parallel", "parallel"), name="mla_attention")(q, *ks, *vs)


def _outproj_kernel(a_ref, b_ref, c_ref, d_ref, w_ref, x_ref, g1_ref, sh_ref, sc_ref, g_ref, rw_ref,
                    xo_ref, h_ref, lg_ref):
    mix = None
    for j, o_ref in enumerate((a_ref, b_ref, c_ref, d_ref)):
        wd = o_ref.shape[-1]
        part = _dot(o_ref[0].astype(BF16), w_ref[j * wd:(j + 1) * wd, :])
        mix = part if mix is None else mix + part
    x = x_ref[0] + g1_ref[0] * mix
    xo_ref[0] = x
    ms = jnp.mean(x * x, axis=-1, keepdims=True)
    h = x * lax.rsqrt(ms + EPS) * g_ref[...] * (1.0 + sc_ref[0]) + sh_ref[0]
    h_ref[0] = h.astype(BF16)
    lg_ref[...] = lax.dot_general(rw_ref[...], h, NT_DIMS, precision=HIGHEST, preferred_element_type=F32)


def out_projection(parts, w_out, x, g1, shift, scale, g, router_w):
    bsz, seq, dm = x.shape
    tm = _tile(seq, 512)
    nt = seq // tm
    ne = router_w.shape[1]
    rows = lambda w: pl.BlockSpec((1, tm, w), lambda b, i: (b, i, 0))
    vec = pl.BlockSpec((1, 1, dm), lambda b, i: (b, 0, 0))
    full = lambda shape: pl.BlockSpec(shape, lambda b, i: (0,) * len(shape))
    return pl.pallas_call(
        _outproj_kernel, grid=(bsz, nt),
        in_specs=[rows(p.shape[-1]) for p in parts] + [full(w_out.shape), rows(dm), vec, vec, vec, full((1, dm)),
                                                       full((ne, dm))],
        out_specs=[rows(dm), rows(dm), pl.BlockSpec((ne, tm), lambda b, i: (0, b * nt + i))],
        out_shape=[jax.ShapeDtypeStruct((bsz, seq, dm), F32), jax.ShapeDtypeStruct((bsz, seq, dm), BF16),
                   jax.ShapeDtypeStruct((ne, bsz * seq), F32)],
        compiler_params=_params("parallel", "parallel"), name="out_projection")(
            *parts, w_out.astype(BF16), x, g1, shift, scale, g.reshape(1, dm), router_w.T)


def _route_kernel(lg_ref, b_ref, o_ref):
    ne, tt = lg_ref.shape
    per = ne // N_GROUPS
    neg = -jnp.inf
    scores = jax.nn.sigmoid(lg_ref[...])
    biased = scores + b_ref[...]
    v3 = biased.reshape(N_GROUPS, per, tt)
    e_in = lax.broadcasted_iota(jnp.int32, v3.shape, 1).astype(F32)
    m1 = jnp.max(v3, axis=1, keepdims=True)
    i1 = jnp.min(jnp.where(v3 == m1, e_in, float(per)), axis=1, keepdims=True)
    m2 = jnp.max(jnp.where(e_in == i1, neg, v3), axis=1, keepdims=True)
    grp = (m1 + m2).reshape(N_GROUPS, tt)

    def pick(cur, count):
        ids = lax.broadcasted_iota(jnp.int32, cur.shape, 0).astype(F32)
        marks = jnp.zeros(cur.shape, F32)
        for _ in range(count):
            m = jnp.max(cur, axis=0, keepdims=True)
            hit = ids == jnp.min(jnp.where(cur == m, ids, float(cur.shape[0])), axis=0, keepdims=True)
            marks = jnp.where(hit, 1.0, marks)
            cur = jnp.where(hit, neg, cur)
        return marks

    grp_on = pick(grp, TOPK_GROUPS)
    exp_on = jnp.broadcast_to(grp_on.reshape(N_GROUPS, 1, tt), v3.shape).reshape(ne, tt)
    chosen = pick(jnp.where(exp_on > 0.0, biased, neg), TOP_K)
    w = scores * chosen
    o_ref[...] = w / jnp.sum(w, axis=0, keepdims=True) * ROUTED_SCALE


def route(logits_t, router_b):
    ne, n = logits_t.shape
    tt = _tile(n, 1024)
    return pl.pallas_call(
        _route_kernel, grid=(n // tt,),
        in_specs=[pl.BlockSpec((ne, tt), lambda i: (0, i)), pl.BlockSpec((ne, 1), lambda i: (0, 0))],
        out_specs=pl.BlockSpec((ne, tt), lambda i: (0, i)),
        out_shape=jax.ShapeDtypeStruct((ne, n), F32),
        compiler_params=_params("parallel"), name="route")(logits_t, router_b.reshape(ne, 1))


def _moe_kernel(h_ref, gate_ref, wg_ref, wu_ref, wd_ref, x_ref, g2_ref, fg_ref, o_ref, acc_scr, *, final_norm):
    e = pl.program_id(1)

    @pl.when(e == 0)
    def _():
        acc_scr[...] = jnp.zeros_like(acc_scr)

    h = h_ref[...]
    lane = lax.broadcasted_iota(jnp.int32, gate_ref.shape, 1)
    gate = jnp.sum(jnp.where(lane == e, gate_ref[...], 0.0), axis=1, keepdims=True)
    act = _silu(_dot(h, wg_ref[0])) * _dot(h, wu_ref[0]) * gate
    acc_scr[...] += _dot(act.astype(BF16), wd_ref[0])

    @pl.when(e == pl.num_programs(1) - 1)
    def _():
        x = x_ref[...] + g2_ref[0] * acc_scr[...]
        if final_norm:
            x = x * lax.rsqrt(jnp.mean(x * x, axis=-1, keepdims=True) + EPS) * fg_ref[...]
        o_ref[...] = x


def moe_residual(h2, gates, wg, wu, wd, x, g2, final_g, final_norm):
    bsz, seq, dm = x.shape
    n = bsz * seq
    tm = _tile(seq, 1024)
    nt = seq // tm
    ne, _, ff = wg.shape
    rows = lambda w: pl.BlockSpec((tm, w), lambda i, e: (i, 0))
    out = pl.pallas_call(
        functools.partial(_moe_kernel, final_norm=final_norm), grid=(n // tm, ne),
        in_specs=[rows(dm), rows(GATE_LANES),
                  pl.BlockSpec((1, dm, ff), lambda i, e: (e, 0, 0)),
                  pl.BlockSpec((1, dm, ff), lambda i, e: (e, 0, 0)),
                  pl.BlockSpec((1, ff, dm), lambda i, e: (e, 0, 0)),
                  rows(dm), pl.BlockSpec((1, 1, dm), lambda i, e: (i // nt, 0, 0)),
                  pl.BlockSpec((1, dm), lambda i, e: (0, 0))],
        out_specs=rows(dm), out_shape=jax.ShapeDtypeStruct((n, dm), F32),
        scratch_shapes=[pltpu.VMEM((tm, dm), F32)],
        compiler_params=_params("parallel", "arbitrary"), name="moe_residual")(
            h2.reshape(n, dm), gates, wg, wu, wd, x.reshape(n, dm), g2, final_g.reshape(1, dm))
    return out.reshape(bsz, seq, dm)


def gate_matrix(gates_t):
    ne, n = gates_t.shape
    return jnp.concatenate([gates_t.T, jnp.ones((n, 1), F32), jnp.zeros((n, GATE_LANES - ne - 1), F32)], axis=1)


def kernel(x, c, ctx, c_ctx, ada_w, ada_b, norm1_g, norm2_g, w_in, w_out, gla_w_a2, gla_b_a, gla_norm_g,
           conv_dw_w, conv_dw_b, conv_ln_g, conv_ln_b, conv_pw_w, conv_pw_b,
           s5_lam_re, s5_lam_im, s5_log_dt, s5_b_re, s5_b_im, s5_c_re, s5_c_im, s5_d, s5_glu_w, s5_glu_b,
           mla_qn_g, mla_kvn_g, mla_w_uq, mla_w_ukv,
           moe_router_w, moe_router_b, moe_w_gate, moe_w_up, moe_w_down,
           shared_w_gate, shared_w_up, shared_w_down, final_g):
    bsz, seq, dm = x.shape
    ctx_len = ctx.shape[1]
    depth = ada_w.shape[0]
    cc = jnp.concatenate([c, c_ctx[None], jnp.zeros((-(bsz + 1) % 8, dm), F32)], axis=0)
    tables = mla_tables(seq, True)
    tables_c = mla_tables(ctx_len, False)
    xc = ctx
    for l in range(depth):
        last = l == depth - 1
        mod = ada_mod(cc, ada_w[l], ada_b[l])
        lat = [t.reshape(bsz, 1, dm) for t in jnp.split(mod[:bsz], 6, axis=-1)]
        con = [jnp.broadcast_to(t.reshape(1, 1, dm), (bsz, 1, dm)) for t in jnp.split(mod[bsz], 6)]
        sh1, sc1, g1, sh2, sc2, g2 = lat
        sh1c, sc1c, g1c, sh2c, sc2c, g2c = con
        w_in_l = assemble_w_in(w_in[l])
        zg, zv, zs, zm = in_projection(x, sh1, sc1, norm1_g[l], w_in_l)
        zgc, zvc, zsc, zmc = in_projection(xc, sh1c, sc1c, norm1_g[l], w_in_l)
        conv_p = (conv_dw_w[l], conv_dw_b[l], conv_ln_g[l], conv_ln_b[l], conv_pw_w[l], conv_pw_b[l])
        o_gla, oc_gla = gla_mixer(zg, zgc, gla_w_a2[l], gla_b_a[l], gla_norm_g[l])
        o_conv = conformer_conv(zv, *conv_p)
        o_s5, oc_s5 = s5_mixer(zs, zsc, s5_lam_re[l], s5_lam_im[l], s5_log_dt[l], s5_b_re[l], s5_b_im[l],
                               s5_c_re[l], s5_c_im[l], s5_d[l], s5_glu_w[l], s5_glu_b[l])
        mla_w = mla_weights(mla_w_uq[l], mla_w_ukv[l])
        q, k, v = mla_prep(zm, mla_qn_g[l], mla_kvn_g[l], mla_w, tables)
        qc, kc, vc = mla_prep(zmc, mla_qn_g[l], mla_kvn_g[l], mla_w, tables_c)
        o_mla = mla_attention(q, [k, kc], [v, vc])
        wg = jnp.concatenate([moe_w_gate[l], shared_w_gate[l][None]], axis=0).astype(BF16)
        wu = jnp.concatenate([moe_w_up[l], shared_w_up[l][None]], axis=0).astype(BF16)
        wd = jnp.concatenate([moe_w_down[l], shared_w_down[l][None]], axis=0).astype(BF16)
        x, h2, logits = out_projection([o_gla, o_conv, o_s5, o_mla], w_out[l], x, g1, sh2, sc2, norm2_g[l],
                                       moe_router_w[l])
        gates = gate_matrix(route(logits, moe_router_b[l]))
        if not last:
            oc_conv = conformer_conv(zvc, *conv_p)
            oc_mla = mla_attention(qc, [kc], [vc])
            xc, h2c, logits_c = out_projection([oc_gla, oc_conv, oc_s5, oc_mla], w_out[l], xc, g1c, sh2c, sc2c,
                                               norm2_g[l], moe_router_w[l])
            gates_c = gate_matrix(route(logits_c, moe_router_b[l]))
            xc = moe_residual(h2c, gates_c, wg, wu, wd, xc, g2c, final_g, False)
        x = moe_residual(h2, gates, wg, wu, wd, x, g2, final_g, last)
    return x
```

```python
import functools
import math

import numpy as np
import jax
import jax.numpy as jnp
from jax import lax
from jax.experimental import pallas as pl
from jax.experimental.pallas import tpu as pltpu

F32 = jnp.float32
BF16 = jnp.bfloat16
HIGHEST = lax.Precision.HIGHEST
EPS = 1e-6

GRID_W = 64

GLA_HEADS = 4
GLA_DK = 32
GLA_DV = 64
GLA_LOWRANK = 16
GLA_TAU = 16.0
GLA_CHUNK = 64

CONV_C = 256
CONV_K = 31
CONV_HALO = 16

S5_C = 256
S5_GROUP = 16
S5_NG = 16
S5_P = 64
S5_STATE = S5_NG * S5_P
S5_STRIP = 256

MLA_HEADS = 4
MLA_NOPE = 64
MLA_ROPE = 32
MLA_V = 64
MLA_Q_LORA = 256
MLA_KV_LORA = 128
MLA_HEAD_PAD = 128
ROPE_AXIS = MLA_ROPE // 2
ROPE_BASE = 10000.0

N_EXPERTS = 64
TOP_K = 8
N_GROUPS = 8
TOPK_GROUPS = 4
ROUTED_SCALE = 2.5
MOE_CHUNK = 4096
MOE_ROWS = 128
MOE_BATCH = 8

IN_SPLITS = (128, 128, 256, 256, 16, 16, 512, 256, 256, 128, 32)
W_GLA, W_CONV, W_S5, W_MLA = 1024, 512, 256, 512

VMEM_LIMIT = 48 * 1024 * 1024

NT_DIMS = (((1,), (1,)), ((), ()))
TN_DIMS = (((0,), (0,)), ((), ()))


def _params(*sem, vmem=VMEM_LIMIT):
    return pltpu.CompilerParams(dimension_semantics=sem, vmem_limit_bytes=vmem)


def _silu(v):
    return v * jax.nn.sigmoid(v)


def _dot(a, b, **kw):
    return jnp.dot(a, b, preferred_element_type=F32, **kw)


def _tile(n, pref):
    return pref if n % pref == 0 else n


def _pack_bf16_pairs(h):
    bits = pltpu.bitcast(h.astype(BF16).astype(F32), jnp.int32)
    w = h.shape[-1] // 2
    return lax.shift_right_logical(bits[:, :w], 16) | bits[:, w:]


def _unpack_bf16_pairs(words):
    low = jnp.concatenate([pltpu.bitcast(v << 16, F32) for v in words], axis=1)
    high = jnp.concatenate([pltpu.bitcast(v & -65536, F32) for v in words], axis=1)
    return low.astype(BF16), high.astype(BF16)


def _ada_kernel(c_ref, w_ref, b_ref, o_ref):
    o_ref[...] = _dot(_silu(c_ref[...]), w_ref[...], precision=HIGHEST) + b_ref[...]


def ada_mod(cc, w, b):
    rows, dm = cc.shape
    n = w.shape[1]
    tn = _tile(n, 512)
    return pl.pallas_call(
        _ada_kernel, grid=(n // tn,),
        in_specs=[pl.BlockSpec((rows, dm), lambda j: (0, 0)),
                  pl.BlockSpec((dm, tn), lambda j: (0, j)),
                  pl.BlockSpec((1, tn), lambda j: (0, j))],
        out_specs=pl.BlockSpec((rows, tn), lambda j: (0, j)),
        out_shape=jax.ShapeDtypeStruct((rows, n), F32),
        compiler_params=_params("arbitrary"), name="ada_mod")(cc, w, b.reshape(1, n))


def _inproj_kernel(x_ref, sh_ref, sc_ref, g_ref, w_ref, *o_refs):
    x = x_ref[0]
    ms = jnp.mean(x * x, axis=-1, keepdims=True)
    h = (x * lax.rsqrt(ms + EPS) * g_ref[...] * (1.0 + sc_ref[0]) + sh_ref[0]).astype(BF16)
    off = 0
    for o_ref in o_refs:
        w = o_ref.shape[-1]
        o_ref[0] = _dot(h, w_ref[:, off:off + w])
        off += w


def in_projection(x, shift, scale, g, w):
    bsz, seq, dm = x.shape
    tm = _tile(seq, 512)
    widths = (W_GLA, W_CONV, W_S5, W_MLA)
    vec = pl.BlockSpec((1, 1, dm), lambda b, i: (b, 0, 0))
    return pl.pallas_call(
        _inproj_kernel, grid=(bsz, seq // tm),
        in_specs=[pl.BlockSpec((1, tm, dm), lambda b, i: (b, i, 0)), vec, vec,
                  pl.BlockSpec((1, dm), lambda b, i: (0, 0)),
                  pl.BlockSpec(w.shape, lambda b, i: (0, 0))],
        out_specs=[pl.BlockSpec((1, tm, wd), lambda b, i: (b, i, 0)) for wd in widths],
        out_shape=[jax.ShapeDtypeStruct((bsz, seq, wd), F32) for wd in widths],
        compiler_params=_params("parallel", "parallel"), name="in_projection")(x, shift, scale, g.reshape(1, dm), w)


def _rope_swap(t):
    s = t.reshape(t.shape[:-1] + (2, 2, ROPE_AXIS // 2))
    return jnp.stack([-s[..., 1, :], s[..., 0, :]], axis=-2).reshape(t.shape)


def assemble_w_in(w_in):
    edges = np.cumsum((0,) + IN_SPLITS)
    q, k, v, r, a_f, a_b, conv, s5, cq, ckv, kr = [w_in[:, edges[i]:edges[i + 1]] for i in range(len(IN_SPLITS))]
    pad = lambda t, w: jnp.pad(t, ((0, 0), (0, w - t.shape[1])))
    cols = [q, k, v, r, pad(a_f, 128), pad(a_b, 128), conv, s5, cq, ckv, pad(jnp.concatenate([kr, _rope_swap(kr)], 1), 128)]
    return jnp.concatenate(cols, axis=1).astype(BF16)


def _gla_kernel(z_ref, zc_ref, wa_ref, ba_ref, g_ref, o_ref, oc_ref, of_scr, st_scr, *, seq, ctx_len):
    c = GLA_CHUNK
    nh = GLA_HEADS
    kw = nh * GLA_DK
    vw = nh * GLA_DV
    iota = lambda shape, d: lax.broadcasted_iota(jnp.int32, shape, d)
    tri = [(iota((c, c), 1) <= iota((c, c), 0)).astype(F32), (iota((c, c), 1) >= iota((c, c), 0)).astype(F32)]
    r4 = iota((nh * c, c), 0) & (c - 1)
    c4 = iota((nh * c, c), 1)
    causal = [c4 <= r4, c4 >= r4]
    k_of = lambda t: t >> int(math.log2(GLA_DK))
    v_of = lambda t: t >> int(math.log2(GLA_DV))
    k_head = [(k_of(iota((1, kw), 1)) == h).astype(F32) for h in range(nh)]
    v_head = [(v_of(iota((1, vw), 1)) == h).astype(F32) for h in range(nh)]
    st_mask = (v_of(iota((vw, kw), 0)) == k_of(iota((vw, kw), 1))).astype(F32)
    head_mean = (v_of(iota((vw, vw), 0)) == v_of(iota((vw, vw), 1))).astype(F32) * (1.0 / GLA_DV)

    def chunk(ref, s, d):
        blk = ref[0, pl.ds(s, c), :]
        q = blk[:, 0:kw] * (GLA_DK ** -0.5)
        k = blk[:, kw:2 * kw]
        v = blk[:, 2 * kw:2 * kw + vw]
        r = blk[:, 2 * kw + vw:2 * kw + 2 * vw]
        a0 = 2 * kw + 2 * vw + 128 * d
        zl = _dot(blk[:, a0:a0 + 128], wa_ref[d], precision=HIGHEST) + ba_ref[d]
        la = (jnp.minimum(zl, 0.0) - jnp.log1p(jnp.exp(-jnp.abs(zl)))) / GLA_TAU
        b = _dot(tri[d], la, precision=HIGHEST)
        b_end = b[c - 1:c, :] if d == 0 else b[0:1, :]
        q_dec = q * jnp.exp(b)
        k_inv = (k * jnp.exp(-b)).astype(BF16)
        k_tail = (k * jnp.exp(b_end - b)).astype(BF16)
        vb = v.astype(BF16)
        qs = jnp.concatenate([q_dec * k_head[h] for h in range(nh)], axis=0).astype(BF16)
        att = lax.dot_general(qs, k_inv, NT_DIMS, preferred_element_type=F32)
        att = jnp.where(causal[d], att, 0.0).astype(BF16)
        o_all = _dot(att, vb)
        o = o_all[0:c] * v_head[0]
        for h in range(1, nh):
            o = o + o_all[h * c:(h + 1) * c] * v_head[h]
        st = st_scr[...]
        o = o + lax.dot_general(q_dec.astype(BF16), st.astype(BF16), NT_DIMS, preferred_element_type=F32)
        upd = lax.dot_general(vb, k_tail, TN_DIMS, preferred_element_type=F32)
        st_scr[...] = st * jnp.exp(b_end) + upd * st_mask
        return o, r

    def finish(o, r):
        ms = _dot(o * o, head_mean, precision=HIGHEST)
        return o * lax.rsqrt(ms + EPS) * g_ref[...] * _silu(r)

    n_ctx, n_seq = ctx_len // c, seq // c

    st_scr[...] = jnp.zeros_like(st_scr)

    @pl.loop(0, n_ctx)
    def _(n):
        s = pl.multiple_of(n * c, c)
        of_scr[pl.ds(s, c), :] = chunk(zc_ref, s, 0)[0]

    @pl.loop(0, n_seq)
    def _(n):
        s = pl.multiple_of(n * c, c)
        of_scr[pl.ds(pl.multiple_of(ctx_len + s, c), c), :] = chunk(z_ref, s, 0)[0]

    st_scr[...] = jnp.zeros_like(st_scr)

    @pl.loop(0, n_ctx)
    def _(n):
        s = pl.multiple_of((n_ctx - 1 - n) * c, c)
        o, r = chunk(zc_ref, s, 1)
        oc_ref[0, pl.ds(s, c), :] = finish(o + of_scr[pl.ds(s, c), :], r)

    @pl.loop(0, n_seq)
    def _(n):
        s = pl.multiple_of((n_seq - 1 - n) * c, c)
        o, r = chunk(z_ref, s, 1)
        o_ref[0, pl.ds(s, c), :] = finish(o + of_scr[pl.ds(pl.multiple_of(ctx_len + s, c), c), :], r)


def gla_mixer(zg, zgc, w_a2, b_a, norm_g):
    bsz, seq, wd = zg.shape
    ctx_len = zgc.shape[1]
    vw = GLA_HEADS * GLA_DV
    kw = GLA_HEADS * GLA_DK
    wa = jnp.pad(w_a2, ((0, 0), (0, 128 - GLA_LOWRANK), (0, 0)))
    full = lambda shape: pl.BlockSpec(shape, lambda b: (0,) * len(shape))
    return pl.pallas_call(
        functools.partial(_gla_kernel, seq=seq, ctx_len=ctx_len), grid=(bsz,),
        in_specs=[pl.BlockSpec((1, seq, wd), lambda b: (b, 0, 0)),
                  pl.BlockSpec((1, ctx_len, wd), lambda b: (b, 0, 0)),
                  full((2, 128, kw)), full((2, 1, kw)), full((1, vw))],
        out_specs=[pl.BlockSpec((1, seq, vw), lambda b: (b, 0, 0)),
                   pl.BlockSpec((1, ctx_len, vw), lambda b: (b, 0, 0))],
        out_shape=[jax.ShapeDtypeStruct((bsz, seq, vw), F32), jax.ShapeDtypeStruct((bsz, ctx_len, vw), F32)],
        scratch_shapes=[pltpu.VMEM((seq + ctx_len, vw), F32), pltpu.VMEM((vw, kw), F32)],
        compiler_params=_params("parallel"), name="gla_mixer")(
            zg, zgc, wa, b_a.reshape(2, 1, kw), norm_g.reshape(1, vw))


def _conv_kernel(u_ref, dw_ref, dwb_ref, lng_ref, lnb_ref, pw_ref, pwb_ref, o_ref, h_scr, *, seq, rows):
    halo = CONV_HALO
    h_scr[0:halo, :] = jnp.zeros((halo, CONV_C), F32)
    h_scr[halo + seq:2 * halo + seq, :] = jnp.zeros((halo, CONV_C), F32)

    @pl.loop(0, seq // rows)
    def _(t):
        s = pl.multiple_of(t * rows, rows)
        u = u_ref[0, pl.ds(s, rows), :]
        h_scr[pl.ds(pl.multiple_of(halo + s, 8), rows), :] = u[:, :CONV_C] * jax.nn.sigmoid(u[:, CONV_C:])

    @pl.loop(0, seq // rows)
    def _(t):
        s = pl.multiple_of(t * rows, rows)
        win = h_scr[pl.ds(s, rows + 2 * halo), :]
        first = halo - CONV_K // 2
        acc = win[first:first + rows] * dw_ref[0:1, :]
        for k in range(1, CONV_K):
            acc = acc + win[first + k:first + k + rows] * dw_ref[k:k + 1, :]
        acc = acc + dwb_ref[...]
        mu = jnp.mean(acc, axis=-1, keepdims=True)
        var = jnp.mean(jnp.square(acc - mu), axis=-1, keepdims=True)
        y = _silu((acc - mu) * lax.rsqrt(var + EPS) * lng_ref[...] + lnb_ref[...])
        o_ref[0, pl.ds(s, rows), :] = _dot(y.astype(BF16), pw_ref[...]) + pwb_ref[...]


def conformer_conv(u, dw_w, dw_b, ln_g, ln_b, pw_w, pw_b):
    bsz, seq, _ = u.shape
    rows = _tile(seq, 128)
    full = lambda shape: pl.BlockSpec(shape, lambda b: (0,) * len(shape))
    row = lambda t: t.reshape(1, CONV_C)
    return pl.pallas_call(
        functools.partial(_conv_kernel, seq=seq, rows=rows), grid=(bsz,),
        in_specs=[pl.BlockSpec((1, seq, 2 * CONV_C), lambda b: (b, 0, 0)),
                  full((CONV_K + 1, CONV_C)), full((1, CONV_C)), full((1, CONV_C)), full((1, CONV_C)),
                  full((CONV_C, CONV_C)), full((1, CONV_C))],
        out_specs=pl.BlockSpec((1, seq, CONV_C), lambda b: (b, 0, 0)),
        out_shape=jax.ShapeDtypeStruct((bsz, seq, CONV_C), F32),
        scratch_shapes=[pltpu.VMEM((seq + 2 * CONV_HALO, CONV_C), F32)],
        compiler_params=_params("parallel"), name="conformer_conv")(
            u, jnp.pad(dw_w, ((0, 1), (0, 0))), row(dw_b), row(ln_g), row(ln_b), pw_w.astype(BF16), row(pw_b))


def s5_matrices(lam_re, lam_im, log_dt, b_re, b_im, c_re, c_im):
    dt = jnp.exp(log_dt)[:, None]
    mag = jnp.exp(lam_re * dt)
    a_re, a_im = mag * jnp.cos(lam_im * dt), mag * jnp.sin(lam_im * dt)
    den = lam_re * lam_re + lam_im * lam_im
    f_re = ((a_re - 1.0) * lam_re + a_im * lam_im) / den
    f_im = (a_im * lam_re - (a_re - 1.0) * lam_im) / den
    bb_re = f_re[..., None] * b_re - f_im[..., None] * b_im
    bb_im = f_re[..., None] * b_im + f_im[..., None] * b_re
    eye = jnp.eye(S5_NG, dtype=F32)
    blk_b = lambda t: jnp.einsum("gph,gk->ghkp", t, eye).reshape(S5_C, S5_STATE)
    blk_c = lambda t: jnp.einsum("ghp,gk->gpkh", t, eye).reshape(S5_STATE, S5_C)
    a = jnp.stack([a_re.reshape(S5_STATE), a_im.reshape(S5_STATE)])
    b_mat = jnp.concatenate([blk_b(bb_re), blk_b(bb_im)], axis=1).astype(BF16)
    c_mat = jnp.concatenate([blk_c(c_re), -blk_c(c_im)], axis=0).astype(BF16)
    return a, b_mat, c_mat


def _s5_scan_kernel(u_ref, a_ref, b_ref, c_ref, h0_ref, y_ref, hl_ref, hs_scr, st_scr, *, steps, bsz, reverse):
    i = pl.program_id(0)
    ns = S5_STATE

    @pl.when(i == 0)
    def _():
        st_scr[...] = h0_ref[...]

    u = u_ref[...].reshape(steps * bsz, S5_C).astype(BF16)
    hs_scr[...] = _dot(u, b_ref[...])
    for s0 in range(0, ns, S5_STRIP):
        a_re = a_ref[0:1, s0:s0 + S5_STRIP]
        a_im = a_ref[1:2, s0:s0 + S5_STRIP]

        def step(j, carry):
            h_re, h_im = carry
            t = (steps - 1 - j) if reverse else j
            r0 = pl.multiple_of(t * bsz, bsz)
            n_re = a_re * h_re - a_im * h_im + hs_scr[pl.ds(r0, bsz), s0:s0 + S5_STRIP]
            n_im = a_re * h_im + a_im * h_re + hs_scr[pl.ds(r0, bsz), ns + s0:ns + s0 + S5_STRIP]
            hs_scr[pl.ds(r0, bsz), s0:s0 + S5_STRIP] = n_re
            hs_scr[pl.ds(r0, bsz), ns + s0:ns + s0 + S5_STRIP] = n_im
            return n_re, n_im

        h_re, h_im = lax.fori_loop(0, steps, step, (st_scr[:, s0:s0 + S5_STRIP], st_scr[:, ns + s0:ns + s0 + S5_STRIP]))
        st_scr[:, s0:s0 + S5_STRIP] = h_re
        st_scr[:, ns + s0:ns + s0 + S5_STRIP] = h_im
    y_ref[...] = _dot(hs_scr[...].astype(BF16), c_ref[...]).reshape(steps, bsz, S5_C)
    hl_ref[...] = st_scr[...]


def s5_scan(u_tm, mats, h0, reverse):
    a, b_mat, c_mat = mats
    seq, bsz, _ = u_tm.shape
    steps = _tile(seq, 32)
    n = seq // steps
    order = (lambda i: (n - 1 - i, 0, 0)) if reverse else (lambda i: (i, 0, 0))
    full = lambda shape: pl.BlockSpec(shape, lambda i: (0,) * len(shape))
    return pl.pallas_call(
        functools.partial(_s5_scan_kernel, steps=steps, bsz=bsz, reverse=reverse), grid=(n,),
        in_specs=[pl.BlockSpec((steps, bsz, S5_C), order), full(a.shape), full(b_mat.shape), full(c_mat.shape),
                  full(h0.shape)],
        out_specs=[pl.BlockSpec((steps, bsz, S5_C), order), full(h0.shape)],
        out_shape=[jax.ShapeDtypeStruct((seq, bsz, S5_C), F32), jax.ShapeDtypeStruct(h0.shape, F32)],
        scratch_shapes=[pltpu.VMEM((steps * bsz, 2 * S5_STATE), F32), pltpu.VMEM(h0.shape, F32)],
        compiler_params=_params("arbitrary"), name="s5_scan")(u_tm, a, b_mat, c_mat, h0)


def _s5_out_kernel(u_ref, yf_ref, yb_ref, d_ref, w_ref, b_ref, o_ref):
    y = d_ref[...] * u_ref[...] + yf_ref[...] + yb_ref[...]
    z = _dot(y.astype(BF16), w_ref[...]) + b_ref[...]
    o_ref[...] = z[:, :S5_C] * jax.nn.sigmoid(z[:, S5_C:])


def s5_output(u, y_f, y_b, d_skip, glu_w, glu_b):
    n = u.shape[0]
    tm = _tile(n, 1024)
    rows = pl.BlockSpec((tm, S5_C), lambda i: (i, 0))
    full = lambda shape: pl.BlockSpec(shape, lambda i: (0,) * len(shape))
    return pl.pallas_call(
        _s5_out_kernel, grid=(n // tm,),
        in_specs=[rows, rows, rows, full((1, S5_C)), full((S5_C, 2 * S5_C)), full((1, 2 * S5_C))],
        out_specs=rows, out_shape=jax.ShapeDtypeStruct((n, S5_C), F32),
        compiler_params=_params("parallel"), name="s5_output")(
            u, y_f, y_b, d_skip.reshape(1, S5_C), glu_w.astype(BF16), glu_b.reshape(1, 2 * S5_C))


def s5_mixer(zs, zsc, lam_re, lam_im, log_dt, b_re, b_im, c_re, c_im, d_skip, glu_w, glu_b):
    bsz, seq, _ = zs.shape
    ctx_len = zsc.shape[1]
    u = jnp.transpose(zs, (1, 0, 2))
    uc = jnp.transpose(zsc, (1, 0, 2))
    zeros = jnp.zeros((bsz, 2 * S5_STATE), F32)
    ys, ycs = [], []
    for d in range(2):
        mats = s5_matrices(lam_re[d], lam_im[d], log_dt[d], b_re[d], b_im[d], c_re[d], c_im[d])
        yc, hc = s5_scan(uc, mats, zeros, reverse=d == 1)
        y, _ = s5_scan(u, mats, hc, reverse=d == 1)
        ys.append(y)
        ycs.append(yc)
    flat = lambda t: t.reshape(-1, S5_C)
    o = s5_output(flat(u), flat(ys[0]), flat(ys[1]), d_skip, glu_w, glu_b).reshape(seq, bsz, S5_C)
    oc = s5_output(flat(uc), flat(ycs[0]), flat(ycs[1]), d_skip, glu_w, glu_b).reshape(ctx_len, bsz, S5_C)
    return jnp.transpose(o, (1, 0, 2)), jnp.transpose(oc, (1, 0, 2))


def rope_tables(seq_len, rotate):
    if not rotate:
        return jnp.ones((seq_len, MLA_ROPE), F32), jnp.zeros((seq_len, MLA_ROPE), F32)
    rows = seq_len // GRID_W
    row = jnp.broadcast_to(jnp.arange(rows, dtype=F32)[:, None], (rows, GRID_W)).reshape(seq_len)
    col = jnp.broadcast_to(jnp.arange(GRID_W, dtype=F32)[None, :], (rows, GRID_W)).reshape(seq_len)
    inv_freq = ROPE_BASE ** (-jnp.arange(ROPE_AXIS // 2, dtype=F32) / (ROPE_AXIS // 2))
    ang = jnp.stack([row[:, None] * inv_freq, col[:, None] * inv_freq], axis=1)
    full = lambda t: jnp.broadcast_to(t[:, :, None, :], (seq_len, 2, 2, ROPE_AXIS // 2)).reshape(seq_len, MLA_ROPE)
    return full(jnp.cos(ang)), full(jnp.sin(ang))


def mla_tables(seq_len, rotate):
    cos, sin = rope_tables(seq_len, rotate)
    head = lambda rope, fill: jnp.concatenate(
        [jnp.full((seq_len, MLA_NOPE), fill, F32), rope, jnp.zeros((seq_len, MLA_HEAD_PAD - MLA_NOPE - MLA_ROPE), F32)], 1)
    cos_q = jnp.tile(head(cos, 1.0), (1, MLA_HEADS))
    sin_q = jnp.tile(head(sin, 0.0), (1, MLA_HEADS))
    cs_k = jnp.concatenate([cos, sin, jnp.zeros((seq_len, 128 - 2 * MLA_ROPE), F32)], 1)
    return cos_q, sin_q, cs_k


def mla_weights(w_uq, w_ukv):
    hd = MLA_NOPE + MLA_ROPE
    zq = jnp.zeros((MLA_Q_LORA, MLA_HEAD_PAD - hd), F32)
    zn = jnp.zeros((MLA_Q_LORA, MLA_NOPE), F32)
    wq, wq_sw, wk, wv = [], [], [], []
    for h in range(MLA_HEADS):
        qh = w_uq[:, h * hd:(h + 1) * hd]
        wq += [qh, zq]
        wq_sw += [zn, _rope_swap(qh[:, MLA_NOPE:]), zq]
        kvh = w_ukv[:, h * (MLA_NOPE + MLA_V):(h + 1) * (MLA_NOPE + MLA_V)]
        wk += [kvh[:, :MLA_NOPE], jnp.zeros((MLA_KV_LORA, MLA_HEAD_PAD - MLA_NOPE), F32)]
        wv += [kvh[:, MLA_NOPE:]]
    place = np.zeros((128, MLA_HEADS * MLA_HEAD_PAD), np.float32)
    for h in range(MLA_HEADS):
        for j in range(MLA_ROPE):
            place[j, h * MLA_HEAD_PAD + MLA_NOPE + j] = 1.0
            place[MLA_ROPE + j, h * MLA_HEAD_PAD + MLA_NOPE + j] = 1.0
    cat = lambda ts: jnp.concatenate(ts, axis=1).astype(BF16)
    return cat(wq), cat(wq_sw), cat(wk), cat(wv), jnp.asarray(place)


def _mla_prep_kernel(z_ref, qg_ref, kg_ref, wq_ref, wqs_ref, wk_ref, wv_ref, pl_ref, cq_ref, sq_ref, csk_ref,
                     q_ref, k_ref, v_ref):
    z = z_ref[0]
    norm = lambda t, g: (t * lax.rsqrt(jnp.mean(t * t, axis=-1, keepdims=True) + EPS) * g).astype(BF16)
    cq = norm(z[:, :MLA_Q_LORA], qg_ref[...])
    ckv = norm(z[:, MLA_Q_LORA:MLA_Q_LORA + MLA_KV_LORA], kg_ref[...])
    scale = (MLA_NOPE + MLA_ROPE) ** -0.5
    q = _dot(cq, wq_ref[...]) * cq_ref[...] + _dot(cq, wqs_ref[...]) * sq_ref[...]
    q_ref[0] = (q * scale).astype(BF16)
    kr = z[:, MLA_Q_LORA + MLA_KV_LORA:] * csk_ref[...]
    k_ref[0] = (_dot(ckv, wk_ref[...]) + _dot(kr, pl_ref[...], precision=HIGHEST)).astype(BF16)
    v_ref[0] = _dot(ckv, wv_ref[...]).astype(BF16)


def mla_prep(zm, qn_g, kvn_g, weights, tables):
    bsz, seq, wd = zm.shape
    tm = _tile(seq, 512)
    wq, wq_sw, wk, wv, place = weights
    cos_q, sin_q, cs_k = tables
    qw, vw = MLA_HEADS * MLA_HEAD_PAD, MLA_HEADS * MLA_V
    full = lambda t: pl.BlockSpec(t.shape, lambda b, i: (0,) * t.ndim)
    pos = lambda t: pl.BlockSpec((tm, t.shape[1]), lambda b, i: (i, 0))
    out = lambda w: pl.BlockSpec((1, tm, w), lambda b, i: (b, i, 0))
    qg, kg = qn_g.reshape(1, -1), kvn_g.reshape(1, -1)
    return pl.pallas_call(
        _mla_prep_kernel, grid=(bsz, seq // tm),
        in_specs=[pl.BlockSpec((1, tm, wd), lambda b, i: (b, i, 0)), full(qg), full(kg), full(wq), full(wq_sw),
                  full(wk), full(wv), full(place), pos(cos_q), pos(sin_q), pos(cs_k)],
        out_specs=[out(qw), out(qw), out(vw)],
        out_shape=[jax.ShapeDtypeStruct((bsz, seq, qw), BF16), jax.ShapeDtypeStruct((bsz, seq, qw), BF16),
                   jax.ShapeDtypeStruct((bsz, seq, vw), BF16)],
        compiler_params=_params("parallel", "parallel"), name="mla_prep")(
            zm, qg, kg, wq, wq_sw, wk, wv, place, cos_q, sin_q, cs_k)


def _mla_attn_kernel(q_ref, *refs, n_seg):
    k_refs, v_refs, o_ref = refs[:n_seg], refs[n_seg:2 * n_seg], refs[2 * n_seg]
    outs = []
    for h in range(MLA_HEADS):
        q = q_ref[0, :, h * MLA_HEAD_PAD:(h + 1) * MLA_HEAD_PAD]
        s = [lax.dot_general(q, k_ref[0, :, h * MLA_HEAD_PAD:(h + 1) * MLA_HEAD_PAD], NT_DIMS,
                             preferred_element_type=F32) for k_ref in k_refs]
        m = functools.reduce(jnp.maximum, [jnp.max(t, axis=-1, keepdims=True) for t in s])
        p = [jnp.exp(t - m) for t in s]
        den = functools.reduce(jnp.add, [jnp.sum(t, axis=-1, keepdims=True) for t in p])
        o = functools.reduce(jnp.add, [_dot(t.astype(BF16), v_ref[0, :, h * MLA_V:(h + 1) * MLA_V])
                                       for t, v_ref in zip(p, v_refs)])
        outs.append(o / den)
    o_ref[0] = jnp.concatenate(outs, axis=-1)


def mla_attention(q, ks, vs):
    bsz, seq, qw = q.shape
    tq = _tile(seq, 256)
    vw = MLA_HEADS * MLA_V
    seg = lambda t: pl.BlockSpec((1,) + t.shape[1:], lambda b, i: (b, 0, 0))
    return pl.pallas_call(
        functools.partial(_mla_attn_kernel, n_seg=len(ks)), grid=(bsz, seq // tq),
        in_specs=[pl.BlockSpec((1, tq, qw), lambda b, i: (b, i, 0))] + [seg(t) for t in ks] + [seg(t) for t in vs],
        out_specs=pl.BlockSpec((1, tq, vw), lambda b, i: (b, i, 0)),
        out_shape=jax.ShapeDtypeStruct((bsz, seq, vw), F32),
        compiler_params=_params("parallel", "parallel"), name="mla_attention")(q, *ks, *vs)


def _outproj_kernel(a_ref, b_ref, c_ref, d_ref, w_ref, x_ref, g1_ref, sh_ref, sc_ref, g_ref, rw_ref,
                    xo_ref, h_ref, lg_ref):
    mix = None
    for j, o_ref in enumerate((a_ref, b_ref, c_ref, d_ref)):
        wd = o_ref.shape[-1]
        part = _dot(o_ref[0].astype(BF16), w_ref[j * wd:(j + 1) * wd, :])
        mix = part if mix is None else mix + part
    x = x_ref[0] + g1_ref[0] * mix
    xo_ref[0] = x
    ms = jnp.mean(x * x, axis=-1, keepdims=True)
    h = x * lax.rsqrt(ms + EPS) * g_ref[...] * (1.0 + sc_ref[0]) + sh_ref[0]
    h_ref[0] = _pack_bf16_pairs(h)
    lg_ref[...] = lax.dot_general(rw_ref[...], h, NT_DIMS, precision=HIGHEST, preferred_element_type=F32)


def out_projection(parts, w_out, x, g1, shift, scale, g, router_w):
    bsz, seq, dm = x.shape
    tm = _tile(seq, 512)
    nt = seq // tm
    ne = router_w.shape[1]
    rows = lambda w: pl.BlockSpec((1, tm, w), lambda b, i: (b, i, 0))
    vec = pl.BlockSpec((1, 1, dm), lambda b, i: (b, 0, 0))
    full = lambda shape: pl.BlockSpec(shape, lambda b, i: (0,) * len(shape))
    return pl.pallas_call(
        _outproj_kernel, grid=(bsz, nt),
        in_specs=[rows(p.shape[-1]) for p in parts] + [full(w_out.shape), rows(dm), vec, vec, vec, full((1, dm)),
                                                       full((ne, dm))],
        out_specs=[rows(dm), rows(dm // 2), pl.BlockSpec((ne, tm), lambda b, i: (0, b * nt + i))],
        out_shape=[jax.ShapeDtypeStruct((bsz, seq, dm), F32), jax.ShapeDtypeStruct((bsz, seq, dm // 2), jnp.int32),
                   jax.ShapeDtypeStruct((ne, bsz * seq), F32)],
        compiler_params=_params("parallel", "parallel"), name="out_projection")(
            *parts, w_out.astype(BF16), x, g1, shift, scale, g.reshape(1, dm), router_w.T)


def _route_kernel(lg_ref, b_ref, e_ref, w_ref):
    ne, tt = lg_ref.shape
    per = ne // N_GROUPS
    neg = -jnp.inf
    scores = jax.nn.sigmoid(lg_ref[...])
    biased = scores + b_ref[...]
    v3 = biased.reshape(N_GROUPS, per, tt)
    e_in = lax.broadcasted_iota(jnp.int32, v3.shape, 1).astype(F32)
    m1 = jnp.max(v3, axis=1, keepdims=True)
    i1 = jnp.min(jnp.where(v3 == m1, e_in, float(per)), axis=1, keepdims=True)
    m2 = jnp.max(jnp.where(e_in == i1, neg, v3), axis=1, keepdims=True)
    grp = (m1 + m2).reshape(N_GROUPS, tt)

    def pick(cur, count):
        ids = lax.broadcasted_iota(jnp.int32, cur.shape, 0).astype(F32)
        marks = jnp.zeros(cur.shape, F32)
        picked = []
        for _ in range(count):
            m = jnp.max(cur, axis=0, keepdims=True)
            first = jnp.min(jnp.where(cur == m, ids, float(cur.shape[0])), axis=0, keepdims=True)
            hit = ids == first
            marks = jnp.where(hit, 1.0, marks)
            cur = jnp.where(hit, neg, cur)
            picked.append(first)
        return marks, picked

    grp_on, _ = pick(grp, TOPK_GROUPS)
    exp_on = jnp.broadcast_to(grp_on.reshape(N_GROUPS, 1, tt), v3.shape).reshape(ne, tt)
    chosen, picked = pick(jnp.where(exp_on > 0.0, biased, neg), TOP_K)
    w = scores * chosen
    gate = w / jnp.sum(w, axis=0, keepdims=True) * ROUTED_SCALE
    ids = lax.broadcasted_iota(jnp.int32, gate.shape, 0).astype(F32)
    e_ref[...] = jnp.concatenate(picked, axis=0).astype(jnp.int32)
    w_ref[...] = jnp.concatenate([jnp.sum(jnp.where(ids == p, gate, 0.0), axis=0, keepdims=True) for p in picked], axis=0)


def route(logits_t, router_b):
    ne, n = logits_t.shape
    tt = _tile(n, 1024)
    out = pl.BlockSpec((TOP_K, tt), lambda i: (0, i))
    return pl.pallas_call(
        _route_kernel, grid=(n // tt,),
        in_specs=[pl.BlockSpec((ne, tt), lambda i: (0, i)), pl.BlockSpec((ne, 1), lambda i: (0, 0))],
        out_specs=[out, out],
        out_shape=[jax.ShapeDtypeStruct((TOP_K, n), jnp.int32), jax.ShapeDtypeStruct((TOP_K, n), F32)],
        compiler_params=_params("parallel"), name="route")(logits_t, router_b.reshape(ne, 1))


def moe_plan(eid, wgt, chunk):
    n = eid.shape[1]
    nc, na = n // chunk, chunk * TOP_K
    nb = na // MOE_ROWS
    ns = nb + N_EXPERTS
    key = eid.T.reshape(nc, na) * na + jnp.arange(na, dtype=jnp.int32)
    skey, sw = lax.sort((key, wgt.T.reshape(nc, na)), dimension=1, num_keys=1)
    e_sorted, tok = skey // na, (skey % na) // TOP_K
    ends = jnp.sum(e_sorted[:, None, :] <= jnp.arange(N_EXPERTS, dtype=jnp.int32)[None, :, None], axis=-1, dtype=jnp.int32)
    starts = jnp.concatenate([jnp.zeros((nc, 1), jnp.int32), ends[:, :-1]], axis=1)
    first_e, last_e = e_sorted[:, ::MOE_ROWS], e_sorted[:, MOE_ROWS - 1::MOE_ROWS]
    cum = jnp.cumsum(last_e - first_e + 1, axis=1)
    step = jnp.arange(ns, dtype=jnp.int32)
    valid = step[None, :] < cum[:, -1:]
    blk = jnp.minimum(jnp.sum(cum[:, None, :] <= step[None, :, None], axis=-1, dtype=jnp.int32), nb - 1)
    take = lambda t, i: jnp.take_along_axis(t, i, axis=1)
    exp = jnp.where(valid, take(first_e, blk) + step[None, :] - (take(cum, blk) - take(last_e - first_e + 1, blk)), last_e[:, -1:])
    lo = jnp.clip(take(starts, exp) - blk * MOE_ROWS, 0, MOE_ROWS)
    hi = jnp.clip(take(ends, exp) - blk * MOE_ROWS, 0, MOE_ROWS)
    row = jnp.arange(MOE_ROWS, dtype=jnp.int32)
    mine = (row >= lo[..., None]) & (row < hi[..., None]) & valid[..., None]
    rows_of = lambda t: jnp.take_along_axis(t.reshape(nc, nb, MOE_ROWS), blk[..., None], axis=1)
    tok_step = jnp.where(mine, rows_of(tok), chunk).reshape(nc * ns, 1, MOE_ROWS)
    w_step = jnp.where(mine, rows_of(sw), 0.0).reshape(nc * ns, 1, MOE_ROWS)
    return tok_step, w_step, exp.reshape(nc * ns), (valid & (hi > lo)).astype(jnp.int32).reshape(nc * ns)


def _moe_routed_kernel(exp_ref, act_ref, x_ref, tok_ref, w_ref, wg_ref, wu_ref, wd_ref, y_ref, xt_scr, ot_scr,
                       *, chunk, ns):
    c, s = pl.program_id(0), pl.program_id(1)
    bm = MOE_ROWS
    stride = bm + 1
    xg = x_ref.shape[1] // chunk
    yg = (y_ref.shape[1] - 8) // chunk

    @pl.when(s == 0)
    def _():
        y_ref[...] = jnp.zeros_like(y_ref)

    @pl.when(act_ref[c * ns + s] != 0)
    def _():
        for mi in range(bm):
            t = jnp.minimum(tok_ref[0, 0, mi], chunk - 1)
            xt_scr[pl.ds(mi, xg, stride=stride), :] = x_ref[0, pl.ds(pl.multiple_of(t * xg, xg), xg), :]
        low, high = _unpack_bf16_pairs([xt_scr[j * stride:j * stride + bm, :] for j in range(xg)])
        half = low.shape[1]
        up = lambda w_ref: _dot(low, w_ref[0, :half, :]) + _dot(high, w_ref[0, half:, :])
        act = (_silu(up(wg_ref)) * up(wu_ref)).astype(BF16)
        out = _dot(act, wd_ref[0])
        for j in range(yg):
            ot_scr[j * stride:j * stride + bm, :] = out[:, j * 128:(j + 1) * 128]
        for base in range(0, bm, MOE_BATCH):
            at = [pl.multiple_of(tok_ref[0, 0, mi] * yg, yg) for mi in range(base, base + MOE_BATCH)]
            new = [y_ref[0, pl.ds(i, yg), :] + w_ref[0, 0, mi] * ot_scr[pl.ds(mi, yg, stride=stride), :]
                   for i, mi in zip(at, range(base, base + MOE_BATCH))]
            for i, v in zip(at, new):
                y_ref[0, pl.ds(i, yg), :] = v


def moe_routed(words, eid, wgt, wg, wu, wd):
    n, hw = words.shape
    dm = 2 * hw
    chunk = _tile(n, MOE_CHUNK)
    nc = n // chunk
    ns = chunk * TOP_K // MOE_ROWS + N_EXPERTS
    xg, yg = hw // 128, dm // 128
    ff = wg.shape[2]
    tok_step, w_step, exp, active = moe_plan(eid, wgt, chunk)
    smem = lambda: pl.BlockSpec((1, 1, MOE_ROWS), lambda c, s, e, a: (c * ns + s, 0, 0), memory_space=pltpu.SMEM)
    once = pl.Buffered(1)
    y = pl.pallas_call(
        functools.partial(_moe_routed_kernel, chunk=chunk, ns=ns),
        grid_spec=pltpu.PrefetchScalarGridSpec(
            num_scalar_prefetch=2, grid=(nc, ns),
            in_specs=[pl.BlockSpec((1, chunk * xg, 128), lambda c, s, e, a: (c, 0, 0), pipeline_mode=once),
                      smem(), smem(),
                      pl.BlockSpec((1, dm, ff), lambda c, s, e, a: (e[c * ns + s], 0, 0)),
                      pl.BlockSpec((1, dm, ff), lambda c, s, e, a: (e[c * ns + s], 0, 0)),
                      pl.BlockSpec((1, ff, dm), lambda c, s, e, a: (e[c * ns + s], 0, 0))],
            out_specs=pl.BlockSpec((1, chunk * yg + 8, 128), lambda c, s, e, a: (c, 0, 0), pipeline_mode=once),
            scratch_shapes=[pltpu.VMEM((-(-xg * (MOE_ROWS + 1) // 8) * 8, 128), jnp.int32),
                            pltpu.VMEM((yg * (MOE_ROWS + 1), 128), F32)]),
        out_shape=jax.ShapeDtypeStruct((nc, chunk * yg + 8, 128), F32),
        compiler_params=_params("arbitrary", "arbitrary"), name="moe_routed")(
            exp, active, words.reshape(nc, chunk * xg, 128), tok_step, w_step, wg, wu, wd)
    return y[:, :chunk * yg].reshape(n, dm)


def _moe_finish_kernel(y_ref, h_ref, x_ref, g2_ref, wg_ref, wu_ref, wd_ref, fg_ref, o_ref, *, final_norm):
    words = h_ref[...]
    low, high = _unpack_bf16_pairs([words[:, j * 128:(j + 1) * 128] for j in range(words.shape[1] // 128)])
    half = low.shape[1]
    up = lambda w_ref: _dot(low, w_ref[:half, :]) + _dot(high, w_ref[half:, :])
    shared = _dot((_silu(up(wg_ref)) * up(wu_ref)).astype(BF16), wd_ref[...])
    x = x_ref[...] + g2_ref[0] * (y_ref[...] + shared)
    if final_norm:
        x = x * lax.rsqrt(jnp.mean(x * x, axis=-1, keepdims=True) + EPS) * fg_ref[...]
    o_ref[...] = x


def moe_finish(y, words, x, g2, sg, su, sd, final_g, final_norm):
    bsz, seq, dm = x.shape
    n = bsz * seq
    tm = _tile(seq, 512)
    nt = seq // tm
    rows = lambda w: pl.BlockSpec((tm, w), lambda i: (i, 0))
    full = lambda t: pl.BlockSpec(t.shape, lambda i: (0,) * t.ndim)
    sg, su, sd, fg = sg.astype(BF16), su.astype(BF16), sd.astype(BF16), final_g.reshape(1, dm)
    out = pl.pallas_call(
        functools.partial(_moe_finish_kernel, final_norm=final_norm), grid=(n // tm,),
        in_specs=[rows(dm), rows(dm // 2), rows(dm), pl.BlockSpec((1, 1, dm), lambda i: (i // nt, 0, 0)),
                  full(sg), full(su), full(sd), full(fg)],
        out_specs=rows(dm), out_shape=jax.ShapeDtypeStruct((n, dm), F32),
        compiler_params=_params("parallel"), name="moe_finish")(y, words, x.reshape(n, dm), g2, sg, su, sd, fg)
    return out.reshape(bsz, seq, dm)


def kernel(x, c, ctx, c_ctx, ada_w, ada_b, norm1_g, norm2_g, w_in, w_out, gla_w_a2, gla_b_a, gla_norm_g,
           conv_dw_w, conv_dw_b, conv_ln_g, conv_ln_b, conv_pw_w, conv_pw_b,
           s5_lam_re, s5_lam_im, s5_log_dt, s5_b_re, s5_b_im, s5_c_re, s5_c_im, s5_d, s5_glu_w, s5_glu_b,
           mla_qn_g, mla_kvn_g, mla_w_uq, mla_w_ukv,
           moe_router_w, moe_router_b, moe_w_gate, moe_w_up, moe_w_down,
           shared_w_gate, shared_w_up, shared_w_down, final_g):
    bsz, seq, dm = x.shape
    ctx_len = ctx.shape[1]
    depth = ada_w.shape[0]
    cc = jnp.concatenate([c, c_ctx[None], jnp.zeros((-(bsz + 1) % 8, dm), F32)], axis=0)
    tables = mla_tables(seq, True)
    tables_c = mla_tables(ctx_len, False)
    xc = ctx
    for l in range(depth):
        last = l == depth - 1
        mod = ada_mod(cc, ada_w[l], ada_b[l])
        lat = [t.reshape(bsz, 1, dm) for t in jnp.split(mod[:bsz], 6, axis=-1)]
        con = [jnp.broadcast_to(t.reshape(1, 1, dm), (bsz, 1, dm)) for t in jnp.split(mod[bsz], 6)]
        sh1, sc1, g1, sh2, sc2, g2 = lat
        sh1c, sc1c, g1c, sh2c, sc2c, g2c = con
        w_in_l = assemble_w_in(w_in[l])
        zg, zv, zs, zm = in_projection(x, sh1, sc1, norm1_g[l], w_in_l)
        zgc, zvc, zsc, zmc = in_projection(xc, sh1c, sc1c, norm1_g[l], w_in_l)
        conv_p = (conv_dw_w[l], conv_dw_b[l], conv_ln_g[l], conv_ln_b[l], conv_pw_w[l], conv_pw_b[l])
        o_gla, oc_gla = gla_mixer(zg, zgc, gla_w_a2[l], gla_b_a[l], gla_norm_g[l])
        o_conv = conformer_conv(zv, *conv_p)
        o_s5, oc_s5 = s5_mixer(zs, zsc, s5_lam_re[l], s5_lam_im[l], s5_log_dt[l], s5_b_re[l], s5_b_im[l],
                               s5_c_re[l], s5_c_im[l], s5_d[l], s5_glu_w[l], s5_glu_b[l])
        mla_w = mla_weights(mla_w_uq[l], mla_w_ukv[l])
        q, k, v = mla_prep(zm, mla_qn_g[l], mla_kvn_g[l], mla_w, tables)
        qc, kc, vc = mla_prep(zmc, mla_qn_g[l], mla_kvn_g[l], mla_w, tables_c)
        o_mla = mla_attention(q, [k, kc], [v, vc])
        experts = (moe_w_gate[l].astype(BF16), moe_w_up[l].astype(BF16), moe_w_down[l].astype(BF16))
        shared = (shared_w_gate[l], shared_w_up[l], shared_w_down[l])

        def ffn(parts, x_in, gate1, shift, scale, gate2, final_norm):
            x_mid, words, logits = out_projection(parts, w_out[l], x_in, gate1, shift, scale, norm2_g[l], moe_router_w[l])
            words = words.reshape(-1, dm // 2)
            y = moe_routed(words, *route(logits, moe_router_b[l]), *experts)
            return moe_finish(y, words, x_mid, gate2, *shared, final_g, final_norm)

        if not last:
            oc_conv = conformer_conv(zvc, *conv_p)
            oc_mla = mla_attention(qc, [kc], [vc])
            xc = ffn([oc_gla, oc_conv, oc_s5, oc_mla], xc, g1c, sh2c, sc2c, g2c, False)
        x = ffn([o_gla, o_conv, o_s5, o_mla], x, g1, sh2, sc2, g2, last)
    return x
```

```python
import functools
import math

import numpy as np
import jax
import jax.numpy as jnp
from jax import lax
from jax.experimental import pallas as pl
from jax.experimental.pallas import tpu as pltpu

F32 = jnp.float32
BF16 = jnp.bfloat16
HIGHEST = lax.Precision.HIGHEST
EPS = 1e-6

GRID_W = 64

GLA_HEADS = 4
GLA_DK = 32
GLA_DV = 64
GLA_LOWRANK = 16
GLA_TAU = 16.0
GLA_CHUNK = 64
GLA_UNROLL = 2
GLA_FINISH_ROWS = 256

CONV_C = 256
CONV_K = 31
CONV_HALO = 16

S5_C = 256
S5_GROUP = 16
S5_NG = 16
S5_P = 64
S5_STATE = S5_NG * S5_P
S5_STRIP = 256

MLA_HEADS = 4
MLA_NOPE = 64
MLA_ROPE = 32
MLA_V = 64
MLA_Q_LORA = 256
MLA_KV_LORA = 128
MLA_HEAD_PAD = 128
ROPE_AXIS = MLA_ROPE // 2
ROPE_BASE = 10000.0

N_EXPERTS = 64
TOP_K = 8
N_GROUPS = 8
TOPK_GROUPS = 4
ROUTED_SCALE = 2.5
MOE_CHUNK = 4096
MOE_ROWS = 256
MOE_BATCH = 8

IN_SPLITS = (128, 128, 256, 256, 16, 16, 512, 256, 256, 128, 32)
W_GLA, W_CONV, W_S5, W_MLA = 1024, 512, 256, 512

VMEM_LIMIT = 48 * 1024 * 1024

NT_DIMS = (((1,), (1,)), ((), ()))
TN_DIMS = (((0,), (0,)), ((), ()))


def _params(*sem, vmem=VMEM_LIMIT):
    return pltpu.CompilerParams(dimension_semantics=sem, vmem_limit_bytes=vmem)


def _silu(v):
    return v * jax.nn.sigmoid(v)


def _dot(a, b, **kw):
    return jnp.dot(a, b, preferred_element_type=F32, **kw)


def _tile(n, pref):
    return pref if n % pref == 0 else n


def _pack_bf16_pairs(h):
    bits = pltpu.bitcast(h.astype(BF16).astype(F32), jnp.int32)
    w = h.shape[-1] // 2
    return lax.shift_right_logical(bits[:, :w], 16) | bits[:, w:]


def _store_row_slabs(ref, value):
    rows, g = value.shape[0], value.shape[1] // 128
    for j in range(g):
        ref[pl.ds(j, rows, stride=g), :] = value[:, j * 128:(j + 1) * 128]


def _load_row_slabs(ref, rows, g, lead=()):
    return [ref[lead + (pl.ds(j, rows, stride=g), slice(None))] for j in range(g)]


def _unpack_bf16_pairs(words):
    low = jnp.concatenate([pltpu.bitcast(v << 16, F32) for v in words], axis=1)
    high = jnp.concatenate([pltpu.bitcast(v & -65536, F32) for v in words], axis=1)
    return low.astype(BF16), high.astype(BF16)


def _ada_kernel(c_ref, w_ref, b_ref, o_ref):
    o_ref[...] = _dot(_silu(c_ref[...]), w_ref[...], precision=HIGHEST) + b_ref[...]


def ada_mod(cc, w, b):
    rows, dm = cc.shape
    n = w.shape[1]
    tn = _tile(n, 512)
    return pl.pallas_call(
        _ada_kernel, grid=(n // tn,),
        in_specs=[pl.BlockSpec((rows, dm), lambda j: (0, 0)),
                  pl.BlockSpec((dm, tn), lambda j: (0, j)),
                  pl.BlockSpec((1, tn), lambda j: (0, j))],
        out_specs=pl.BlockSpec((rows, tn), lambda j: (0, j)),
        out_shape=jax.ShapeDtypeStruct((rows, n), F32),
        compiler_params=_params("arbitrary"), name="ada_mod")(cc, w, b.reshape(1, n))


def _inproj_kernel(x_ref, sh_ref, sc_ref, g_ref, w_ref, *o_refs):
    x = x_ref[0]
    ms = jnp.mean(x * x, axis=-1, keepdims=True)
    h = (x * lax.rsqrt(ms + EPS) * g_ref[...] * (1.0 + sc_ref[0]) + sh_ref[0]).astype(BF16)
    off = 0
    for o_ref in o_refs:
        w = o_ref.shape[-1]
        o_ref[0] = _dot(h, w_ref[:, off:off + w])
        off += w


def in_projection(x, shift, scale, g, w):
    bsz, seq, dm = x.shape
    tm = _tile(seq, 512)
    widths = (W_GLA, W_CONV, W_S5, W_MLA)
    vec = pl.BlockSpec((1, 1, dm), lambda b, i: (b, 0, 0))
    return pl.pallas_call(
        _inproj_kernel, grid=(bsz, seq // tm),
        in_specs=[pl.BlockSpec((1, tm, dm), lambda b, i: (b, i, 0)), vec, vec,
                  pl.BlockSpec((1, dm), lambda b, i: (0, 0)),
                  pl.BlockSpec(w.shape, lambda b, i: (0, 0))],
        out_specs=[pl.BlockSpec((1, tm, wd), lambda b, i: (b, i, 0)) for wd in widths],
        out_shape=[jax.ShapeDtypeStruct((bsz, seq, wd), F32) for wd in widths],
        compiler_params=_params("parallel", "parallel"), name="in_projection")(x, shift, scale, g.reshape(1, dm), w)


def _rope_swap(t):
    s = t.reshape(t.shape[:-1] + (2, 2, ROPE_AXIS // 2))
    return jnp.stack([-s[..., 1, :], s[..., 0, :]], axis=-2).reshape(t.shape)


def assemble_w_in(w_in):
    edges = np.cumsum((0,) + IN_SPLITS)
    q, k, v, r, a_f, a_b, conv, s5, cq, ckv, kr = [w_in[:, edges[i]:edges[i + 1]] for i in range(len(IN_SPLITS))]
    pad = lambda t, w: jnp.pad(t, ((0, 0), (0, w - t.shape[1])))
    cols = [q, k, v, r, pad(a_f, 128), pad(a_b, 128), conv, s5, cq, ckv, pad(jnp.concatenate([kr, _rope_swap(kr)], 1), 128)]
    return jnp.concatenate(cols, axis=1).astype(BF16)


def _gla_kernel(z_ref, zc_ref, wa_ref, ba_ref, g_ref, o_ref, oc_ref, of_scr, ob_scr, stf_scr, stb_scr,
                *, seq, ctx_len):
    c = GLA_CHUNK
    nh = GLA_HEADS
    kw = nh * GLA_DK
    vw = nh * GLA_DV
    iota = lambda shape, d: lax.broadcasted_iota(jnp.int32, shape, d)
    tri = [(iota((c, c), 1) <= iota((c, c), 0)).astype(F32), (iota((c, c), 1) >= iota((c, c), 0)).astype(F32)]
    r4 = iota((nh * c, c), 0) & (c - 1)
    c4 = iota((nh * c, c), 1)
    causal = [c4 <= r4, c4 >= r4]
    k_of = lambda t: t >> int(math.log2(GLA_DK))
    v_of = lambda t: t >> int(math.log2(GLA_DV))
    k_head = [(k_of(iota((1, kw), 1)) == h).astype(F32) for h in range(nh)]
    v_head = [(v_of(iota((1, vw), 1)) == h).astype(F32) for h in range(nh)]
    st_mask = (v_of(iota((vw, kw), 0)) == k_of(iota((vw, kw), 1))).astype(F32)
    head_mean = (v_of(iota((vw, vw), 0)) == v_of(iota((vw, vw), 1))).astype(F32) * (1.0 / GLA_DV)

    def chunk(ref, s, d, st):
        blk = ref[0, pl.ds(s, c), :]
        q = blk[:, 0:kw] * (GLA_DK ** -0.5)
        k = blk[:, kw:2 * kw]
        vb = blk[:, 2 * kw:2 * kw + vw].astype(BF16)
        a0 = 2 * kw + 2 * vw + 128 * d
        zl = _dot(blk[:, a0:a0 + 128], wa_ref[d], precision=HIGHEST) + ba_ref[d]
        la = (jnp.minimum(zl, 0.0) - jnp.log1p(jnp.exp(-jnp.abs(zl)))) / GLA_TAU
        b = _dot(tri[d], la, precision=HIGHEST)
        b_end = b[c - 1:c, :] if d == 0 else b[0:1, :]
        q_dec = q * jnp.exp(b)
        k_inv = (k * jnp.exp(-b)).astype(BF16)
        k_tail = (k * jnp.exp(b_end - b)).astype(BF16)
        qs = jnp.concatenate([q_dec * k_head[h] for h in range(nh)], axis=0).astype(BF16)
        att = lax.dot_general(qs, k_inv, NT_DIMS, preferred_element_type=F32)
        att = jnp.where(causal[d], att, 0.0).astype(BF16)
        o_all = _dot(att, vb)
        o = o_all[0:c] * v_head[0]
        for h in range(1, nh):
            o = o + o_all[h * c:(h + 1) * c] * v_head[h]
        o = o + lax.dot_general(q_dec.astype(BF16), st.astype(BF16), NT_DIMS, preferred_element_type=F32)
        upd = lax.dot_general(vb, k_tail, TN_DIMS, preferred_element_type=F32)
        return o, st * jnp.exp(b_end) + upd * st_mask

    def scan(ref, n_chunks, base):
        unroll = GLA_UNROLL if n_chunks % GLA_UNROLL == 0 else 1

        @pl.loop(0, n_chunks // unroll)
        def _(it):
            st_f, st_b = stf_scr[...], stb_scr[...]
            for u in range(unroll):
                n = it * unroll + u
                sf = pl.multiple_of(n * c, c)
                sb = pl.multiple_of((n_chunks - 1 - n) * c, c)
                o_f, st_f = chunk(ref, sf, 0, st_f)
                o_b, st_b = chunk(ref, sb, 1, st_b)
                of_scr[pl.ds(pl.multiple_of(base + sf, c), c), :] = o_f
                ob_scr[pl.ds(pl.multiple_of(base + sb, c), c), :] = o_b
            stf_scr[...] = st_f
            stb_scr[...] = st_b

    def finish(ref, out_ref, n_rows, base):
        rows = GLA_FINISH_ROWS if n_rows % GLA_FINISH_ROWS == 0 else c

        @pl.loop(0, n_rows // rows)
        def _(t):
            s = pl.multiple_of(t * rows, rows)
            at = pl.ds(pl.multiple_of(base + s, rows), rows)
            o = of_scr[at, :] + ob_scr[at, :]
            ms = _dot(o * o, head_mean, precision=HIGHEST)
            gate = ref[0, pl.ds(s, rows), 2 * kw + vw:2 * kw + 2 * vw]
            out_ref[0, pl.ds(s, rows), :] = o * lax.rsqrt(ms + EPS) * g_ref[...] * _silu(gate)

    stf_scr[...] = jnp.zeros_like(stf_scr)
    stb_scr[...] = jnp.zeros_like(stb_scr)
    scan(zc_ref, ctx_len // c, 0)
    scan(z_ref, seq // c, ctx_len)
    finish(zc_ref, oc_ref, ctx_len, 0)
    finish(z_ref, o_ref, seq, ctx_len)


def gla_mixer(zg, zgc, w_a2, b_a, norm_g):
    bsz, seq, wd = zg.shape
    ctx_len = zgc.shape[1]
    vw = GLA_HEADS * GLA_DV
    kw = GLA_HEADS * GLA_DK
    wa = jnp.pad(w_a2, ((0, 0), (0, 128 - GLA_LOWRANK), (0, 0)))
    full = lambda shape: pl.BlockSpec(shape, lambda b: (0,) * len(shape))
    return pl.pallas_call(
        functools.partial(_gla_kernel, seq=seq, ctx_len=ctx_len), grid=(bsz,),
        in_specs=[pl.BlockSpec((1, seq, wd), lambda b: (b, 0, 0)),
                  pl.BlockSpec((1, ctx_len, wd), lambda b: (b, 0, 0)),
                  full((2, 128, kw)), full((2, 1, kw)), full((1, vw))],
        out_specs=[pl.BlockSpec((1, seq, vw), lambda b: (b, 0, 0)),
                   pl.BlockSpec((1, ctx_len, vw), lambda b: (b, 0, 0))],
        out_shape=[jax.ShapeDtypeStruct((bsz, seq, vw), F32), jax.ShapeDtypeStruct((bsz, ctx_len, vw), F32)],
        scratch_shapes=[pltpu.VMEM((seq + ctx_len, vw), F32), pltpu.VMEM((seq + ctx_len, vw), F32),
                        pltpu.VMEM((vw, kw), F32), pltpu.VMEM((vw, kw), F32)],
        compiler_params=_params("parallel"), name="gla_mixer")(
            zg, zgc, wa, b_a.reshape(2, 1, kw), norm_g.reshape(1, vw))


def _conv_kernel(u_ref, dw_ref, dwb_ref, lng_ref, lnb_ref, pw_ref, pwb_ref, o_ref, h_scr, *, seq, rows):
    halo = CONV_HALO
    h_scr[0:halo, :] = jnp.zeros((halo, CONV_C), F32)
    h_scr[halo + seq:2 * halo + seq, :] = jnp.zeros((halo, CONV_C), F32)

    @pl.loop(0, seq // rows)
    def _(t):
        s = pl.multiple_of(t * rows, rows)
        u = u_ref[0, pl.ds(s, rows), :]
        h_scr[pl.ds(pl.multiple_of(halo + s, 8), rows), :] = u[:, :CONV_C] * jax.nn.sigmoid(u[:, CONV_C:])

    @pl.loop(0, seq // rows)
    def _(t):
        s = pl.multiple_of(t * rows, rows)
        win = h_scr[pl.ds(s, rows + 2 * halo), :]
        first = halo - CONV_K // 2
        acc = win[first:first + rows] * dw_ref[0:1, :]
        for k in range(1, CONV_K):
            acc = acc + win[first + k:first + k + rows] * dw_ref[k:k + 1, :]
        acc = acc + dwb_ref[...]
        mu = jnp.mean(acc, axis=-1, keepdims=True)
        var = jnp.mean(jnp.square(acc - mu), axis=-1, keepdims=True)
        y = _silu((acc - mu) * lax.rsqrt(var + EPS) * lng_ref[...] + lnb_ref[...])
        o_ref[0, pl.ds(s, rows), :] = _dot(y.astype(BF16), pw_ref[...]) + pwb_ref[...]


def conformer_conv(u, dw_w, dw_b, ln_g, ln_b, pw_w, pw_b):
    bsz, seq, _ = u.shape
    rows = _tile(seq, 128)
    full = lambda shape: pl.BlockSpec(shape, lambda b: (0,) * len(shape))
    row = lambda t: t.reshape(1, CONV_C)
    return pl.pallas_call(
        functools.partial(_conv_kernel, seq=seq, rows=rows), grid=(bsz,),
        in_specs=[pl.BlockSpec((1, seq, 2 * CONV_C), lambda b: (b, 0, 0)),
                  full((CONV_K + 1, CONV_C)), full((1, CONV_C)), full((1, CONV_C)), full((1, CONV_C)),
                  full((CONV_C, CONV_C)), full((1, CONV_C))],
        out_specs=pl.BlockSpec((1, seq, CONV_C), lambda b: (b, 0, 0)),
        out_shape=jax.ShapeDtypeStruct((bsz, seq, CONV_C), F32),
        scratch_shapes=[pltpu.VMEM((seq + 2 * CONV_HALO, CONV_C), F32)],
        compiler_params=_params("parallel"), name="conformer_conv")(
            u, jnp.pad(dw_w, ((0, 1), (0, 0))), row(dw_b), row(ln_g), row(ln_b), pw_w.astype(BF16), row(pw_b))


def s5_matrices(lam_re, lam_im, log_dt, b_re, b_im, c_re, c_im):
    dt = jnp.exp(log_dt)[:, None]
    mag = jnp.exp(lam_re * dt)
    a_re, a_im = mag * jnp.cos(lam_im * dt), mag * jnp.sin(lam_im * dt)
    den = lam_re * lam_re + lam_im * lam_im
    f_re = ((a_re - 1.0) * lam_re + a_im * lam_im) / den
    f_im = (a_im * lam_re - (a_re - 1.0) * lam_im) / den
    bb_re = f_re[..., None] * b_re - f_im[..., None] * b_im
    bb_im = f_re[..., None] * b_im + f_im[..., None] * b_re
    eye = jnp.eye(S5_NG, dtype=F32)
    blk_b = lambda t: jnp.einsum("gph,gk->ghkp", t, eye).reshape(S5_C, S5_STATE)
    blk_c = lambda t: jnp.einsum("ghp,gk->gpkh", t, eye).reshape(S5_STATE, S5_C)
    a = jnp.stack([a_re.reshape(S5_STATE), a_im.reshape(S5_STATE)])
    b_mat = jnp.concatenate([blk_b(bb_re), blk_b(bb_im)], axis=1).astype(BF16)
    c_mat = jnp.concatenate([blk_c(c_re), -blk_c(c_im)], axis=0).astype(BF16)
    return a, b_mat, c_mat


def _s5_scan_kernel(u_ref, a_ref, b_ref, c_ref, h0_ref, y_ref, hl_ref, hs_scr, st_scr, *, steps, bsz, reverse):
    i = pl.program_id(0)
    ns = S5_STATE

    @pl.when(i == 0)
    def _():
        st_scr[...] = h0_ref[...]

    u = u_ref[...].reshape(steps * bsz, S5_C).astype(BF16)
    hs_scr[...] = _dot(u, b_ref[...])
    for s0 in range(0, ns, S5_STRIP):
        a_re = a_ref[0:1, s0:s0 + S5_STRIP]
        a_im = a_ref[1:2, s0:s0 + S5_STRIP]

        def step(j, carry):
            h_re, h_im = carry
            t = (steps - 1 - j) if reverse else j
            r0 = pl.multiple_of(t * bsz, bsz)
            n_re = a_re * h_re - a_im * h_im + hs_scr[pl.ds(r0, bsz), s0:s0 + S5_STRIP]
            n_im = a_re * h_im + a_im * h_re + hs_scr[pl.ds(r0, bsz), ns + s0:ns + s0 + S5_STRIP]
            hs_scr[pl.ds(r0, bsz), s0:s0 + S5_STRIP] = n_re
            hs_scr[pl.ds(r0, bsz), ns + s0:ns + s0 + S5_STRIP] = n_im
            return n_re, n_im

        h_re, h_im = lax.fori_loop(0, steps, step, (st_scr[:, s0:s0 + S5_STRIP], st_scr[:, ns + s0:ns + s0 + S5_STRIP]))
        st_scr[:, s0:s0 + S5_STRIP] = h_re
        st_scr[:, ns + s0:ns + s0 + S5_STRIP] = h_im
    y_ref[...] = _dot(hs_scr[...].astype(BF16), c_ref[...]).reshape(steps, bsz, S5_C)
    hl_ref[...] = st_scr[...]


def s5_scan(u_tm, mats, h0, reverse):
    a, b_mat, c_mat = mats
    seq, bsz, _ = u_tm.shape
    steps = _tile(seq, 32)
    n = seq // steps
    order = (lambda i: (n - 1 - i, 0, 0)) if reverse else (lambda i: (i, 0, 0))
    full = lambda shape: pl.BlockSpec(shape, lambda i: (0,) * len(shape))
    return pl.pallas_call(
        functools.partial(_s5_scan_kernel, steps=steps, bsz=bsz, reverse=reverse), grid=(n,),
        in_specs=[pl.BlockSpec((steps, bsz, S5_C), order), full(a.shape), full(b_mat.shape), full(c_mat.shape),
                  full(h0.shape)],
        out_specs=[pl.BlockSpec((steps, bsz, S5_C), order), full(h0.shape)],
        out_shape=[jax.ShapeDtypeStruct((seq, bsz, S5_C), F32), jax.ShapeDtypeStruct(h0.shape, F32)],
        scratch_shapes=[pltpu.VMEM((steps * bsz, 2 * S5_STATE), F32), pltpu.VMEM(h0.shape, F32)],
        compiler_params=_params("arbitrary"), name="s5_scan")(u_tm, a, b_mat, c_mat, h0)


def _s5_out_kernel(u_ref, yf_ref, yb_ref, d_ref, w_ref, b_ref, o_ref):
    y = d_ref[...] * u_ref[...] + yf_ref[...] + yb_ref[...]
    z = _dot(y.astype(BF16), w_ref[...]) + b_ref[...]
    o_ref[...] = z[:, :S5_C] * jax.nn.sigmoid(z[:, S5_C:])


def s5_output(u, y_f, y_b, d_skip, glu_w, glu_b):
    n = u.shape[0]
    tm = _tile(n, 1024)
    rows = pl.BlockSpec((tm, S5_C), lambda i: (i, 0))
    full = lambda shape: pl.BlockSpec(shape, lambda i: (0,) * len(shape))
    return pl.pallas_call(
        _s5_out_kernel, grid=(n // tm,),
        in_specs=[rows, rows, rows, full((1, S5_C)), full((S5_C, 2 * S5_C)), full((1, 2 * S5_C))],
        out_specs=rows, out_shape=jax.ShapeDtypeStruct((n, S5_C), F32),
        compiler_params=_params("parallel"), name="s5_output")(
            u, y_f, y_b, d_skip.reshape(1, S5_C), glu_w.astype(BF16), glu_b.reshape(1, 2 * S5_C))


def s5_mixer(zs, zsc, lam_re, lam_im, log_dt, b_re, b_im, c_re, c_im, d_skip, glu_w, glu_b):
    bsz, seq, _ = zs.shape
    ctx_len = zsc.shape[1]
    u = jnp.transpose(zs, (1, 0, 2))
    uc = jnp.transpose(zsc, (1, 0, 2))
    zeros = jnp.zeros((bsz, 2 * S5_STATE), F32)
    ys, ycs = [], []
    for d in range(2):
        mats = s5_matrices(lam_re[d], lam_im[d], log_dt[d], b_re[d], b_im[d], c_re[d], c_im[d])
        yc, hc = s5_scan(uc, mats, zeros, reverse=d == 1)
        y, _ = s5_scan(u, mats, hc, reverse=d == 1)
        ys.append(y)
        ycs.append(yc)
    flat = lambda t: t.reshape(-1, S5_C)
    o = s5_output(flat(u), flat(ys[0]), flat(ys[1]), d_skip, glu_w, glu_b).reshape(seq, bsz, S5_C)
    oc = s5_output(flat(uc), flat(ycs[0]), flat(ycs[1]), d_skip, glu_w, glu_b).reshape(ctx_len, bsz, S5_C)
    return jnp.transpose(o, (1, 0, 2)), jnp.transpose(oc, (1, 0, 2))


def rope_tables(seq_len, rotate):
    if not rotate:
        return jnp.ones((seq_len, MLA_ROPE), F32), jnp.zeros((seq_len, MLA_ROPE), F32)
    rows = seq_len // GRID_W
    row = jnp.broadcast_to(jnp.arange(rows, dtype=F32)[:, None], (rows, GRID_W)).reshape(seq_len)
    col = jnp.broadcast_to(jnp.arange(GRID_W, dtype=F32)[None, :], (rows, GRID_W)).reshape(seq_len)
    inv_freq = ROPE_BASE ** (-jnp.arange(ROPE_AXIS // 2, dtype=F32) / (ROPE_AXIS // 2))
    ang = jnp.stack([row[:, None] * inv_freq, col[:, None] * inv_freq], axis=1)
    full = lambda t: jnp.broadcast_to(t[:, :, None, :], (seq_len, 2, 2, ROPE_AXIS // 2)).reshape(seq_len, MLA_ROPE)
    return full(jnp.cos(ang)), full(jnp.sin(ang))


def mla_tables(seq_len, rotate):
    cos, sin = rope_tables(seq_len, rotate)
    head = lambda rope, fill: jnp.concatenate(
        [jnp.full((seq_len, MLA_NOPE), fill, F32), rope, jnp.zeros((seq_len, MLA_HEAD_PAD - MLA_NOPE - MLA_ROPE), F32)], 1)
    cos_q = jnp.tile(head(cos, 1.0), (1, MLA_HEADS))
    sin_q = jnp.tile(head(sin, 0.0), (1, MLA_HEADS))
    cs_k = jnp.concatenate([cos, sin, jnp.zeros((seq_len, 128 - 2 * MLA_ROPE), F32)], 1)
    return cos_q, sin_q, cs_k


def mla_weights(w_uq, w_ukv):
    hd = MLA_NOPE + MLA_ROPE
    zq = jnp.zeros((MLA_Q_LORA, MLA_HEAD_PAD - hd), F32)
    zn = jnp.zeros((MLA_Q_LORA, MLA_NOPE), F32)
    wq, wq_sw, wk, wv = [], [], [], []
    for h in range(MLA_HEADS):
        qh = w_uq[:, h * hd:(h + 1) * hd]
        wq += [qh, zq]
        wq_sw += [zn, _rope_swap(qh[:, MLA_NOPE:]), zq]
        kvh = w_ukv[:, h * (MLA_NOPE + MLA_V):(h + 1) * (MLA_NOPE + MLA_V)]
        wk += [kvh[:, :MLA_NOPE], jnp.zeros((MLA_KV_LORA, MLA_HEAD_PAD - MLA_NOPE), F32)]
        wv += [kvh[:, MLA_NOPE:]]
    place = np.zeros((128, MLA_HEADS * MLA_HEAD_PAD), np.float32)
    for h in range(MLA_HEADS):
        for j in range(MLA_ROPE):
            place[j, h * MLA_HEAD_PAD + MLA_NOPE + j] = 1.0
            place[MLA_ROPE + j, h * MLA_HEAD_PAD + MLA_NOPE + j] = 1.0
    cat = lambda ts: jnp.concatenate(ts, axis=1).astype(BF16)
    return cat(wq), cat(wq_sw), cat(wk), cat(wv), jnp.asarray(place)


def _mla_prep_kernel(z_ref, qg_ref, kg_ref, wq_ref, wqs_ref, wk_ref, wv_ref, pl_ref, cq_ref, sq_ref, csk_ref,
                     q_ref, k_ref, v_ref):
    z = z_ref[0]
    norm = lambda t, g: (t * lax.rsqrt(jnp.mean(t * t, axis=-1, keepdims=True) + EPS) * g).astype(BF16)
    cq = norm(z[:, :MLA_Q_LORA], qg_ref[...])
    ckv = norm(z[:, MLA_Q_LORA:MLA_Q_LORA + MLA_KV_LORA], kg_ref[...])
    scale = (MLA_NOPE + MLA_ROPE) ** -0.5
    q = _dot(cq, wq_ref[...]) * cq_ref[...] + _dot(cq, wqs_ref[...]) * sq_ref[...]
    q_ref[0] = (q * scale).astype(BF16)
    kr = z[:, MLA_Q_LORA + MLA_KV_LORA:] * csk_ref[...]
    k_ref[0] = (_dot(ckv, wk_ref[...]) + _dot(kr, pl_ref[...], precision=HIGHEST)).astype(BF16)
    v_ref[0] = _dot(ckv, wv_ref[...]).astype(BF16)


def mla_prep(zm, qn_g, kvn_g, weights, tables):
    bsz, seq, wd = zm.shape
    tm = _tile(seq, 512)
    wq, wq_sw, wk, wv, place = weights
    cos_q, sin_q, cs_k = tables
    qw, vw = MLA_HEADS * MLA_HEAD_PAD, MLA_HEADS * MLA_V
    full = lambda t: pl.BlockSpec(t.shape, lambda b, i: (0,) * t.ndim)
    pos = lambda t: pl.BlockSpec((tm, t.shape[1]), lambda b, i: (i, 0))
    out = lambda w: pl.BlockSpec((1, tm, w), lambda b, i: (b, i, 0))
    qg, kg = qn_g.reshape(1, -1), kvn_g.reshape(1, -1)
    return pl.pallas_call(
        _mla_prep_kernel, grid=(bsz, seq // tm),
        in_specs=[pl.BlockSpec((1, tm, wd), lambda b, i: (b, i, 0)), full(qg), full(kg), full(wq), full(wq_sw),
                  full(wk), full(wv), full(place), pos(cos_q), pos(sin_q), pos(cs_k)],
        out_specs=[out(qw), out(qw), out(vw)],
        out_shape=[jax.ShapeDtypeStruct((bsz, seq, qw), BF16), jax.ShapeDtypeStruct((bsz, seq, qw), BF16),
                   jax.ShapeDtypeStruct((bsz, seq, vw), BF16)],
        compiler_params=_params("parallel", "parallel"), name="mla_prep")(
            zm, qg, kg, wq, wq_sw, wk, wv, place, cos_q, sin_q, cs_k)


def _mla_attn_kernel(q_ref, *refs, n_seg):
    k_refs, v_refs, o_ref = refs[:n_seg], refs[n_seg:2 * n_seg], refs[2 * n_seg]
    outs = []
    for h in range(MLA_HEADS):
        q = q_ref[0, :, h * MLA_HEAD_PAD:(h + 1) * MLA_HEAD_PAD]
        s = [lax.dot_general(q, k_ref[0, :, h * MLA_HEAD_PAD:(h + 1) * MLA_HEAD_PAD], NT_DIMS,
                             preferred_element_type=F32) for k_ref in k_refs]
        m = functools.reduce(jnp.maximum, [jnp.max(t, axis=-1, keepdims=True) for t in s])
        p = [jnp.exp(t - m) for t in s]
        den = functools.reduce(jnp.add, [jnp.sum(t, axis=-1, keepdims=True) for t in p])
        o = functools.reduce(jnp.add, [_dot(t.astype(BF16), v_ref[0, :, h * MLA_V:(h + 1) * MLA_V])
                                       for t, v_ref in zip(p, v_refs)])
        outs.append(o / den)
    o_ref[0] = jnp.concatenate(outs, axis=-1)


def mla_attention(q, ks, vs):
    bsz, seq, qw = q.shape
    tq = _tile(seq, 256)
    vw = MLA_HEADS * MLA_V
    seg = lambda t: pl.BlockSpec((1,) + t.shape[1:], lambda b, i: (b, 0, 0))
    return pl.pallas_call(
        functools.partial(_mla_attn_kernel, n_seg=len(ks)), grid=(bsz, seq // tq),
        in_specs=[pl.BlockSpec((1, tq, qw), lambda b, i: (b, i, 0))] + [seg(t) for t in ks] + [seg(t) for t in vs],
        out_specs=pl.BlockSpec((1, tq, vw), lambda b, i: (b, i, 0)),
        out_shape=jax.ShapeDtypeStruct((bsz, seq, vw), F32),
        compiler_params=_params("parallel", "parallel"), name="mla_attention")(q, *ks, *vs)


def _outproj_kernel(a_ref, b_ref, c_ref, d_ref, w_ref, x_ref, g1_ref, sh_ref, sc_ref, g_ref, rw_ref,
                    xo_ref, h_ref, lg_ref):
    mix = None
    for j, o_ref in enumerate((a_ref, b_ref, c_ref, d_ref)):
        wd = o_ref.shape[-1]
        part = _dot(o_ref[0].astype(BF16), w_ref[j * wd:(j + 1) * wd, :])
        mix = part if mix is None else mix + part
    x = x_ref[0] + g1_ref[0] * mix
    xo_ref[0] = x
    ms = jnp.mean(x * x, axis=-1, keepdims=True)
    h = x * lax.rsqrt(ms + EPS) * g_ref[...] * (1.0 + sc_ref[0]) + sh_ref[0]
    _store_row_slabs(h_ref, _pack_bf16_pairs(h))
    lg_ref[...] =lax.dot_general(rw_ref[...], h, NT_DIMS, precision=HIGHEST, preferred_element_type=F32)


def out_projection(parts, w_out, x, g1, shift, scale, g, router_w):
    bsz, seq, dm = x.shape
    tm = _tile(seq, 512)
    nt = seq // tm
    ne = router_w.shape[1]
    groups = dm // 2 // 128
    rows = lambda w: pl.BlockSpec((1, tm, w), lambda b, i: (b, i, 0))
    vec = pl.BlockSpec((1, 1, dm), lambda b, i: (b, 0, 0))
    full = lambda shape: pl.BlockSpec(shape, lambda b, i: (0,) * len(shape))
    return pl.pallas_call(
        _outproj_kernel, grid=(bsz, nt),
        in_specs=[rows(p.shape[-1]) for p in parts] + [full(w_out.shape), rows(dm), vec, vec, vec, full((1, dm)),
                                                       full((ne, dm))],
        out_specs=[rows(dm), pl.BlockSpec((tm * groups, 128), lambda b, i: (b * nt + i, 0)),
                   pl.BlockSpec((ne, tm), lambda b, i: (0, b * nt + i))],
        out_shape=[jax.ShapeDtypeStruct((bsz, seq, dm), F32), jax.ShapeDtypeStruct((bsz * seq * groups, 128), jnp.int32),
                   jax.ShapeDtypeStruct((ne, bsz * seq), F32)],
        compiler_params=_params("parallel", "parallel"), name="out_projection")(
            *parts, w_out.astype(BF16), x, g1, shift, scale, g.reshape(1, dm), router_w.T)


def _route_kernel(lg_ref, b_ref, e_ref, w_ref):
    ne, tt = lg_ref.shape
    per = ne // N_GROUPS
    neg = -jnp.inf
    scores = jax.nn.sigmoid(lg_ref[...])
    biased = scores + b_ref[...]
    v3 = biased.reshape(N_GROUPS, per, tt)
    e_in = lax.broadcasted_iota(jnp.int32, v3.shape, 1).astype(F32)
    m1 = jnp.max(v3, axis=1, keepdims=True)
    i1 = jnp.min(jnp.where(v3 == m1, e_in, float(per)), axis=1, keepdims=True)
    m2 = jnp.max(jnp.where(e_in == i1, neg, v3), axis=1, keepdims=True)
    grp = (m1 + m2).reshape(N_GROUPS, tt)

    def pick(cur, count):
        ids = lax.broadcasted_iota(jnp.int32, cur.shape, 0).astype(F32)
        marks = jnp.zeros(cur.shape, F32)
        picked = []
        for _ in range(count):
            m = jnp.max(cur, axis=0, keepdims=True)
            first = jnp.min(jnp.where(cur == m, ids, float(cur.shape[0])), axis=0, keepdims=True)
            hit = ids == first
            marks = jnp.where(hit, 1.0, marks)
            cur = jnp.where(hit, neg, cur)
            picked.append(first)
        return marks, picked

    grp_on, _ = pick(grp, TOPK_GROUPS)
    exp_on = jnp.broadcast_to(grp_on.reshape(N_GROUPS, 1, tt), v3.shape).reshape(ne, tt)
    chosen, picked = pick(jnp.where(exp_on > 0.0, biased, neg), TOP_K)
    w = scores * chosen
    gate = w / jnp.sum(w, axis=0, keepdims=True) * ROUTED_SCALE
    ids = lax.broadcasted_iota(jnp.int32, gate.shape, 0).astype(F32)
    e_ref[...] = jnp.concatenate(picked, axis=0).astype(jnp.int32)
    w_ref[...] = jnp.concatenate([jnp.sum(jnp.where(ids == p, gate, 0.0), axis=0, keepdims=True) for p in picked], axis=0)


def route(logits_t, router_b):
    ne, n = logits_t.shape
    tt = _tile(n, 1024)
    out = pl.BlockSpec((TOP_K, tt), lambda i: (0, i))
    return pl.pallas_call(
        _route_kernel, grid=(n // tt,),
        in_specs=[pl.BlockSpec((ne, tt), lambda i: (0, i)), pl.BlockSpec((ne, 1), lambda i: (0, 0))],
        out_specs=[out, out],
        out_shape=[jax.ShapeDtypeStruct((TOP_K, n), jnp.int32), jax.ShapeDtypeStruct((TOP_K, n), F32)],
        compiler_params=_params("parallel"), name="route")(logits_t, router_b.reshape(ne, 1))


def moe_plan(eid, wgt, chunk, xg, yg):
    n = eid.shape[1]
    nc, na = n // chunk, chunk * TOP_K
    ns = na // MOE_ROWS + N_EXPERTS
    key = eid.T.reshape(nc, na) * na + jnp.arange(na, dtype=jnp.int32)
    skey, sw = lax.sort((key, wgt.T.reshape(nc, na)), dimension=1, num_keys=1)
    e_sorted, tok = skey // na, (skey % na) // TOP_K
    experts = jnp.arange(N_EXPERTS, dtype=jnp.int32)
    ends = jnp.sum(e_sorted[:, None, :] <= experts[None, :, None], axis=-1, dtype=jnp.int32)
    starts = jnp.concatenate([jnp.zeros((nc, 1), jnp.int32), ends[:, :-1]], axis=1)
    pad_end = jnp.cumsum(-(-(ends - starts) // MOE_ROWS) * MOE_ROWS, axis=1)
    pad_start = jnp.concatenate([jnp.zeros((nc, 1), jnp.int32), pad_end[:, :-1]], axis=1)
    first = jnp.arange(ns, dtype=jnp.int32) * MOE_ROWS
    exp = jnp.minimum(jnp.sum(pad_end[:, None, :] <= first[None, :, None], axis=-1, dtype=jnp.int32), N_EXPERTS - 1)
    active = first[None, :] < pad_end[:, -1:]
    take = lambda t, i: jnp.take_along_axis(t, i, axis=1)
    rank = first[None, :, None] + jnp.arange(MOE_ROWS, dtype=jnp.int32) - take(pad_start, exp)[..., None]
    real = (rank < take(ends - starts, exp)[..., None]) & active[..., None]
    src = jnp.clip(take(starts, exp)[..., None] + rank, 0, na - 1).reshape(nc, ns * MOE_ROWS)
    rows = lambda t: take(t, src).reshape(nc, ns, MOE_ROWS)
    flat = lambda t: t.reshape(nc * ns, 1, MOE_ROWS)
    gather_at = flat(jnp.where(real, rows(tok), 0) * xg)
    scatter_at = flat(jnp.where(real, rows(tok), chunk) * yg)
    weight = flat(jnp.where(real, rows(sw), 0.0))
    return gather_at, scatter_at, weight, exp.reshape(nc * ns), active.astype(jnp.int32).reshape(nc * ns)


def _moe_tick(x_ref, y_ref, gx_ref, sy_ref, w_ref, wg_ref, wu_ref, wd_ref, gather_to, mm_from, mm_to, scatter_from):
    bm = MOE_ROWS
    stride = bm + 1
    xg = mm_from.shape[0] // stride
    yg = mm_to.shape[0] // stride
    for mi in range(bm):
        gather_to[pl.ds(mi, xg, stride=stride), :] = x_ref[0, pl.ds(pl.multiple_of(gx_ref[0, 0, mi], xg), xg), :]
    low, high = _unpack_bf16_pairs([mm_from[j * stride:j * stride + bm, :] for j in range(xg)])
    half = low.shape[1]
    up = lambda w_ref: _dot(low, w_ref[0, :half, :]) + _dot(high, w_ref[0, half:, :])
    act = (_silu(up(wg_ref)) * up(wu_ref)).astype(BF16)
    out = _dot(act, wd_ref[0])
    for j in range(yg):
        mm_to[j * stride:j * stride + bm, :] = out[:, j * 128:(j + 1) * 128]
    for base in range(0, bm, MOE_BATCH):
        at = [pl.multiple_of(sy_ref[0, 0, mi], yg) for mi in range(base, base + MOE_BATCH)]
        new = [y_ref[0, pl.ds(i, yg), :] + w_ref[0, 0, mi] * scatter_from[pl.ds(mi, yg, stride=stride), :]
               for i, mi in zip(at, range(base, base + MOE_BATCH))]
        for i, v in zip(at, new):
            y_ref[0, pl.ds(i, yg), :] = v


def _moe_routed_kernel(exp_ref, act_ref, x_ref, gxa_ref, sya_ref, wa_ref, gxb_ref, syb_ref, wb_ref,
                       wga_ref, wua_ref, wda_ref, wgb_ref, wub_ref, wdb_ref, y_ref, xt0, xt1, ot0, ot1, *, ns):
    c, g = pl.program_id(0), pl.program_id(1)

    @pl.when(g == 0)
    def _():
        y_ref[...] = jnp.zeros_like(y_ref)

    @pl.when((c == 0) & (g == 0))
    def _():
        xt1[...] = jnp.zeros_like(xt1)
        ot0[...] = jnp.zeros_like(ot0)
        ot1[...] = jnp.zeros_like(ot1)

    def live(step):
        return (step >= 0) & (step < ns) & (act_ref[c * ns + jnp.clip(step, 0, ns - 1)] != 0)

    t = 2 * g

    @pl.when(live(t - 2) | live(t - 1) | live(t) | live(t + 1))
    def _():
        _moe_tick(x_ref, y_ref, gxa_ref, sya_ref, wa_ref, wga_ref, wua_ref, wda_ref,
                  gather_to=xt0, mm_from=xt1, mm_to=ot1, scatter_from=ot0)
        _moe_tick(x_ref, y_ref, gxb_ref, syb_ref, wb_ref, wgb_ref, wub_ref, wdb_ref,
                  gather_to=xt1, mm_from=xt0, mm_to=ot0, scatter_from=ot1)


def moe_routed(slabs, eid, wgt, wg, wu, wd):
    n = eid.shape[1]
    xg = slabs.shape[0] // n
    yg = 2 * xg
    dm, ff = wg.shape[1], wg.shape[2]
    chunk = _tile(n, MOE_CHUNK)
    nc = n // chunk
    ns = chunk * TOP_K // MOE_ROWS + N_EXPERTS
    assert ns % 2 == 0
    last = ns - 1
    gather_at, scatter_at, weight, exp, active = moe_plan(eid, wgt, chunk, xg, yg)
    smem = lambda step: pl.BlockSpec((1, 1, MOE_ROWS), lambda c, g, e, a: (c * ns + step(g), 0, 0), memory_space=pltpu.SMEM)
    expert = lambda shape, step: pl.BlockSpec((1,) + shape, lambda c, g, e, a: (e[c * ns + step(g)], 0, 0))
    gathered = lambda tick: lambda g: jnp.minimum(2 * g + tick, last)
    computed = lambda tick: lambda g: jnp.clip(2 * g + tick - 1, 0, last)
    scattered = lambda tick: lambda g: jnp.where(2 * g + tick >= 2, 2 * g + tick - 2, last)
    once = pl.Buffered(1)
    stage = lambda groups, dtype: pltpu.VMEM((-(-groups * (MOE_ROWS + 1) // 8) * 8, 128), dtype)
    return pl.pallas_call(
        functools.partial(_moe_routed_kernel, ns=ns),
        grid_spec=pltpu.PrefetchScalarGridSpec(
            num_scalar_prefetch=2, grid=(nc, (ns + 2) // 2),
            in_specs=[pl.BlockSpec((1, chunk * xg, 128), lambda c, g, e, a: (c, 0, 0), pipeline_mode=once),
                      smem(gathered(0)), smem(scattered(0)), smem(scattered(0)),
                      smem(gathered(1)), smem(scattered(1)), smem(scattered(1)),
                      expert((dm, ff), computed(0)), expert((dm, ff), computed(0)), expert((ff, dm), computed(0)),
                      expert((dm, ff), computed(1)), expert((dm, ff), computed(1)), expert((ff, dm), computed(1))],
            out_specs=pl.BlockSpec((1, chunk * yg + 8, 128), lambda c, g, e, a: (c, 0, 0), pipeline_mode=once),
            scratch_shapes=[stage(xg, jnp.int32), stage(xg, jnp.int32), stage(yg, F32), stage(yg, F32)]),
        out_shape=jax.ShapeDtypeStruct((nc, chunk * yg + 8, 128), F32),
        compiler_params=_params("arbitrary", "arbitrary"), name="moe_routed")(
            exp, active, slabs.reshape(nc, chunk * xg, 128), gather_at, scatter_at, weight, gather_at, scatter_at, weight,
            wg, wu, wd, wg, wu, wd)


def _moe_finish_kernel(y_ref, h_ref, x_ref, g2_ref, wg_ref, wu_ref, wd_ref, fg_ref, o_ref, *, final_norm):
    tm = x_ref.shape[0]
    low, high = _unpack_bf16_pairs(_load_row_slabs(h_ref, tm, h_ref.shape[0] // tm))
    routed = jnp.concatenate(_load_row_slabs(y_ref, tm, y_ref.shape[1] // tm, lead=(0,)), axis=1)
    half = low.shape[1]
    up = lambda w_ref: _dot(low, w_ref[:half, :]) + _dot(high, w_ref[half:, :])
    shared = _dot((_silu(up(wg_ref)) * up(wu_ref)).astype(BF16), wd_ref[...])
    x = x_ref[...] + g2_ref[0] * (routed + shared)
    if final_norm:
        x = x * lax.rsqrt(jnp.mean(x * x, axis=-1, keepdims=True) + EPS) * fg_ref[...]
    o_ref[...] = x


def moe_finish(y, slabs, x, g2, sg, su, sd, final_g, final_norm):
    bsz, seq, dm = x.shape
    n = bsz * seq
    tm = _tile(seq, 512)
    nt = seq // tm
    xg, yg = slabs.shape[0] // n, dm // 128
    per_chunk = (y.shape[1] - 8) // yg // tm
    rows = lambda w: pl.BlockSpec((tm, w), lambda i: (i, 0))
    full = lambda t: pl.BlockSpec(t.shape, lambda i: (0,) * t.ndim)
    sg, su, sd, fg = sg.astype(BF16), su.astype(BF16), sd.astype(BF16), final_g.reshape(1, dm)
    out = pl.pallas_call(
        functools.partial(_moe_finish_kernel, final_norm=final_norm), grid=(n // tm,),
        in_specs=[pl.BlockSpec((1, tm * yg, 128), lambda i: (i // per_chunk, i % per_chunk, 0)),
                  pl.BlockSpec((tm * xg, 128), lambda i: (i, 0)), rows(dm),
                  pl.BlockSpec((1, 1, dm), lambda i: (i // nt, 0, 0)), full(sg), full(su), full(sd), full(fg)],
        out_specs=rows(dm), out_shape=jax.ShapeDtypeStruct((n, dm), F32),
        compiler_params=_params("parallel"), name="moe_finish")(y, slabs, x.reshape(n, dm), g2, sg, su, sd, fg)
    return out.reshape(bsz, seq, dm)


def kernel(x, c, ctx, c_ctx, ada_w, ada_b, norm1_g, norm2_g, w_in, w_out, gla_w_a2, gla_b_a, gla_norm_g,
           conv_dw_w, conv_dw_b, conv_ln_g, conv_ln_b, conv_pw_w, conv_pw_b,
           s5_lam_re, s5_lam_im, s5_log_dt, s5_b_re, s5_b_im, s5_c_re, s5_c_im, s5_d, s5_glu_w, s5_glu_b,
           mla_qn_g, mla_kvn_g, mla_w_uq, mla_w_ukv,
           moe_router_w, moe_router_b, moe_w_gate, moe_w_up, moe_w_down,
           shared_w_gate, shared_w_up, shared_w_down, final_g):
    bsz, seq, dm = x.shape
    ctx_len = ctx.shape[1]
    depth = ada_w.shape[0]
    cc = jnp.concatenate([c, c_ctx[None], jnp.zeros((-(bsz + 1) % 8, dm), F32)], axis=0)
    tables = mla_tables(seq, True)
    tables_c = mla_tables(ctx_len, False)
    xc = ctx
    for l in range(depth):
        last = l == depth - 1
        mod = ada_mod(cc, ada_w[l], ada_b[l])
        lat = [t.reshape(bsz, 1, dm) for t in jnp.split(mod[:bsz], 6, axis=-1)]
        con = [jnp.broadcast_to(t.reshape(1, 1, dm), (bsz, 1, dm)) for t in jnp.split(mod[bsz], 6)]
        sh1, sc1, g1, sh2, sc2, g2 = lat
        sh1c, sc1c, g1c, sh2c, sc2c, g2c = con
        w_in_l = assemble_w_in(w_in[l])
        zg, zv, zs, zm = in_projection(x, sh1, sc1, norm1_g[l], w_in_l)
        zgc, zvc, zsc, zmc = in_projection(xc, sh1c, sc1c, norm1_g[l], w_in_l)
        conv_p = (conv_dw_w[l], conv_dw_b[l], conv_ln_g[l], conv_ln_b[l], conv_pw_w[l], conv_pw_b[l])
        o_gla, oc_gla = gla_mixer(zg, zgc, gla_w_a2[l], gla_b_a[l], gla_norm_g[l])
        o_conv = conformer_conv(zv, *conv_p)
        o_s5, oc_s5 = s5_mixer(zs, zsc, s5_lam_re[l], s5_lam_im[l], s5_log_dt[l], s5_b_re[l], s5_b_im[l],
                               s5_c_re[l], s5_c_im[l], s5_d[l], s5_glu_w[l], s5_glu_b[l])
        mla_w = mla_weights(mla_w_uq[l], mla_w_ukv[l])
        q, k, v = mla_prep(zm, mla_qn_g[l], mla_kvn_g[l], mla_w, tables)
        qc, kc, vc = mla_prep(zmc, mla_qn_g[l], mla_kvn_g[l], mla_w, tables_c)
        o_mla = mla_attention(q, [k, kc], [v, vc])
        experts = (moe_w_gate[l].astype(BF16), moe_w_up[l].astype(BF16), moe_w_down[l].astype(BF16))
        shared = (shared_w_gate[l], shared_w_up[l], shared_w_down[l])

        def ffn(parts, x_in, gate1, shift, scale, gate2, final_norm):
            x_mid, slabs, logits = out_projection(parts, w_out[l], x_in, gate1, shift, scale, norm2_g[l], moe_router_w[l])
            y = moe_routed(slabs, *route(logits, moe_router_b[l]), *experts)
            return moe_finish(y, slabs, x_mid, gate2, *shared, final_g, final_norm)

        if not last:
            oc_conv = conformer_conv(zvc, *conv_p)
            oc_mla = mla_attention(qc, [kc], [vc])
            xc = ffn([oc_gla, oc_conv, oc_s5, oc_mla], xc, g1c, sh2c, sc2c, g2c, False)
        x = ffn([o_gla, o_conv, o_s5, o_mla], x, g1, sh2, sc2, g2, last)
    return x
```

```python
import functools
import math

import numpy as np
import jax
import jax.numpy as jnp
from jax import lax
from jax.experimental import pallas as pl
from jax.experimental.pallas import tpu as pltpu

F32 = jnp.float32
BF16 = jnp.bfloat16
HIGHEST = lax.Precision.HIGHEST
EPS = 1e-6

GRID_W = 64

GLA_HEADS = 4
GLA_DK = 32
GLA_DV = 64
GLA_LOWRANK = 16
GLA_TAU = 16.0
GLA_CHUNK = 64
GLA_UNROLL = 2
GLA_FINISH_ROWS = 256

CONV_C = 256
CONV_K = 31
CONV_HALO = 16

S5_C = 256
S5_GROUP = 16
S5_NG = 16
S5_P = 64
S5_STATE = S5_NG * S5_P
S5_STRIP = 256

MLA_HEADS = 4
MLA_NOPE = 64
MLA_ROPE = 32
MLA_V = 64
MLA_Q_LORA = 256
MLA_KV_LORA = 128
MLA_HEAD_PAD = 128
ROPE_AXIS = MLA_ROPE // 2
ROPE_BASE = 10000.0

N_EXPERTS = 64
TOP_K = 8
N_GROUPS = 8
TOPK_GROUPS = 4
ROUTED_SCALE = 2.5
MOE_CHUNK = 4096
MOE_ROWS = 256
MOE_BATCH = 8

IN_SPLITS = (128, 128, 256, 256, 16, 16, 512, 256, 256, 128, 32)
W_GLA, W_CONV, W_S5, W_MLA = 1024, 512, 256, 512

VMEM_LIMIT = 48 * 1024 * 1024

NT_DIMS = (((1,), (1,)), ((), ()))
TN_DIMS = (((0,), (0,)), ((), ()))


def _params(*sem, vmem=VMEM_LIMIT):
    return pltpu.CompilerParams(dimension_semantics=sem, vmem_limit_bytes=vmem)


def _silu(v):
    return v * jax.nn.sigmoid(v)


def _dot(a, b, **kw):
    return jnp.dot(a, b, preferred_element_type=F32, **kw)


def _split_bf16(x):
    hi = x.astype(BF16)
    rest = x - hi.astype(F32)
    mid = rest.astype(BF16)
    return hi, mid, (rest - mid.astype(F32)).astype(BF16)


def _dot_exact_lhs(m, x):
    n = x.shape[1]
    prod = _dot(m, jnp.concatenate(_split_bf16(x), axis=1))
    return prod[:, :n] + prod[:, n:2 * n] + prod[:, 2 * n:]


def _dot_exact_rhs(x, m):
    n = x.shape[0]
    prod = _dot(jnp.concatenate(_split_bf16(x), axis=0), m)
    return prod[:n] + prod[n:2 * n] + prod[2 * n:]


def _tile(n, pref):
    return pref if n % pref == 0 else n


def _pack_bf16_pairs(h):
    bits = pltpu.bitcast(h.astype(BF16).astype(F32), jnp.int32)
    w = h.shape[-1] // 2
    return lax.shift_right_logical(bits[:, :w], 16) | bits[:, w:]


def _store_row_slabs(ref, value):
    rows, g = value.shape[0], value.shape[1] // 128
    for j in range(g):
        ref[pl.ds(j, rows, stride=g), :] = value[:, j * 128:(j + 1) * 128]


def _load_row_slabs(ref, rows, g, lead=()):
    return [ref[lead + (pl.ds(j, rows, stride=g), slice(None))] for j in range(g)]


def _unpack_bf16_pairs(words):
    low = jnp.concatenate([pltpu.bitcast(v << 16, F32) for v in words], axis=1)
    high = jnp.concatenate([pltpu.bitcast(v & -65536, F32) for v in words], axis=1)
    return low.astype(BF16), high.astype(BF16)


def _ada_kernel(c_ref, w_ref, b_ref, o_ref):
    o_ref[...] = _dot(_silu(c_ref[...]), w_ref[...], precision=HIGHEST) + b_ref[...]


def ada_mod(cc, w, b):
    rows, dm = cc.shape
    n = w.shape[1]
    tn = _tile(n, 512)
    return pl.pallas_call(
        _ada_kernel, grid=(n // tn,),
        in_specs=[pl.BlockSpec((rows, dm), lambda j: (0, 0)),
                  pl.BlockSpec((dm, tn), lambda j: (0, j)),
                  pl.BlockSpec((1, tn), lambda j: (0, j))],
        out_specs=pl.BlockSpec((rows, tn), lambda j: (0, j)),
        out_shape=jax.ShapeDtypeStruct((rows, n), F32),
        compiler_params=_params("arbitrary"), name="ada_mod")(cc, w, b.reshape(1, n))


def _inproj_kernel(x_ref, sh_ref, sc_ref, g_ref, w_ref, *o_refs):
    x = x_ref[0]
    ms = jnp.mean(x * x, axis=-1, keepdims=True)
    h = (x * lax.rsqrt(ms + EPS) * g_ref[...] * (1.0 + sc_ref[0]) + sh_ref[0]).astype(BF16)
    off = 0
    for o_ref in o_refs:
        w = o_ref.shape[-1]
        o_ref[0] = _dot(h, w_ref[:, off:off + w])
        off += w


def in_projection(x, shift, scale, g, w):
    bsz, seq, dm = x.shape
    tm = _tile(seq, 512)
    widths = (W_GLA, W_CONV, W_S5, W_MLA)
    vec = pl.BlockSpec((1, 1, dm), lambda b, i: (b, 0, 0))
    return pl.pallas_call(
        _inproj_kernel, grid=(bsz, seq // tm),
        in_specs=[pl.BlockSpec((1, tm, dm), lambda b, i: (b, i, 0)), vec, vec,
                  pl.BlockSpec((1, dm), lambda b, i: (0, 0)),
                  pl.BlockSpec(w.shape, lambda b, i: (0, 0))],
        out_specs=[pl.BlockSpec((1, tm, wd), lambda b, i: (b, i, 0)) for wd in widths],
        out_shape=[jax.ShapeDtypeStruct((bsz, seq, wd), F32) for wd in widths],
        compiler_params=_params("parallel", "parallel"), name="in_projection")(x, shift, scale, g.reshape(1, dm), w)


def _rope_swap(t):
    s = t.reshape(t.shape[:-1] + (2, 2, ROPE_AXIS // 2))
    return jnp.stack([-s[..., 1, :], s[..., 0, :]], axis=-2).reshape(t.shape)


def assemble_w_in(w_in):
    edges = np.cumsum((0,) + IN_SPLITS)
    q, k, v, r, a_f, a_b, conv, s5, cq, ckv, kr = [w_in[:, edges[i]:edges[i + 1]] for i in range(len(IN_SPLITS))]
    pad = lambda t, w: jnp.pad(t, ((0, 0), (0, w - t.shape[1])))
    cols = [q, k, v, r, pad(a_f, 128), pad(a_b, 128), conv, s5, cq, ckv, pad(jnp.concatenate([kr, _rope_swap(kr)], 1), 128)]
    return jnp.concatenate(cols, axis=1).astype(BF16)


def _gla_kernel(z_ref, zc_ref, wa_ref, ba_ref, g_ref, o_ref, oc_ref, of_scr, ob_scr, stf_scr, stb_scr,
                *, seq, ctx_len):
    c = GLA_CHUNK
    nh = GLA_HEADS
    kw = nh * GLA_DK
    vw = nh * GLA_DV
    iota = lambda shape, d: lax.broadcasted_iota(jnp.int32, shape, d)
    tri = [(iota((c, c), 1) <= iota((c, c), 0)).astype(F32).astype(BF16),
           (iota((c, c), 1) >= iota((c, c), 0)).astype(F32).astype(BF16)]
    r4 = iota((nh * c, c), 0) & (c - 1)
    c4 = iota((nh * c, c), 1)
    causal = [c4 <= r4, c4 >= r4]
    k_of = lambda t: t >> int(math.log2(GLA_DK))
    v_of = lambda t: t >> int(math.log2(GLA_DV))
    k_head = [(k_of(iota((1, kw), 1)) == h).astype(F32) for h in range(nh)]
    v_head = [(v_of(iota((1, vw), 1)) == h).astype(F32) for h in range(nh)]
    st_mask = (v_of(iota((vw, kw), 0)) == k_of(iota((vw, kw), 1))).astype(F32)
    head_mean = ((v_of(iota((vw, vw), 0)) == v_of(iota((vw, vw), 1))).astype(F32) * (1.0 / GLA_DV)).astype(BF16)

    def chunk(ref, s, d, st):
        blk = ref[0, pl.ds(s, c), :]
        q = blk[:, 0:kw] * (GLA_DK ** -0.5)
        k = blk[:, kw:2 * kw]
        vb = blk[:, 2 * kw:2 * kw + vw].astype(BF16)
        a0 = 2 * kw + 2 * vw + 128 * d
        zl = _dot(blk[:, a0:a0 + 128].astype(BF16), wa_ref[d]) + ba_ref[d]
        la = (jnp.minimum(zl, 0.0) - jnp.log1p(jnp.exp(-jnp.abs(zl)))) / GLA_TAU
        b = _dot_exact_lhs(tri[d], la)
        b_end = b[c - 1:c, :] if d == 0 else b[0:1, :]
        q_dec = q * jnp.exp(b)
        k_inv = (k * jnp.exp(-b)).astype(BF16)
        k_tail = (k * jnp.exp(b_end - b)).astype(BF16)
        qs = jnp.concatenate([q_dec * k_head[h] for h in range(nh)], axis=0).astype(BF16)
        att = lax.dot_general(qs, k_inv, NT_DIMS, preferred_element_type=F32)
        att = jnp.where(causal[d], att, 0.0).astype(BF16)
        o_all = _dot(att, vb)
        o = o_all[0:c] * v_head[0]
        for h in range(1, nh):
            o = o + o_all[h * c:(h + 1) * c] * v_head[h]
        o = o + lax.dot_general(q_dec.astype(BF16), st.astype(BF16), NT_DIMS, preferred_element_type=F32)
        upd = lax.dot_general(vb, k_tail, TN_DIMS, preferred_element_type=F32)
        return o, st * jnp.exp(b_end) + upd * st_mask

    def scan(ref, n_chunks, base):
        unroll = GLA_UNROLL if n_chunks % GLA_UNROLL == 0 else 1

        @pl.loop(0, n_chunks // unroll)
        def _(it):
            st_f, st_b = stf_scr[...], stb_scr[...]
            for u in range(unroll):
                n = it * unroll + u
                sf = pl.multiple_of(n * c, c)
                sb = pl.multiple_of((n_chunks - 1 - n) * c, c)
                o_f, st_f = chunk(ref, sf, 0, st_f)
                o_b, st_b = chunk(ref, sb, 1, st_b)
                of_scr[pl.ds(pl.multiple_of(base + sf, c), c), :] = o_f
                ob_scr[pl.ds(pl.multiple_of(base + sb, c), c), :] = o_b
            stf_scr[...] = st_f
            stb_scr[...] = st_b

    def finish(ref, out_ref, n_rows, base):
        rows = GLA_FINISH_ROWS if n_rows % GLA_FINISH_ROWS == 0 else c

        @pl.loop(0, n_rows // rows)
        def _(t):
            s = pl.multiple_of(t * rows, rows)
            at = pl.ds(pl.multiple_of(base + s, rows), rows)
            o = of_scr[at, :] + ob_scr[at, :]
            ms = _dot_exact_rhs(o * o, head_mean)
            gate = ref[0, pl.ds(s, rows), 2 * kw + vw:2 * kw + 2 * vw]
            out_ref[0, pl.ds(s, rows), :] = o * lax.rsqrt(ms + EPS) * g_ref[...] * _silu(gate)

    stf_scr[...] = jnp.zeros_like(stf_scr)
    stb_scr[...] = jnp.zeros_like(stb_scr)
    scan(zc_ref, ctx_len // c, 0)
    scan(z_ref, seq // c, ctx_len)
    finish(zc_ref, oc_ref, ctx_len, 0)
    finish(z_ref, o_ref, seq, ctx_len)


def gla_mixer(zg, zgc, w_a2, b_a, norm_g):
    bsz, seq, wd = zg.shape
    ctx_len = zgc.shape[1]
    vw = GLA_HEADS * GLA_DV
    kw = GLA_HEADS * GLA_DK
    wa = jnp.pad(w_a2, ((0, 0), (0, 128 - GLA_LOWRANK), (0, 0))).astype(BF16)
    full = lambda shape: pl.BlockSpec(shape, lambda b: (0,) * len(shape))
    return pl.pallas_call(
        functools.partial(_gla_kernel, seq=seq, ctx_len=ctx_len), grid=(bsz,),
        in_specs=[pl.BlockSpec((1, seq, wd), lambda b: (b, 0, 0)),
                  pl.BlockSpec((1, ctx_len, wd), lambda b: (b, 0, 0)),
                  full((2, 128, kw)), full((2, 1, kw)), full((1, vw))],
        out_specs=[pl.BlockSpec((1, seq, vw), lambda b: (b, 0, 0)),
                   pl.BlockSpec((1, ctx_len, vw), lambda b: (b, 0, 0))],
        out_shape=[jax.ShapeDtypeStruct((bsz, seq, vw), F32), jax.ShapeDtypeStruct((bsz, ctx_len, vw), F32)],
        scratch_shapes=[pltpu.VMEM((seq + ctx_len, vw), F32), pltpu.VMEM((seq + ctx_len, vw), F32),
                        pltpu.VMEM((vw, kw), F32), pltpu.VMEM((vw, kw), F32)],
        compiler_params=_params("parallel"), name="gla_mixer")(
            zg, zgc, wa, b_a.reshape(2, 1, kw), norm_g.reshape(1, vw))


def _conv_kernel(u_ref, dw_ref, dwb_ref, lng_ref, lnb_ref, pw_ref, pwb_ref, o_ref, h_scr, *, seq, rows):
    halo = CONV_HALO
    h_scr[0:halo, :] = jnp.zeros((halo, CONV_C), F32)
    h_scr[halo + seq:2 * halo + seq, :] = jnp.zeros((halo, CONV_C), F32)

    @pl.loop(0, seq // rows)
    def _(t):
        s = pl.multiple_of(t * rows, rows)
        u = u_ref[0, pl.ds(s, rows), :]
        h_scr[pl.ds(pl.multiple_of(halo + s, 8), rows), :] = u[:, :CONV_C] * jax.nn.sigmoid(u[:, CONV_C:])

    @pl.loop(0, seq // rows)
    def _(t):
        s = pl.multiple_of(t * rows, rows)
        win = h_scr[pl.ds(s, rows + 2 * halo), :]
        first = halo - CONV_K // 2
        acc = jnp.broadcast_to(dwb_ref[...], (rows, CONV_C))
        for r in range(8):
            taps = [k for k in range(CONV_K) if (first + k) % 8 == r]
            shifted = win[r:r + rows + 8 * ((first + taps[-1]) // 8)]
            for k in taps:
                at = 8 * ((first + k) // 8)
                acc = acc + shifted[at:at + rows] * dw_ref[k:k + 1, :]
        mu = jnp.mean(acc, axis=-1, keepdims=True)
        var = jnp.mean(jnp.square(acc - mu), axis=-1, keepdims=True)
        y = _silu((acc - mu) * lax.rsqrt(var + EPS) * lng_ref[...] + lnb_ref[...])
        o_ref[0, pl.ds(s, rows), :] = _dot(y.astype(BF16), pw_ref[...]) + pwb_ref[...]


def conformer_conv(u, dw_w, dw_b, ln_g, ln_b, pw_w, pw_b):
    bsz, seq, _ = u.shape
    rows = _tile(seq, 128)
    full = lambda shape: pl.BlockSpec(shape, lambda b: (0,) * len(shape))
    row = lambda t: t.reshape(1, CONV_C)
    return pl.pallas_call(
        functools.partial(_conv_kernel, seq=seq, rows=rows), grid=(bsz,),
        in_specs=[pl.BlockSpec((1, seq, 2 * CONV_C), lambda b: (b, 0, 0)),
                  full((CONV_K + 1, CONV_C)), full((1, CONV_C)), full((1, CONV_C)), full((1, CONV_C)),
                  full((CONV_C, CONV_C)), full((1, CONV_C))],
        out_specs=pl.BlockSpec((1, seq, CONV_C), lambda b: (b, 0, 0)),
        out_shape=jax.ShapeDtypeStruct((bsz, seq, CONV_C), F32),
        scratch_shapes=[pltpu.VMEM((seq + 2 * CONV_HALO, CONV_C), F32)],
        compiler_params=_params("parallel"), name="conformer_conv")(
            u, jnp.pad(dw_w, ((0, 1), (0, 0))), row(dw_b), row(ln_g), row(ln_b), pw_w.astype(BF16), row(pw_b))


def s5_matrices(lam_re, lam_im, log_dt, b_re, b_im, c_re, c_im):
    dt = jnp.exp(log_dt)[:, None]
    mag = jnp.exp(lam_re * dt)
    a_re, a_im = mag * jnp.cos(lam_im * dt), mag * jnp.sin(lam_im * dt)
    den = lam_re * lam_re + lam_im * lam_im
    f_re = ((a_re - 1.0) * lam_re + a_im * lam_im) / den
    f_im = (a_im * lam_re - (a_re - 1.0) * lam_im) / den
    bb_re = f_re[..., None] * b_re - f_im[..., None] * b_im
    bb_im = f_re[..., None] * b_im + f_im[..., None] * b_re
    eye = jnp.eye(S5_NG, dtype=F32)
    blk_b = lambda t: jnp.einsum("gph,gk->ghkp", t, eye).reshape(S5_C, S5_STATE)
    blk_c = lambda t: jnp.einsum("ghp,gk->gpkh", t, eye).reshape(S5_STATE, S5_C)
    a = jnp.stack([a_re.reshape(S5_STATE), a_im.reshape(S5_STATE)])
    b_mat = jnp.concatenate([blk_b(bb_re), blk_b(bb_im)], axis=1).astype(BF16)
    c_mat = jnp.concatenate([blk_c(c_re), -blk_c(c_im)], axis=0).astype(BF16)
    return a, b_mat, c_mat


def _s5_scan_kernel(u_ref, a_ref, b_ref, c_ref, h0_ref, y_ref, hl_ref, hs_scr, st_scr, *, steps, bsz, reverse):
    i = pl.program_id(0)
    ns = S5_STATE

    @pl.when(i == 0)
    def _():
        st_scr[...] = h0_ref[...]

    u = u_ref[...].reshape(steps * bsz, S5_C).astype(BF16)
    hs_scr[...] = _dot(u, b_ref[...])
    for s0 in range(0, ns, S5_STRIP):
        a_re = a_ref[0:1, s0:s0 + S5_STRIP]
        a_im = a_ref[1:2, s0:s0 + S5_STRIP]

        def step(j, carry):
            h_re, h_im = carry
            t = (steps - 1 - j) if reverse else j
            r0 = pl.multiple_of(t * bsz, bsz)
            n_re = a_re * h_re - a_im * h_im + hs_scr[pl.ds(r0, bsz), s0:s0 + S5_STRIP]
            n_im = a_re * h_im + a_im * h_re + hs_scr[pl.ds(r0, bsz), ns + s0:ns + s0 + S5_STRIP]
            hs_scr[pl.ds(r0, bsz), s0:s0 + S5_STRIP] = n_re
            hs_scr[pl.ds(r0, bsz), ns + s0:ns + s0 + S5_STRIP] = n_im
            return n_re, n_im

        h_re, h_im = lax.fori_loop(0, steps, step, (st_scr[:, s0:s0 + S5_STRIP], st_scr[:, ns + s0:ns + s0 + S5_STRIP]))
        st_scr[:, s0:s0 + S5_STRIP] = h_re
        st_scr[:, ns + s0:ns + s0 + S5_STRIP] = h_im
    y_ref[...] = _dot(hs_scr[...].astype(BF16), c_ref[...]).reshape(steps, bsz, S5_C)
    hl_ref[...] = st_scr[...]


def s5_scan(u_tm, mats, h0, reverse):
    a, b_mat, c_mat = mats
    seq, bsz, _ = u_tm.shape
    steps = _tile(seq, 32)
    n = seq // steps
    order = (lambda i: (n - 1 - i, 0, 0)) if reverse else (lambda i: (i, 0, 0))
    full = lambda shape: pl.BlockSpec(shape, lambda i: (0,) * len(shape))
    return pl.pallas_call(
        functools.partial(_s5_scan_kernel, steps=steps, bsz=bsz, reverse=reverse), grid=(n,),
        in_specs=[pl.BlockSpec((steps, bsz, S5_C), order), full(a.shape), full(b_mat.shape), full(c_mat.shape),
                  full(h0.shape)],
        out_specs=[pl.BlockSpec((steps, bsz, S5_C), order), full(h0.shape)],
        out_shape=[jax.ShapeDtypeStruct((seq, bsz, S5_C), F32), jax.ShapeDtypeStruct(h0.shape, F32)],
        scratch_shapes=[pltpu.VMEM((steps * bsz, 2 * S5_STATE), F32), pltpu.VMEM(h0.shape, F32)],
        compiler_params=_params("arbitrary"), name="s5_scan")(u_tm, a, b_mat, c_mat, h0)


def _s5_out_kernel(u_ref, yf_ref, yb_ref, d_ref, w_ref, b_ref, o_ref):
    y = d_ref[...] * u_ref[...] + yf_ref[...] + yb_ref[...]
    z = _dot(y.astype(BF16), w_ref[...]) + b_ref[...]
    o_ref[...] = z[:, :S5_C] * jax.nn.sigmoid(z[:, S5_C:])


def s5_output(u, y_f, y_b, d_skip, glu_w, glu_b):
    n = u.shape[0]
    tm = _tile(n, 1024)
    rows = pl.BlockSpec((tm, S5_C), lambda i: (i, 0))
    full = lambda shape: pl.BlockSpec(shape, lambda i: (0,) * len(shape))
    return pl.pallas_call(
        _s5_out_kernel, grid=(n // tm,),
        in_specs=[rows, rows, rows, full((1, S5_C)), full((S5_C, 2 * S5_C)), full((1, 2 * S5_C))],
        out_specs=rows, out_shape=jax.ShapeDtypeStruct((n, S5_C), F32),
        compiler_params=_params("parallel"), name="s5_output")(
            u, y_f, y_b, d_skip.reshape(1, S5_C), glu_w.astype(BF16), glu_b.reshape(1, 2 * S5_C))


def s5_mixer(zs, zsc, lam_re, lam_im, log_dt, b_re, b_im, c_re, c_im, d_skip, glu_w, glu_b):
    bsz, seq, _ = zs.shape
    ctx_len = zsc.shape[1]
    u = jnp.transpose(zs, (1, 0, 2))
    uc = jnp.transpose(zsc, (1, 0, 2))
    zeros = jnp.zeros((bsz, 2 * S5_STATE), F32)
    ys, ycs = [], []
    for d in range(2):
        mats = s5_matrices(lam_re[d], lam_im[d], log_dt[d], b_re[d], b_im[d], c_re[d], c_im[d])
        yc, hc = s5_scan(uc, mats, zeros, reverse=d == 1)
        y, _ = s5_scan(u, mats, hc, reverse=d == 1)
        ys.append(y)
        ycs.append(yc)
    flat = lambda t: t.reshape(-1, S5_C)
    o = s5_output(flat(u), flat(ys[0]), flat(ys[1]), d_skip, glu_w, glu_b).reshape(seq, bsz, S5_C)
    oc = s5_output(flat(uc), flat(ycs[0]), flat(ycs[1]), d_skip, glu_w, glu_b).reshape(ctx_len, bsz, S5_C)
    return jnp.transpose(o, (1, 0, 2)), jnp.transpose(oc, (1, 0, 2))


def rope_tables(seq_len, rotate):
    if not rotate:
        return jnp.ones((seq_len, MLA_ROPE), F32), jnp.zeros((seq_len, MLA_ROPE), F32)
    rows = seq_len // GRID_W
    row = jnp.broadcast_to(jnp.arange(rows, dtype=F32)[:, None], (rows, GRID_W)).reshape(seq_len)
    col = jnp.broadcast_to(jnp.arange(GRID_W, dtype=F32)[None, :], (rows, GRID_W)).reshape(seq_len)
    inv_freq = ROPE_BASE ** (-jnp.arange(ROPE_AXIS // 2, dtype=F32) / (ROPE_AXIS // 2))
    ang = jnp.stack([row[:, None] * inv_freq, col[:, None] * inv_freq], axis=1)
    full = lambda t: jnp.broadcast_to(t[:, :, None, :], (seq_len, 2, 2, ROPE_AXIS // 2)).reshape(seq_len, MLA_ROPE)
    return full(jnp.cos(ang)), full(jnp.sin(ang))


def mla_tables(seq_len, rotate):
    cos, sin = rope_tables(seq_len, rotate)
    head = lambda rope, fill: jnp.concatenate(
        [jnp.full((seq_len, MLA_NOPE), fill, F32), rope, jnp.zeros((seq_len, MLA_HEAD_PAD - MLA_NOPE - MLA_ROPE), F32)], 1)
    cos_q = jnp.tile(head(cos, 1.0), (1, MLA_HEADS))
    sin_q = jnp.tile(head(sin, 0.0), (1, MLA_HEADS))
    cs_k = jnp.concatenate([cos, sin, jnp.zeros((seq_len, 128 - 2 * MLA_ROPE), F32)], 1)
    return cos_q, sin_q, cs_k


def mla_weights(w_uq, w_ukv):
    hd = MLA_NOPE + MLA_ROPE
    zq = jnp.zeros((MLA_Q_LORA, MLA_HEAD_PAD - hd), F32)
    zn = jnp.zeros((MLA_Q_LORA, MLA_NOPE), F32)
    wq, wq_sw, wk, wv = [], [], [], []
    for h in range(MLA_HEADS):
        qh = w_uq[:, h * hd:(h + 1) * hd]
        wq += [qh, zq]
        wq_sw += [zn, _rope_swap(qh[:, MLA_NOPE:]), zq]
        kvh = w_ukv[:, h * (MLA_NOPE + MLA_V):(h + 1) * (MLA_NOPE + MLA_V)]
        wk += [kvh[:, :MLA_NOPE], jnp.zeros((MLA_KV_LORA, MLA_HEAD_PAD - MLA_NOPE), F32)]
        wv += [kvh[:, MLA_NOPE:]]
    place = np.zeros((128, MLA_HEADS * MLA_HEAD_PAD), np.float32)
    for h in range(MLA_HEADS):
        for j in range(MLA_ROPE):
            place[j, h * MLA_HEAD_PAD + MLA_NOPE + j] = 1.0
            place[MLA_ROPE + j, h * MLA_HEAD_PAD + MLA_NOPE + j] = 1.0
    cat = lambda ts: jnp.concatenate(ts, axis=1).astype(BF16)
    return cat(wq), cat(wq_sw), cat(wk), cat(wv), jnp.asarray(place).astype(BF16)


def _mla_prep_kernel(z_ref, qg_ref, kg_ref, wq_ref, wqs_ref, wk_ref, wv_ref, pl_ref, cq_ref, sq_ref, csk_ref,
                     q_ref, k_ref, v_ref):
    z = z_ref[0]
    norm = lambda t, g: (t * lax.rsqrt(jnp.mean(t * t, axis=-1, keepdims=True) + EPS) * g).astype(BF16)
    cq = norm(z[:, :MLA_Q_LORA], qg_ref[...])
    ckv = norm(z[:, MLA_Q_LORA:MLA_Q_LORA + MLA_KV_LORA], kg_ref[...])
    scale = (MLA_NOPE + MLA_ROPE) ** -0.5
    q = _dot(cq, wq_ref[...]) * cq_ref[...] + _dot(cq, wqs_ref[...]) * sq_ref[...]
    q_ref[0] = (q * scale).astype(BF16)
    kr = z[:, MLA_Q_LORA + MLA_KV_LORA:] * csk_ref[...]
    k_ref[0] = (_dot(ckv, wk_ref[...]) + _dot_exact_rhs(kr, pl_ref[...])).astype(BF16)
    v_ref[0] = _dot(ckv, wv_ref[...]).astype(BF16)


def mla_prep(zm, qn_g, kvn_g, weights, tables):
    bsz, seq, wd = zm.shape
    tm = _tile(seq, 512)
    wq, wq_sw, wk, wv, place = weights
    cos_q, sin_q, cs_k = tables
    qw, vw = MLA_HEADS * MLA_HEAD_PAD, MLA_HEADS * MLA_V
    full = lambda t: pl.BlockSpec(t.shape, lambda b, i: (0,) * t.ndim)
    pos = lambda t: pl.BlockSpec((tm, t.shape[1]), lambda b, i: (i, 0))
    out = lambda w: pl.BlockSpec((1, tm, w), lambda b, i: (b, i, 0))
    qg, kg = qn_g.reshape(1, -1), kvn_g.reshape(1, -1)
    return pl.pallas_call(
        _mla_prep_kernel, grid=(bsz, seq // tm),
        in_specs=[pl.BlockSpec((1, tm, wd), lambda b, i: (b, i, 0)), full(qg), full(kg), full(wq), full(wq_sw),
                  full(wk), full(wv), full(place), pos(cos_q), pos(sin_q), pos(cs_k)],
        out_specs=[out(qw), out(qw), out(vw)],
        out_shape=[jax.ShapeDtypeStruct((bsz, seq, qw), BF16), jax.ShapeDtypeStruct((bsz, seq, qw), BF16),
                   jax.ShapeDtypeStruct((bsz, seq, vw), BF16)],
        compiler_params=_params("parallel", "parallel"), name="mla_prep")(
            zm, qg, kg, wq, wq_sw, wk, wv, place, cos_q, sin_q, cs_k)


def _mla_attn_kernel(q_ref, *refs, n_seg):
    k_refs, v_refs, o_ref = refs[:n_seg], refs[n_seg:2 * n_seg], refs[2 * n_seg]
    outs = []
    for h in range(MLA_HEADS):
        q = q_ref[0, :, h * MLA_HEAD_PAD:(h + 1) * MLA_HEAD_PAD]
        s = [lax.dot_general(q, k_ref[0, :, h * MLA_HEAD_PAD:(h + 1) * MLA_HEAD_PAD], NT_DIMS,
                             preferred_element_type=F32) for k_ref in k_refs]
        m = functools.reduce(jnp.maximum, [jnp.max(t, axis=-1, keepdims=True) for t in s])
        p = [jnp.exp(t - m) for t in s]
        den = functools.reduce(jnp.add, [jnp.sum(t, axis=-1, keepdims=True) for t in p])
        o = functools.reduce(jnp.add, [_dot(t.astype(BF16), v_ref[0, :, h * MLA_V:(h + 1) * MLA_V])
                                       for t, v_ref in zip(p, v_refs)])
        outs.append(o / den)
    o_ref[0] = jnp.concatenate(outs, axis=-1)


def mla_attention(q, ks, vs):
    bsz, seq, qw = q.shape
    tq = _tile(seq, 256)
    vw = MLA_HEADS * MLA_V
    seg = lambda t: pl.BlockSpec((1,) + t.shape[1:], lambda b, i: (b, 0, 0))
    return pl.pallas_call(
        functools.partial(_mla_attn_kernel, n_seg=len(ks)), grid=(bsz, seq // tq),
        in_specs=[pl.BlockSpec((1, tq, qw), lambda b, i: (b, i, 0))] + [seg(t) for t in ks] + [seg(t) for t in vs],
        out_specs=pl.BlockSpec((1, tq, vw), lambda b, i: (b, i, 0)),
        out_shape=jax.ShapeDtypeStruct((bsz, seq, vw), F32),
        compiler_params=_params("parallel", "parallel"), name="mla_attention")(q, *ks, *vs)


def _outproj_kernel(a_ref, b_ref, c_ref, d_ref, w_ref, x_ref, g1_ref, sh_ref, sc_ref, g_ref, rw_ref,
                    xo_ref, h_ref, lg_ref):
    mix = None
    for j, o_ref in enumerate((a_ref, b_ref, c_ref, d_ref)):
        wd = o_ref.shape[-1]
        part = _dot(o_ref[0].astype(BF16), w_ref[j * wd:(j + 1) * wd, :])
        mix = part if mix is None else mix + part
    x = x_ref[0] + g1_ref[0] * mix
    xo_ref[0] = x
    ms = jnp.mean(x * x, axis=-1, keepdims=True)
    h = x * lax.rsqrt(ms + EPS) * g_ref[...] * (1.0 + sc_ref[0]) + sh_ref[0]
    _store_row_slabs(h_ref, _pack_bf16_pairs(h))
    lg_ref[...] =lax.dot_general(rw_ref[...], h, NT_DIMS, precision=HIGHEST, preferred_element_type=F32)


def out_projection(parts, w_out, x, g1, shift, scale, g, router_w):
    bsz, seq, dm = x.shape
    tm = _tile(seq, 512)
    nt = seq // tm
    ne = router_w.shape[1]
    groups = dm // 2 // 128
    rows = lambda w: pl.BlockSpec((1, tm, w), lambda b, i: (b, i, 0))
    vec = pl.BlockSpec((1, 1, dm), lambda b, i: (b, 0, 0))
    full = lambda shape: pl.BlockSpec(shape, lambda b, i: (0,) * len(shape))
    return pl.pallas_call(
        _outproj_kernel, grid=(bsz, nt),
        in_specs=[rows(p.shape[-1]) for p in parts] + [full(w_out.shape), rows(dm), vec, vec, vec, full((1, dm)),
                                                       full((ne, dm))],
        out_specs=[rows(dm), pl.BlockSpec((tm * groups, 128), lambda b, i: (b * nt + i, 0)),
                   pl.BlockSpec((ne, tm), lambda b, i: (0, b * nt + i))],
        out_shape=[jax.ShapeDtypeStruct((bsz, seq, dm), F32), jax.ShapeDtypeStruct((bsz * seq * groups, 128), jnp.int32),
                   jax.ShapeDtypeStruct((ne, bsz * seq), F32)],
        compiler_params=_params("parallel", "parallel"), name="out_projection")(
            *parts, w_out.astype(BF16), x, g1, shift, scale, g.reshape(1, dm), router_w.T)


def _route_kernel(lg_ref, b_ref, e_ref, w_ref):
    ne, tt = lg_ref.shape
    per = ne // N_GROUPS
    neg = -jnp.inf
    scores = jax.nn.sigmoid(lg_ref[...])
    biased = scores + b_ref[...]
    v3 = biased.reshape(N_GROUPS, per, tt)
    e_in = lax.broadcasted_iota(jnp.int32, v3.shape, 1).astype(F32)
    m1 = jnp.max(v3, axis=1, keepdims=True)
    i1 = jnp.min(jnp.where(v3 == m1, e_in, float(per)), axis=1, keepdims=True)
    m2 = jnp.max(jnp.where(e_in == i1, neg, v3), axis=1, keepdims=True)
    grp = (m1 + m2).reshape(N_GROUPS, tt)

    def pick(cur, count):
        ids = lax.broadcasted_iota(jnp.int32, cur.shape, 0).astype(F32)
        marks = jnp.zeros(cur.shape, F32)
        picked = []
        for _ in range(count):
            m = jnp.max(cur, axis=0, keepdims=True)
            first = jnp.min(jnp.where(cur == m, ids, float(cur.shape[0])), axis=0, keepdims=True)
            hit = ids == first
            marks = jnp.where(hit, 1.0, marks)
            cur = jnp.where(hit, neg, cur)
            picked.append(first)
        return marks, picked

    grp_on, _ = pick(grp, TOPK_GROUPS)
    exp_on = jnp.broadcast_to(grp_on.reshape(N_GROUPS, 1, tt), v3.shape).reshape(ne, tt)
    chosen, picked = pick(jnp.where(exp_on > 0.0, biased, neg), TOP_K)
    w = scores * chosen
    gate = w / jnp.sum(w, axis=0, keepdims=True) * ROUTED_SCALE
    ids = lax.broadcasted_iota(jnp.int32, gate.shape, 0).astype(F32)
    e_ref[...] = jnp.concatenate(picked, axis=0).astype(jnp.int32)
    w_ref[...] = jnp.concatenate([jnp.sum(jnp.where(ids == p, gate, 0.0), axis=0, keepdims=True) for p in picked], axis=0)


def route(logits_t, router_b):
    ne, n = logits_t.shape
    tt = _tile(n, 1024)
    out = pl.BlockSpec((TOP_K, tt), lambda i: (0, i))
    return pl.pallas_call(
        _route_kernel, grid=(n // tt,),
        in_specs=[pl.BlockSpec((ne, tt), lambda i: (0, i)), pl.BlockSpec((ne, 1), lambda i: (0, 0))],
        out_specs=[out, out],
        out_shape=[jax.ShapeDtypeStruct((TOP_K, n), jnp.int32), jax.ShapeDtypeStruct((TOP_K, n), F32)],
        compiler_params=_params("parallel"), name="route")(logits_t, router_b.reshape(ne, 1))


def moe_plan(eid, wgt, chunk, xg, yg):
    n = eid.shape[1]
    nc, na = n // chunk, chunk * TOP_K
    ns = na // MOE_ROWS + N_EXPERTS
    key = eid.T.reshape(nc, na) * na + jnp.arange(na, dtype=jnp.int32)
    skey, sw = lax.sort((key, wgt.T.reshape(nc, na)), dimension=1, num_keys=1)
    e_sorted, tok = skey // na, (skey % na) // TOP_K
    experts = jnp.arange(N_EXPERTS, dtype=jnp.int32)
    ends = jnp.sum(e_sorted[:, None, :] <= experts[None, :, None], axis=-1, dtype=jnp.int32)
    starts = jnp.concatenate([jnp.zeros((nc, 1), jnp.int32), ends[:, :-1]], axis=1)
    pad_end = jnp.cumsum(-(-(ends - starts) // MOE_ROWS) * MOE_ROWS, axis=1)
    pad_start = jnp.concatenate([jnp.zeros((nc, 1), jnp.int32), pad_end[:, :-1]], axis=1)
    first = jnp.arange(ns, dtype=jnp.int32) * MOE_ROWS
    exp = jnp.minimum(jnp.sum(pad_end[:, None, :] <= first[None, :, None], axis=-1, dtype=jnp.int32), N_EXPERTS - 1)
    active = first[None, :] < pad_end[:, -1:]
    take = lambda t, i: jnp.take_along_axis(t, i, axis=1)
    rank = first[None, :, None] + jnp.arange(MOE_ROWS, dtype=jnp.int32) - take(pad_start, exp)[..., None]
    real = (rank < take(ends - starts, exp)[..., None]) & active[..., None]
    src = jnp.clip(take(starts, exp)[..., None] + rank, 0, na - 1).reshape(nc, ns * MOE_ROWS)
    rows = lambda t: take(t, src).reshape(nc, ns, MOE_ROWS)
    flat = lambda t: t.reshape(nc * ns, 1, MOE_ROWS)
    gather_at = flat(jnp.where(real, rows(tok), 0) * xg)
    scatter_at = flat(jnp.where(real, rows(tok), chunk) * yg)
    weight = flat(jnp.where(real, rows(sw), 0.0))
    return gather_at, scatter_at, weight, exp.reshape(nc * ns), active.astype(jnp.int32).reshape(nc * ns)


def _moe_tick(x_ref, y_ref, idx_ref, w_ref, tick, wgu_ref, wd_ref, gather_to, mm_from, mm_to, scatter_from):
    bm = MOE_ROWS
    stride = bm + 1
    xg = mm_from.shape[0] // stride
    yg = mm_to.shape[0] // stride
    for mi in range(bm):
        src = pl.multiple_of(idx_ref[0, 0, 2 * tick * bm + mi], xg)
        gather_to[pl.ds(mi, xg, stride=stride), :] = x_ref[0, pl.ds(src, xg), :]
    low, high = _unpack_bf16_pairs([mm_from[j * stride:j * stride + bm, :] for j in range(xg)])
    half = low.shape[1]
    ff = wd_ref.shape[1]
    up = _dot(low, wgu_ref[0, :half, :]) + _dot(high, wgu_ref[0, half:, :])
    act = (_silu(up[:, :ff]) * up[:, ff:]).astype(BF16)
    out = _dot(act, wd_ref[0])
    for j in range(yg):
        mm_to[j * stride:j * stride + bm, :] = out[:, j * 128:(j + 1) * 128]
    for base in range(0, bm, MOE_BATCH):
        at = [pl.multiple_of(idx_ref[0, 0, (2 * tick + 1) * bm + mi], yg) for mi in range(base, base + MOE_BATCH)]
        new = [y_ref[0, pl.ds(i, yg), :] + w_ref[0, 0, tick * bm + mi] * scatter_from[pl.ds(mi, yg, stride=stride), :]
               for i, mi in zip(at, range(base, base + MOE_BATCH))]
        for i, v in zip(at, new):
            y_ref[0, pl.ds(i, yg), :] = v


def _moe_routed_kernel(exp_ref, act_ref, x_ref, idx_ref, w_ref, wgua_ref, wda_ref, wgub_ref, wdb_ref,
                       y_ref, xt0, xt1, ot0, ot1, *, ns):
    c, g = pl.program_id(0), pl.program_id(1)

    @pl.when(g == 0)
    def _():
        y_ref[...] = jnp.zeros_like(y_ref)

    @pl.when((c == 0) & (g == 0))
    def _():
        xt1[...] = jnp.zeros_like(xt1)
        ot0[...] = jnp.zeros_like(ot0)
        ot1[...] = jnp.zeros_like(ot1)

    def live(step):
        return (step >= 0) & (step < ns) & (act_ref[c * ns + jnp.clip(step, 0, ns - 1)] != 0)

    t = 2 * g

    @pl.when(live(t - 2) | live(t - 1) | live(t) | live(t + 1))
    def _():
        _moe_tick(x_ref, y_ref, idx_ref, w_ref, 0, wgua_ref, wda_ref,
                  gather_to=xt0, mm_from=xt1, mm_to=ot1, scatter_from=ot0)
        _moe_tick(x_ref, y_ref, idx_ref, w_ref, 1, wgub_ref, wdb_ref,
                  gather_to=xt1, mm_from=xt0, mm_to=ot0, scatter_from=ot1)


def moe_routed(slabs, eid, wgt, wgu, wd):
    n = eid.shape[1]
    xg = slabs.shape[0] // n
    yg = 2 * xg
    dm, ff = wd.shape[2], wd.shape[1]
    chunk = _tile(n, MOE_CHUNK)
    nc = n // chunk
    ns = chunk * TOP_K // MOE_ROWS + N_EXPERTS
    assert ns % 2 == 0
    steps = (ns + 2) // 2
    last = ns - 1
    gather_at, scatter_at, weight, exp, active = moe_plan(eid, wgt, chunk, xg, yg)
    tick = jnp.arange(steps, dtype=jnp.int32)[:, None] * 2 + jnp.arange(2, dtype=jnp.int32)[None, :]
    gathered = jnp.minimum(tick, last)
    scattered = jnp.where(tick >= 2, tick - 2, last)
    per_step = lambda t, at: jnp.take(t.reshape(nc, ns, MOE_ROWS), at, axis=1)
    idx = jnp.concatenate([per_step(gather_at, gathered[:, 0]), per_step(scatter_at, scattered[:, 0]),
                           per_step(gather_at, gathered[:, 1]), per_step(scatter_at, scattered[:, 1])], axis=-1)
    wts = jnp.concatenate([per_step(weight, scattered[:, 0]), per_step(weight, scattered[:, 1])], axis=-1)
    smem = lambda width: pl.BlockSpec((1, 1, width), lambda c, g, e, a: (c * steps + g, 0, 0), memory_space=pltpu.SMEM)
    computed = lambda tk: lambda g: jnp.clip(2 * g + tk - 1, 0, last)
    expert = lambda shape, step: pl.BlockSpec((1,) + shape, lambda c, g, e, a: (e[c * ns + step(g)], 0, 0))
    once = pl.Buffered(1)
    stage = lambda groups, dtype: pltpu.VMEM((-(-groups * (MOE_ROWS + 1) // 8) * 8, 128), dtype)
    return pl.pallas_call(
        functools.partial(_moe_routed_kernel, ns=ns),
        grid_spec=pltpu.PrefetchScalarGridSpec(
            num_scalar_prefetch=2, grid=(nc, steps),
            in_specs=[pl.BlockSpec((1, chunk * xg, 128), lambda c, g, e, a: (c, 0, 0), pipeline_mode=once),
                      smem(4 * MOE_ROWS), smem(2 * MOE_ROWS),
                      expert((dm, 2 * ff), computed(0)), expert((ff, dm), computed(0)),
                      expert((dm, 2 * ff), computed(1)), expert((ff, dm), computed(1))],
            out_specs=pl.BlockSpec((1, chunk * yg + 8, 128), lambda c, g, e, a: (c, 0, 0), pipeline_mode=once),
            scratch_shapes=[stage(xg, jnp.int32), stage(xg, jnp.int32), stage(yg, F32), stage(yg, F32)]),
        out_shape=jax.ShapeDtypeStruct((nc, chunk * yg + 8, 128), F32),
        compiler_params=_params("arbitrary", "arbitrary"), name="moe_routed")(
            exp, active, slabs.reshape(nc, chunk * xg, 128), idx.reshape(nc * steps, 1, 4 * MOE_ROWS),
            wts.reshape(nc * steps, 1, 2 * MOE_ROWS), wgu, wd, wgu, wd)


def _moe_finish_kernel(y_ref, h_ref, x_ref, g2_ref, wg_ref, wu_ref, wd_ref, fg_ref, o_ref, *, final_norm):
    tm = x_ref.shape[0]
    low, high = _unpack_bf16_pairs(_load_row_slabs(h_ref, tm, h_ref.shape[0] // tm))
    routed = jnp.concatenate(_load_row_slabs(y_ref, tm, y_ref.shape[1] // tm, lead=(0,)), axis=1)
    half = low.shape[1]
    up = lambda w_ref: _dot(low, w_ref[:half, :]) + _dot(high, w_ref[half:, :])
    shared = _dot((_silu(up(wg_ref)) * up(wu_ref)).astype(BF16), wd_ref[...])
    x = x_ref[...] + g2_ref[0] * (routed + shared)
    if final_norm:
        x = x * lax.rsqrt(jnp.mean(x * x, axis=-1, keepdims=True) + EPS) * fg_ref[...]
    o_ref[...] = x


def moe_finish(y, slabs, x, g2, sg, su, sd, final_g, final_norm):
    bsz, seq, dm = x.shape
    n = bsz * seq
    tm = _tile(seq, 512)
    nt = seq // tm
    xg, yg = slabs.shape[0] // n, dm // 128
    per_chunk = (y.shape[1] - 8) // yg // tm
    rows = lambda w: pl.BlockSpec((tm, w), lambda i: (i, 0))
    full = lambda t: pl.BlockSpec(t.shape, lambda i: (0,) * t.ndim)
    sg, su, sd, fg = sg.astype(BF16), su.astype(BF16), sd.astype(BF16), final_g.reshape(1, dm)
    out = pl.pallas_call(
        functools.partial(_moe_finish_kernel, final_norm=final_norm), grid=(n // tm,),
        in_specs=[pl.BlockSpec((1, tm * yg, 128), lambda i: (i // per_chunk, i % per_chunk, 0)),
                  pl.BlockSpec((tm * xg, 128), lambda i: (i, 0)), rows(dm),
                  pl.BlockSpec((1, 1, dm), lambda i: (i // nt, 0, 0)), full(sg), full(su), full(sd), full(fg)],
        out_specs=rows(dm), out_shape=jax.ShapeDtypeStruct((n, dm), F32),
        compiler_params=_params("parallel"), name="moe_finish")(y, slabs, x.reshape(n, dm), g2, sg, su, sd, fg)
    return out.reshape(bsz, seq, dm)


def kernel(x, c, ctx, c_ctx, ada_w, ada_b, norm1_g, norm2_g, w_in, w_out, gla_w_a2, gla_b_a, gla_norm_g,
           conv_dw_w, conv_dw_b, conv_ln_g, conv_ln_b, conv_pw_w, conv_pw_b,
           s5_lam_re, s5_lam_im, s5_log_dt, s5_b_re, s5_b_im, s5_c_re, s5_c_im, s5_d, s5_glu_w, s5_glu_b,
           mla_qn_g, mla_kvn_g, mla_w_uq, mla_w_ukv,
           moe_router_w, moe_router_b, moe_w_gate, moe_w_up, moe_w_down,
           shared_w_gate, shared_w_up, shared_w_down, final_g):
    bsz, seq, dm = x.shape
    ctx_len = ctx.shape[1]
    depth = ada_w.shape[0]
    cc = jnp.concatenate([c, c_ctx[None], jnp.zeros((-(bsz + 1) % 8, dm), F32)], axis=0)
    tables = mla_tables(seq, True)
    tables_c = mla_tables(ctx_len, False)
    xc = ctx
    for l in range(depth):
        last = l == depth - 1
        mod = ada_mod(cc, ada_w[l], ada_b[l])
        lat = [t.reshape(bsz, 1, dm) for t in jnp.split(mod[:bsz], 6, axis=-1)]
        con = [jnp.broadcast_to(t.reshape(1, 1, dm), (bsz, 1, dm)) for t in jnp.split(mod[bsz], 6)]
        sh1, sc1, g1, sh2, sc2, g2 = lat
        sh1c, sc1c, g1c, sh2c, sc2c, g2c = con
        w_in_l = assemble_w_in(w_in[l])
        zg, zv, zs, zm = in_projection(x, sh1, sc1, norm1_g[l], w_in_l)
        zgc, zvc, zsc, zmc = in_projection(xc, sh1c, sc1c, norm1_g[l], w_in_l)
        conv_p = (conv_dw_w[l], conv_dw_b[l], conv_ln_g[l], conv_ln_b[l], conv_pw_w[l], conv_pw_b[l])
        o_gla, oc_gla = gla_mixer(zg, zgc, gla_w_a2[l], gla_b_a[l], gla_norm_g[l])
        o_conv = conformer_conv(zv, *conv_p)
        o_s5, oc_s5 = s5_mixer(zs, zsc, s5_lam_re[l], s5_lam_im[l], s5_log_dt[l], s5_b_re[l], s5_b_im[l],
                               s5_c_re[l], s5_c_im[l], s5_d[l], s5_glu_w[l], s5_glu_b[l])
        mla_w = mla_weights(mla_w_uq[l], mla_w_ukv[l])
        q, k, v = mla_prep(zm, mla_qn_g[l], mla_kvn_g[l], mla_w, tables)
        qc, kc, vc = mla_prep(zmc, mla_qn_g[l], mla_kvn_g[l], mla_w, tables_c)
        o_mla = mla_attention(q, [k, kc], [v, vc])
        experts = (jnp.concatenate([moe_w_gate[l], moe_w_up[l]], axis=2).astype(BF16), moe_w_down[l].astype(BF16))
        shared = (shared_w_gate[l], shared_w_up[l], shared_w_down[l])

        def ffn(parts, x_in, gate1, shift, scale, gate2, final_norm):
            x_mid, slabs, logits = out_projection(parts, w_out[l], x_in, gate1, shift, scale, norm2_g[l], moe_router_w[l])
            y = moe_routed(slabs, *route(logits, moe_router_b[l]), *experts)
            return moe_finish(y, slabs, x_mid, gate2, *shared, final_g, final_norm)

        if not last:
            oc_conv = conformer_conv(zvc, *conv_p)
            oc_mla = mla_attention(qc, [kc], [vc])
            xc = ffn([oc_gla, oc_conv, oc_s5, oc_mla], xc, g1c, sh2c, sc2c, g2c, False)
        x = ffn([o_gla, o_conv, o_s5, o_mla], x, g1, sh2, sc2, g2, last)
    return x
```

```python
import functools
import math

import numpy as np
import jax
import jax.numpy as jnp
from jax import lax
from jax.experimental import pallas as pl
from jax.experimental.pallas import tpu as pltpu

F32 = jnp.float32
BF16 = jnp.bfloat16
HIGHEST = lax.Precision.HIGHEST
EPS = 1e-6

GRID_W = 64

GLA_HEADS = 4
GLA_DK = 32
GLA_DV = 64
GLA_LOWRANK = 16
GLA_TAU = 16.0
GLA_CHUNK = 64
GLA_UNROLL = 2
GLA_FINISH_ROWS = 256

CONV_C = 256
CONV_K = 31
CONV_HALO = 16

S5_C = 256
S5_GROUP = 16
S5_NG = 16
S5_P = 64
S5_STATE = S5_NG * S5_P
S5_STRIP = 256

MLA_HEADS = 4
MLA_NOPE = 64
MLA_ROPE = 32
MLA_V = 64
MLA_Q_LORA = 256
MLA_KV_LORA = 128
MLA_HEAD_PAD = 128
ROPE_AXIS = MLA_ROPE // 2
ROPE_BASE = 10000.0

N_EXPERTS = 64
TOP_K = 8
N_GROUPS = 8
TOPK_GROUPS = 4
ROUTED_SCALE = 2.5
MOE_CHUNK = 4096
MOE_ROWS = 256
MOE_BATCH = 16

IN_SPLITS = (128, 128, 256, 256, 16, 16, 512, 256, 256, 128, 32)
W_GLA, W_CONV, W_S5, W_MLA = 1024, 512, 256, 512

VMEM_LIMIT = 48 * 1024 * 1024

NT_DIMS = (((1,), (1,)), ((), ()))
TN_DIMS = (((0,), (0,)), ((), ()))


def _params(*sem, vmem=VMEM_LIMIT):
    return pltpu.CompilerParams(dimension_semantics=sem, vmem_limit_bytes=vmem)


def _silu(v):
    return v * jax.nn.sigmoid(v)


def _dot(a, b, **kw):
    return jnp.dot(a, b, preferred_element_type=F32, **kw)


def _split_bf16(x):
    hi = x.astype(BF16)
    rest = x - hi.astype(F32)
    mid = rest.astype(BF16)
    return hi, mid, (rest - mid.astype(F32)).astype(BF16)


def _dot_exact_lhs(m, x):
    n = x.shape[1]
    prod = _dot(m, jnp.concatenate(_split_bf16(x), axis=1))
    return prod[:, :n] + prod[:, n:2 * n] + prod[:, 2 * n:]


def _dot_exact_rhs(x, m):
    n = x.shape[0]
    prod = _dot(jnp.concatenate(_split_bf16(x), axis=0), m)
    return prod[:n] + prod[n:2 * n] + prod[2 * n:]


def _tile(n, pref):
    return pref if n % pref == 0 else n


def _pack_bf16_pairs(h):
    bits = pltpu.bitcast(h.astype(BF16).astype(F32), jnp.int32)
    w = h.shape[-1] // 2
    return lax.shift_right_logical(bits[:, :w], 16) | bits[:, w:]


def _store_row_slabs(ref, value):
    rows, g = value.shape[0], value.shape[1] // 128
    for j in range(g):
        ref[pl.ds(j, rows, stride=g), :] = value[:, j * 128:(j + 1) * 128]


def _load_row_slabs(ref, rows, g, lead=()):
    return [ref[lead + (pl.ds(j, rows, stride=g), slice(None))] for j in range(g)]


def _unpack_bf16_pairs(words):
    low = jnp.concatenate([pltpu.bitcast(v << 16, F32) for v in words], axis=1)
    high = jnp.concatenate([pltpu.bitcast(v & -65536, F32) for v in words], axis=1)
    return low.astype(BF16), high.astype(BF16)


def _ada_kernel(c_ref, w_ref, b_ref, o_ref):
    o_ref[...] = _dot(_silu(c_ref[...]), w_ref[...], precision=HIGHEST) + b_ref[...]


def ada_mod(cc, w, b):
    rows, dm = cc.shape
    n = w.shape[1]
    tn = _tile(n, 512)
    return pl.pallas_call(
        _ada_kernel, grid=(n // tn,),
        in_specs=[pl.BlockSpec((rows, dm), lambda j: (0, 0)),
                  pl.BlockSpec((dm, tn), lambda j: (0, j)),
                  pl.BlockSpec((1, tn), lambda j: (0, j))],
        out_specs=pl.BlockSpec((rows, tn), lambda j: (0, j)),
        out_shape=jax.ShapeDtypeStruct((rows, n), F32),
        compiler_params=_params("arbitrary"), name="ada_mod")(cc, w, b.reshape(1, n))


def _inproj_kernel(x_ref, sh_ref, sc_ref, g_ref, w_ref, *o_refs):
    x = x_ref[0]
    ms = jnp.mean(x * x, axis=-1, keepdims=True)
    h = (x * lax.rsqrt(ms + EPS) * g_ref[...] * (1.0 + sc_ref[0]) + sh_ref[0]).astype(BF16)
    off = 0
    for o_ref in o_refs:
        w = o_ref.shape[-1]
        o_ref[0] = _dot(h, w_ref[:, off:off + w])
        off += w


def in_projection(x, shift, scale, g, w):
    bsz, seq, dm = x.shape
    tm = _tile(seq, 512)
    widths = (W_GLA, W_CONV, W_S5, W_MLA)
    vec = pl.BlockSpec((1, 1, dm), lambda b, i: (b, 0, 0))
    return pl.pallas_call(
        _inproj_kernel, grid=(bsz, seq // tm),
        in_specs=[pl.BlockSpec((1, tm, dm), lambda b, i: (b, i, 0)), vec, vec,
                  pl.BlockSpec((1, dm), lambda b, i: (0, 0)),
                  pl.BlockSpec(w.shape, lambda b, i: (0, 0))],
        out_specs=[pl.BlockSpec((1, tm, wd), lambda b, i: (b, i, 0)) for wd in widths],
        out_shape=[jax.ShapeDtypeStruct((bsz, seq, wd), F32) for wd in widths],
        compiler_params=_params("parallel", "parallel"), name="in_projection")(x, shift, scale, g.reshape(1, dm), w)


def _rope_swap(t):
    s = t.reshape(t.shape[:-1] + (2, 2, ROPE_AXIS // 2))
    return jnp.stack([-s[..., 1, :], s[..., 0, :]], axis=-2).reshape(t.shape)


def assemble_w_in(w_in):
    edges = np.cumsum((0,) + IN_SPLITS)
    q, k, v, r, a_f, a_b, conv, s5, cq, ckv, kr = [w_in[:, edges[i]:edges[i + 1]] for i in range(len(IN_SPLITS))]
    pad = lambda t, w: jnp.pad(t, ((0, 0), (0, w - t.shape[1])))
    cols = [q, k, v, r, pad(a_f, 128), pad(a_b, 128), conv, s5, cq, ckv, pad(jnp.concatenate([kr, _rope_swap(kr)], 1), 128)]
    return jnp.concatenate(cols, axis=1).astype(BF16)


def _gla_kernel(z_ref, zc_ref, wa_ref, ba_ref, g_ref, o_ref, oc_ref, of_scr, ob_scr, st_scr, *, seq, ctx_len):
    c = GLA_CHUNK
    nh = GLA_HEADS
    kw = nh * GLA_DK
    vw = nh * GLA_DV
    iota = lambda shape, d: lax.broadcasted_iota(jnp.int32, shape, d)
    r2, c2 = iota((2 * c, 2 * c), 0), iota((2 * c, 2 * c), 1)
    tri = (((r2 < c) & (c2 <= r2)) | ((r2 >= c) & (c2 >= r2))).astype(F32).astype(BF16)
    ra, ca = iota((nh * 2 * c, 2 * c), 0) & (2 * c - 1), iota((nh * 2 * c, 2 * c), 1)
    causal = ((ra < c) & (ca <= ra)) | ((ra >= c) & (ca >= ra))
    fwd_rows = iota((2 * c, 1), 0) < c
    k_of = lambda t: t >> int(math.log2(GLA_DK))
    v_of = lambda t: t >> int(math.log2(GLA_DV))
    k_head = [(k_of(iota((1, kw), 1)) == h).astype(F32) for h in range(nh)]
    v_head = [(v_of(iota((1, vw), 1)) == h).astype(F32) for h in range(nh)]
    st_rows = iota((2 * vw, kw), 0)
    st_mask = (v_of(st_rows & (vw - 1)) == k_of(iota((2 * vw, kw), 1))).astype(F32)
    st_fwd = st_rows < vw
    head_mean = ((v_of(iota((vw, vw), 0)) == v_of(iota((vw, vw), 1))).astype(F32) * (1.0 / GLA_DV)).astype(BF16)

    def pair(ref, sf, sb, st):
        blk_f, blk_b = ref[0, pl.ds(sf, c), :], ref[0, pl.ds(sb, c), :]
        blk = jnp.concatenate([blk_f, blk_b], axis=0)
        q = blk[:, 0:kw] * (GLA_DK ** -0.5)
        k = blk[:, kw:2 * kw]
        vb = blk[:, 2 * kw:2 * kw + vw].astype(BF16)
        a0 = 2 * kw + 2 * vw
        a_low = jnp.concatenate([blk_f[:, a0:a0 + 128], blk_b[:, a0 + 128:a0 + 256]], axis=0).astype(BF16)
        z2 = _dot(a_low, wa_ref[...])
        zl = jnp.where(fwd_rows, z2[:, :kw] + ba_ref[0], z2[:, kw:] + ba_ref[1])
        la = (jnp.minimum(zl, 0.0) - jnp.log1p(jnp.exp(-jnp.abs(zl)))) / GLA_TAU
        b = _dot_exact_lhs(tri, la)
        b_end = jnp.where(fwd_rows, b[c - 1:c, :], b[c:c + 1, :])
        q_dec = q * jnp.exp(b)
        k_inv = (k * jnp.exp(-b)).astype(BF16)
        k_tail = (k * jnp.exp(b_end - b)).astype(BF16)
        qs = jnp.concatenate([q_dec * k_head[h] for h in range(nh)], axis=0).astype(BF16)
        att = lax.dot_general(qs, k_inv, NT_DIMS, preferred_element_type=F32)
        att = jnp.where(causal, att, 0.0).astype(BF16)
        o_all = _dot(att, vb)
        o = o_all[0:2 * c] * v_head[0]
        for h in range(1, nh):
            o = o + o_all[h * 2 * c:(h + 1) * 2 * c] * v_head[h]
        both = lax.dot_general(q_dec.astype(BF16), st.astype(BF16), NT_DIMS, preferred_element_type=F32)
        o = o + jnp.where(fwd_rows, both[:, :vw], both[:, vw:])
        v2 = jnp.concatenate([jnp.where(fwd_rows, vb, 0), jnp.where(fwd_rows, 0, vb)], axis=1)
        upd = lax.dot_general(v2, k_tail, TN_DIMS, preferred_element_type=F32)
        decay = jnp.where(st_fwd, jnp.exp(b[c - 1:c, :]), jnp.exp(b[c:c + 1, :]))
        return o, st * decay + upd * st_mask

    def scan(ref, n_chunks, base):
        unroll = GLA_UNROLL if n_chunks % GLA_UNROLL == 0 else 1

        @pl.loop(0, n_chunks // unroll)
        def _(it):
            st = st_scr[...]
            for u in range(unroll):
                n = it * unroll + u
                sf = pl.multiple_of(n * c, c)
                sb = pl.multiple_of((n_chunks - 1 - n) * c, c)
                o, st = pair(ref, sf, sb, st)
                of_scr[pl.ds(pl.multiple_of(base + sf, c), c), :] = o[:c]
                ob_scr[pl.ds(pl.multiple_of(base + sb, c), c), :] = o[c:]
            st_scr[...] = st

    def finish(ref, out_ref, n_rows, base):
        rows = GLA_FINISH_ROWS if n_rows % GLA_FINISH_ROWS == 0 else c

        @pl.loop(0, n_rows // rows)
        def _(t):
            s = pl.multiple_of(t * rows, rows)
            at = pl.ds(pl.multiple_of(base + s, rows), rows)
            o = of_scr[at, :] + ob_scr[at, :]
            ms = _dot_exact_rhs(o * o, head_mean)
            gate = ref[0, pl.ds(s, rows), 2 * kw + vw:2 * kw + 2 * vw]
            out_ref[0, pl.ds(s, rows), :] = (o * lax.rsqrt(ms + EPS) * g_ref[...] * _silu(gate)).astype(BF16)

    st_scr[...] = jnp.zeros_like(st_scr)
    scan(zc_ref, ctx_len // c, 0)
    scan(z_ref, seq // c, ctx_len)
    finish(zc_ref, oc_ref, ctx_len, 0)
    finish(z_ref, o_ref, seq, ctx_len)


def gla_mixer(zg, zgc, w_a2, b_a, norm_g):
    bsz, seq, wd = zg.shape
    ctx_len = zgc.shape[1]
    vw = GLA_HEADS * GLA_DV
    kw = GLA_HEADS * GLA_DK
    wa = jnp.concatenate(list(jnp.pad(w_a2, ((0, 0), (0, 128 - GLA_LOWRANK), (0, 0)))), axis=1).astype(BF16)
    full = lambda shape: pl.BlockSpec(shape, lambda b: (0,) * len(shape))
    return pl.pallas_call(
        functools.partial(_gla_kernel, seq=seq, ctx_len=ctx_len), grid=(bsz,),
        in_specs=[pl.BlockSpec((1, seq, wd), lambda b: (b, 0, 0)),
                  pl.BlockSpec((1, ctx_len, wd), lambda b: (b, 0, 0)),
                  full((128, 2 * kw)), full((2, 1, kw)), full((1, vw))],
        out_specs=[pl.BlockSpec((1, seq, vw), lambda b: (b, 0, 0)),
                   pl.BlockSpec((1, ctx_len, vw), lambda b: (b, 0, 0))],
        out_shape=[jax.ShapeDtypeStruct((bsz, seq, vw), BF16), jax.ShapeDtypeStruct((bsz, ctx_len, vw), BF16)],
        scratch_shapes=[pltpu.VMEM((seq + ctx_len, vw), F32), pltpu.VMEM((seq + ctx_len, vw), F32),
                        pltpu.VMEM((2 * vw, kw), F32)],
        compiler_params=_params("parallel"), name="gla_mixer")(
            zg, zgc, wa, b_a.reshape(2, 1, kw), norm_g.reshape(1, vw))


def _conv_kernel(u_ref, dw_ref, dwb_ref, lng_ref, lnb_ref, pw_ref, pwb_ref, o_ref, h_scr, *, seq, rows):
    halo = CONV_HALO
    h_scr[0:halo, :] = jnp.zeros((halo, CONV_C), F32)
    h_scr[halo + seq:2 * halo + seq, :] = jnp.zeros((halo, CONV_C), F32)

    @pl.loop(0, seq // rows)
    def _(t):
        s = pl.multiple_of(t * rows, rows)
        u = u_ref[0, pl.ds(s, rows), :]
        h_scr[pl.ds(pl.multiple_of(halo + s, 8), rows), :] = u[:, :CONV_C] * jax.nn.sigmoid(u[:, CONV_C:])

    @pl.loop(0, seq // rows)
    def _(t):
        s = pl.multiple_of(t * rows, rows)
        win = h_scr[pl.ds(s, rows + 2 * halo), :]
        first = halo - CONV_K // 2
        acc = jnp.broadcast_to(dwb_ref[...], (rows, CONV_C))
        for r in range(8):
            taps = [k for k in range(CONV_K) if (first + k) % 8 == r]
            shifted = win if r == 0 else pltpu.roll(win, win.shape[0] - r, axis=0)
            for k in taps:
                at = 8 * ((first + k) // 8)
                acc = acc + shifted[at:at + rows] * dw_ref[k:k + 1, :]
        mu = jnp.mean(acc, axis=-1, keepdims=True)
        var = jnp.mean(jnp.square(acc - mu), axis=-1, keepdims=True)
        y = _silu((acc - mu) * lax.rsqrt(var + EPS) * lng_ref[...] + lnb_ref[...])
        o_ref[0, pl.ds(s, rows), :] = (_dot(y.astype(BF16), pw_ref[...]) + pwb_ref[...]).astype(BF16)


def conformer_conv(u, dw_w, dw_b, ln_g, ln_b, pw_w, pw_b):
    bsz, seq, _ = u.shape
    rows = _tile(seq, 128)
    full = lambda shape: pl.BlockSpec(shape, lambda b: (0,) * len(shape))
    row = lambda t: t.reshape(1, CONV_C)
    return pl.pallas_call(
        functools.partial(_conv_kernel, seq=seq, rows=rows), grid=(bsz,),
        in_specs=[pl.BlockSpec((1, seq, 2 * CONV_C), lambda b: (b, 0, 0)),
                  full((CONV_K + 1, CONV_C)), full((1, CONV_C)), full((1, CONV_C)), full((1, CONV_C)),
                  full((CONV_C, CONV_C)), full((1, CONV_C))],
        out_specs=pl.BlockSpec((1, seq, CONV_C), lambda b: (b, 0, 0)),
        out_shape=jax.ShapeDtypeStruct((bsz, seq, CONV_C), BF16),
        scratch_shapes=[pltpu.VMEM((seq + 2 * CONV_HALO, CONV_C), F32)],
        compiler_params=_params("parallel"), name="conformer_conv")(
            u, jnp.pad(dw_w, ((0, 1), (0, 0))), row(dw_b), row(ln_g), row(ln_b), pw_w.astype(BF16), row(pw_b))


def s5_matrices(lam_re, lam_im, log_dt, b_re, b_im, c_re, c_im):
    dt = jnp.exp(log_dt)[:, None]
    mag = jnp.exp(lam_re * dt)
    a_re, a_im = mag * jnp.cos(lam_im * dt), mag * jnp.sin(lam_im * dt)
    den = lam_re * lam_re + lam_im * lam_im
    f_re = ((a_re - 1.0) * lam_re + a_im * lam_im) / den
    f_im = (a_im * lam_re - (a_re - 1.0) * lam_im) / den
    bb_re = f_re[..., None] * b_re - f_im[..., None] * b_im
    bb_im = f_re[..., None] * b_im + f_im[..., None] * b_re
    eye = jnp.eye(S5_NG, dtype=F32)
    blk_b = lambda t: jnp.einsum("gph,gk->ghkp", t, eye).reshape(S5_C, S5_STATE)
    blk_c = lambda t: jnp.einsum("ghp,gk->gpkh", t, eye).reshape(S5_STATE, S5_C)
    a = jnp.stack([a_re.reshape(S5_STATE), a_im.reshape(S5_STATE)])
    b_mat = jnp.concatenate([blk_b(bb_re), blk_b(bb_im)], axis=1).astype(BF16)
    c_mat = jnp.concatenate([blk_c(c_re), -blk_c(c_im)], axis=0).astype(BF16)
    return a, b_mat, c_mat


def _s5_scan_kernel(u_ref, a_ref, b_ref, c_ref, h0_ref, y_ref, hl_ref, hs_scr, st_scr, *, steps, bsz, reverse):
    i = pl.program_id(0)
    ns = S5_STATE

    @pl.when(i == 0)
    def _():
        st_scr[...] = h0_ref[...]

    u = u_ref[...].reshape(steps * bsz, S5_C).astype(BF16)
    hs_scr[...] = _dot(u, b_ref[...])
    for s0 in range(0, ns, S5_STRIP):
        a_re = a_ref[0:1, s0:s0 + S5_STRIP]
        a_im = a_ref[1:2, s0:s0 + S5_STRIP]

        def step(j, carry):
            h_re, h_im = carry
            t = (steps - 1 - j) if reverse else j
            r0 = pl.multiple_of(t * bsz, bsz)
            n_re = a_re * h_re - a_im * h_im + hs_scr[pl.ds(r0, bsz), s0:s0 + S5_STRIP]
            n_im = a_re * h_im + a_im * h_re + hs_scr[pl.ds(r0, bsz), ns + s0:ns + s0 + S5_STRIP]
            hs_scr[pl.ds(r0, bsz), s0:s0 + S5_STRIP] = n_re
            hs_scr[pl.ds(r0, bsz), ns + s0:ns + s0 + S5_STRIP] = n_im
            return n_re, n_im

        h_re, h_im = lax.fori_loop(0, steps, step, (st_scr[:, s0:s0 + S5_STRIP], st_scr[:, ns + s0:ns + s0 + S5_STRIP]))
        st_scr[:, s0:s0 + S5_STRIP] = h_re
        st_scr[:, ns + s0:ns + s0 + S5_STRIP] = h_im
    y_ref[...] = _dot(hs_scr[...].astype(BF16), c_ref[...]).reshape(steps, bsz, S5_C)
    hl_ref[...] = st_scr[...]


def s5_scan(u_tm, mats, h0, reverse):
    a, b_mat, c_mat = mats
    seq, bsz, _ = u_tm.shape
    steps = _tile(seq, 32)
    n = seq // steps
    order = (lambda i: (n - 1 - i, 0, 0)) if reverse else (lambda i: (i, 0, 0))
    full = lambda shape: pl.BlockSpec(shape, lambda i: (0,) * len(shape))
    return pl.pallas_call(
        functools.partial(_s5_scan_kernel, steps=steps, bsz=bsz, reverse=reverse), grid=(n,),
        in_specs=[pl.BlockSpec((steps, bsz, S5_C), order), full(a.shape), full(b_mat.shape), full(c_mat.shape),
                  full(h0.shape)],
        out_specs=[pl.BlockSpec((steps, bsz, S5_C), order), full(h0.shape)],
        out_shape=[jax.ShapeDtypeStruct((seq, bsz, S5_C), F32), jax.ShapeDtypeStruct(h0.shape, F32)],
        scratch_shapes=[pltpu.VMEM((steps * bsz, 2 * S5_STATE), F32), pltpu.VMEM(h0.shape, F32)],
        compiler_params=_params("arbitrary"), name="s5_scan")(u_tm, a, b_mat, c_mat, h0)


def _s5_out_kernel(u_ref, yf_ref, yb_ref, d_ref, w_ref, b_ref, o_ref):
    y = d_ref[...] * u_ref[...] + yf_ref[...] + yb_ref[...]
    z = _dot(y.astype(BF16), w_ref[...]) + b_ref[...]
    o_ref[...] = (z[:, :S5_C] * jax.nn.sigmoid(z[:, S5_C:])).astype(BF16)


def s5_output(u, y_f, y_b, d_skip, glu_w, glu_b):
    n = u.shape[0]
    tm = _tile(n, 1024)
    rows = pl.BlockSpec((tm, S5_C), lambda i: (i, 0))
    full = lambda shape: pl.BlockSpec(shape, lambda i: (0,) * len(shape))
    return pl.pallas_call(
        _s5_out_kernel, grid=(n // tm,),
        in_specs=[rows, rows, rows, full((1, S5_C)), full((S5_C, 2 * S5_C)), full((1, 2 * S5_C))],
        out_specs=rows, out_shape=jax.ShapeDtypeStruct((n, S5_C), BF16),
        compiler_params=_params("parallel"), name="s5_output")(
            u, y_f, y_b, d_skip.reshape(1, S5_C), glu_w.astype(BF16), glu_b.reshape(1, 2 * S5_C))


def s5_mixer(zs, zsc, lam_re, lam_im, log_dt, b_re, b_im, c_re, c_im, d_skip, glu_w, glu_b):
    bsz, seq, _ = zs.shape
    ctx_len = zsc.shape[1]
    u = jnp.transpose(zs, (1, 0, 2))
    uc = jnp.transpose(zsc, (1, 0, 2))
    zeros = jnp.zeros((bsz, 2 * S5_STATE), F32)
    ys, ycs = [], []
    for d in range(2):
        mats = s5_matrices(lam_re[d], lam_im[d], log_dt[d], b_re[d], b_im[d], c_re[d], c_im[d])
        yc, hc = s5_scan(uc, mats, zeros, reverse=d == 1)
        y, _ = s5_scan(u, mats, hc, reverse=d == 1)
        ys.append(y)
        ycs.append(yc)
    flat = lambda t: t.reshape(-1, S5_C)
    o = s5_output(flat(u), flat(ys[0]), flat(ys[1]), d_skip, glu_w, glu_b).reshape(seq, bsz, S5_C)
    oc = s5_output(flat(uc), flat(ycs[0]), flat(ycs[1]), d_skip, glu_w, glu_b).reshape(ctx_len, bsz, S5_C)
    return jnp.transpose(o, (1, 0, 2)), jnp.transpose(oc, (1, 0, 2))


def rope_tables(seq_len, rotate):
    if not rotate:
        return jnp.ones((seq_len, MLA_ROPE), F32), jnp.zeros((seq_len, MLA_ROPE), F32)
    rows = seq_len // GRID_W
    row = jnp.broadcast_to(jnp.arange(rows, dtype=F32)[:, None], (rows, GRID_W)).reshape(seq_len)
    col = jnp.broadcast_to(jnp.arange(GRID_W, dtype=F32)[None, :], (rows, GRID_W)).reshape(seq_len)
    inv_freq = ROPE_BASE ** (-jnp.arange(ROPE_AXIS // 2, dtype=F32) / (ROPE_AXIS // 2))
    ang = jnp.stack([row[:, None] * inv_freq, col[:, None] * inv_freq], axis=1)
    full = lambda t: jnp.broadcast_to(t[:, :, None, :], (seq_len, 2, 2, ROPE_AXIS // 2)).reshape(seq_len, MLA_ROPE)
    return full(jnp.cos(ang)), full(jnp.sin(ang))


def mla_tables(seq_len, rotate):
    cos, sin = rope_tables(seq_len, rotate)
    head = lambda rope, fill: jnp.concatenate(
        [jnp.full((seq_len, MLA_NOPE), fill, F32), rope, jnp.zeros((seq_len, MLA_HEAD_PAD - MLA_NOPE - MLA_ROPE), F32)], 1)
    cos_q = jnp.tile(head(cos, 1.0), (1, MLA_HEADS))
    sin_q = jnp.tile(head(sin, 0.0), (1, MLA_HEADS))
    cs_k = jnp.concatenate([cos, sin, jnp.zeros((seq_len, 128 - 2 * MLA_ROPE), F32)], 1)
    return cos_q, sin_q, cs_k


def mla_weights(w_uq, w_ukv):
    hd = MLA_NOPE + MLA_ROPE
    zq = jnp.zeros((MLA_Q_LORA, MLA_HEAD_PAD - hd), F32)
    zn = jnp.zeros((MLA_Q_LORA, MLA_NOPE), F32)
    wq, wq_sw, wk, wv = [], [], [], []
    for h in range(MLA_HEADS):
        qh = w_uq[:, h * hd:(h + 1) * hd]
        wq += [qh, zq]
        wq_sw += [zn, _rope_swap(qh[:, MLA_NOPE:]), zq]
        kvh = w_ukv[:, h * (MLA_NOPE + MLA_V):(h + 1) * (MLA_NOPE + MLA_V)]
        wk += [kvh[:, :MLA_NOPE], jnp.zeros((MLA_KV_LORA, MLA_HEAD_PAD - MLA_NOPE), F32)]
        wv += [kvh[:, MLA_NOPE:]]
    place = np.zeros((128, MLA_HEADS * MLA_HEAD_PAD), np.float32)
    for h in range(MLA_HEADS):
        for j in range(MLA_ROPE):
            place[j, h * MLA_HEAD_PAD + MLA_NOPE + j] = 1.0
            place[MLA_ROPE + j, h * MLA_HEAD_PAD + MLA_NOPE + j] = 1.0
    cat = lambda ts: jnp.concatenate(ts, axis=1).astype(BF16)
    return cat(wq), cat(wq_sw), cat(wk), cat(wv), jnp.asarray(place).astype(BF16)


def _mla_prep_kernel(z_ref, qg_ref, kg_ref, wq_ref, wqs_ref, wk_ref, wv_ref, pl_ref, cq_ref, sq_ref, csk_ref,
                     q_ref, k_ref, v_ref):
    z = z_ref[0]
    norm = lambda t, g: (t * lax.rsqrt(jnp.mean(t * t, axis=-1, keepdims=True) + EPS) * g).astype(BF16)
    cq = norm(z[:, :MLA_Q_LORA], qg_ref[...])
    ckv = norm(z[:, MLA_Q_LORA:MLA_Q_LORA + MLA_KV_LORA], kg_ref[...])
    scale = (MLA_NOPE + MLA_ROPE) ** -0.5
    q = _dot(cq, wq_ref[...]) * cq_ref[...] + _dot(cq, wqs_ref[...]) * sq_ref[...]
    q_ref[0] = (q * scale).astype(BF16)
    kr = z[:, MLA_Q_LORA + MLA_KV_LORA:] * csk_ref[...]
    k_ref[0] = (_dot(ckv, wk_ref[...]) + _dot_exact_rhs(kr, pl_ref[...])).astype(BF16)
    v_ref[0] = _dot(ckv, wv_ref[...]).astype(BF16)


def mla_prep(zm, qn_g, kvn_g, weights, tables):
    bsz, seq, wd = zm.shape
    tm = _tile(seq, 512)
    wq, wq_sw, wk, wv, place = weights
    cos_q, sin_q, cs_k = tables
    qw, vw = MLA_HEADS * MLA_HEAD_PAD, MLA_HEADS * MLA_V
    full = lambda t: pl.BlockSpec(t.shape, lambda b, i: (0,) * t.ndim)
    pos = lambda t: pl.BlockSpec((tm, t.shape[1]), lambda b, i: (i, 0))
    out = lambda w: pl.BlockSpec((1, tm, w), lambda b, i: (b, i, 0))
    qg, kg = qn_g.reshape(1, -1), kvn_g.reshape(1, -1)
    return pl.pallas_call(
        _mla_prep_kernel, grid=(bsz, seq // tm),
        in_specs=[pl.BlockSpec((1, tm, wd), lambda b, i: (b, i, 0)), full(qg), full(kg), full(wq), full(wq_sw),
                  full(wk), full(wv), full(place), pos(cos_q), pos(sin_q), pos(cs_k)],
        out_specs=[out(qw), out(qw), out(vw)],
        out_shape=[jax.ShapeDtypeStruct((bsz, seq, qw), BF16), jax.ShapeDtypeStruct((bsz, seq, qw), BF16),
                   jax.ShapeDtypeStruct((bsz, seq, vw), BF16)],
        compiler_params=_params("parallel", "parallel"), name="mla_prep")(
            zm, qg, kg, wq, wq_sw, wk, wv, place, cos_q, sin_q, cs_k)


def _mla_attn_kernel(q_ref, *refs, n_seg):
    k_refs, v_refs, o_ref = refs[:n_seg], refs[n_seg:2 * n_seg], refs[2 * n_seg]
    outs = []
    for h in range(MLA_HEADS):
        q = q_ref[0, :, h * MLA_HEAD_PAD:(h + 1) * MLA_HEAD_PAD]
        s = [lax.dot_general(q, k_ref[0, :, h * MLA_HEAD_PAD:(h + 1) * MLA_HEAD_PAD], NT_DIMS,
                             preferred_element_type=F32) for k_ref in k_refs]
        m = functools.reduce(jnp.maximum, [jnp.max(t, axis=-1, keepdims=True) for t in s])
        p = [jnp.exp(t - m) for t in s]
        den = functools.reduce(jnp.add, [jnp.sum(t, axis=-1, keepdims=True) for t in p])
        o = functools.reduce(jnp.add, [_dot(t.astype(BF16), v_ref[0, :, h * MLA_V:(h + 1) * MLA_V])
                                       for t, v_ref in zip(p, v_refs)])
        outs.append(o / den)
    o_ref[0] = jnp.concatenate(outs, axis=-1).astype(BF16)


def mla_attention(q, ks, vs):
    bsz, seq, qw = q.shape
    tq = _tile(seq, 256)
    vw = MLA_HEADS * MLA_V
    seg = lambda t: pl.BlockSpec((1,) + t.shape[1:], lambda b, i: (b, 0, 0))
    return pl.pallas_call(
        functools.partial(_mla_attn_kernel, n_seg=len(ks)), grid=(bsz, seq // tq),
        in_specs=[pl.BlockSpec((1, tq, qw), lambda b, i: (b, i, 0))] + [seg(t) for t in ks] + [seg(t) for t in vs],
        out_specs=pl.BlockSpec((1, tq, vw), lambda b, i: (b, i, 0)),
        out_shape=jax.ShapeDtypeStruct((bsz, seq, vw), BF16),
        compiler_params=_params("parallel", "parallel"), name="mla_attention")(q, *ks, *vs)


def _outproj_kernel(a_ref, b_ref, c_ref, d_ref, w_ref, x_ref, g1_ref, sh_ref, sc_ref, g_ref, rw_ref,
                    xo_ref, h_ref, lg_ref):
    mix = None
    for j, o_ref in enumerate((a_ref, b_ref, c_ref, d_ref)):
        wd = o_ref.shape[-1]
        part = _dot(o_ref[0], w_ref[j * wd:(j + 1) * wd, :])
        mix = part if mix is None else mix + part
    x = x_ref[0] + g1_ref[0] * mix
    xo_ref[0] = x
    ms = jnp.mean(x * x, axis=-1, keepdims=True)
    h = x * lax.rsqrt(ms + EPS) * g_ref[...] * (1.0 + sc_ref[0]) + sh_ref[0]
    _store_row_slabs(h_ref, _pack_bf16_pairs(h))
    lg_ref[...] =lax.dot_general(rw_ref[...], h, NT_DIMS, precision=HIGHEST, preferred_element_type=F32)


def out_projection(parts, w_out, x, g1, shift, scale, g, router_w):
    bsz, seq, dm = x.shape
    tm = _tile(seq, 512)
    nt = seq // tm
    ne = router_w.shape[1]
    groups = dm // 2 // 128
    rows = lambda w: pl.BlockSpec((1, tm, w), lambda b, i: (b, i, 0))
    vec = pl.BlockSpec((1, 1, dm), lambda b, i: (b, 0, 0))
    full = lambda shape: pl.BlockSpec(shape, lambda b, i: (0,) * len(shape))
    return pl.pallas_call(
        _outproj_kernel, grid=(bsz, nt),
        in_specs=[rows(p.shape[-1]) for p in parts] + [full(w_out.shape), rows(dm), vec, vec, vec, full((1, dm)),
                                                       full((ne, dm))],
        out_specs=[rows(dm), pl.BlockSpec((tm * groups, 128), lambda b, i: (b * nt + i, 0)),
                   pl.BlockSpec((ne, tm), lambda b, i: (0, b * nt + i))],
        out_shape=[jax.ShapeDtypeStruct((bsz, seq, dm), F32), jax.ShapeDtypeStruct((bsz * seq * groups, 128), jnp.int32),
                   jax.ShapeDtypeStruct((ne, bsz * seq), F32)],
        compiler_params=_params("parallel", "parallel"), name="out_projection")(
            *parts, w_out.astype(BF16), x, g1, shift, scale, g.reshape(1, dm), router_w.T)


def _route_kernel(lg_ref, b_ref, e_ref, w_ref):
    ne, tt = lg_ref.shape
    per = ne // N_GROUPS
    neg = -jnp.inf
    scores = jax.nn.sigmoid(lg_ref[...])
    biased = scores + b_ref[...]
    v3 = biased.reshape(N_GROUPS, per, tt)
    e_in = lax.broadcasted_iota(jnp.int32, v3.shape, 1).astype(F32)
    m1 = jnp.max(v3, axis=1, keepdims=True)
    i1 = jnp.min(jnp.where(v3 == m1, e_in, float(per)), axis=1, keepdims=True)
    m2 = jnp.max(jnp.where(e_in == i1, neg, v3), axis=1, keepdims=True)
    grp = (m1 + m2).reshape(N_GROUPS, tt)

    def pick(cur, count):
        ids = lax.broadcasted_iota(jnp.int32, cur.shape, 0).astype(F32)
        marks = jnp.zeros(cur.shape, F32)
        picked = []
        for _ in range(count):
            m = jnp.max(cur, axis=0, keepdims=True)
            first = jnp.min(jnp.where(cur == m, ids, float(cur.shape[0])), axis=0, keepdims=True)
            hit = ids == first
            marks = jnp.where(hit, 1.0, marks)
            cur = jnp.where(hit, neg, cur)
            picked.append(first)
        return marks, picked

    grp_on, _ = pick(grp, TOPK_GROUPS)
    exp_on = jnp.broadcast_to(grp_on.reshape(N_GROUPS, 1, tt), v3.shape).reshape(ne, tt)
    chosen, picked = pick(jnp.where(exp_on > 0.0, biased, neg), TOP_K)
    w = scores * chosen
    gate = w / jnp.sum(w, axis=0, keepdims=True) * ROUTED_SCALE
    ids = lax.broadcasted_iota(jnp.int32, gate.shape, 0).astype(F32)
    e_ref[...] = jnp.concatenate(picked, axis=0).astype(jnp.int32)
    w_ref[...] = jnp.concatenate([jnp.sum(jnp.where(ids == p, gate, 0.0), axis=0, keepdims=True) for p in picked], axis=0)


def route(logits_t, router_b):
    ne, n = logits_t.shape
    tt = _tile(n, 1024)
    out = pl.BlockSpec((TOP_K, tt), lambda i: (0, i))
    return pl.pallas_call(
        _route_kernel, grid=(n // tt,),
        in_specs=[pl.BlockSpec((ne, tt), lambda i: (0, i)), pl.BlockSpec((ne, 1), lambda i: (0, 0))],
        out_specs=[out, out],
        out_shape=[jax.ShapeDtypeStruct((TOP_K, n), jnp.int32), jax.ShapeDtypeStruct((TOP_K, n), F32)],
        compiler_params=_params("parallel"), name="route")(logits_t, router_b.reshape(ne, 1))


def moe_plan(eid, wgt, chunk, xg, yg):
    n = eid.shape[1]
    nc, na = n // chunk, chunk * TOP_K
    ns = na // MOE_ROWS + N_EXPERTS
    key = eid.T.reshape(nc, na) * na + jnp.arange(na, dtype=jnp.int32)
    skey, sw = lax.sort((key, wgt.T.reshape(nc, na)), dimension=1, num_keys=1)
    e_sorted, tok = skey // na, (skey % na) // TOP_K
    experts = jnp.arange(N_EXPERTS, dtype=jnp.int32)
    ends = jnp.sum(e_sorted[:, None, :] <= experts[None, :, None], axis=-1, dtype=jnp.int32)
    starts = jnp.concatenate([jnp.zeros((nc, 1), jnp.int32), ends[:, :-1]], axis=1)
    pad_end = jnp.cumsum(-(-(ends - starts) // MOE_ROWS) * MOE_ROWS, axis=1)
    pad_start = jnp.concatenate([jnp.zeros((nc, 1), jnp.int32), pad_end[:, :-1]], axis=1)
    first = jnp.arange(ns, dtype=jnp.int32) * MOE_ROWS
    exp = jnp.minimum(jnp.sum(pad_end[:, None, :] <= first[None, :, None], axis=-1, dtype=jnp.int32), N_EXPERTS - 1)
    active = first[None, :] < pad_end[:, -1:]
    take = lambda t, i: jnp.take_along_axis(t, i, axis=1)
    rank = first[None, :, None] + jnp.arange(MOE_ROWS, dtype=jnp.int32) - take(pad_start, exp)[..., None]
    real = (rank < take(ends - starts, exp)[..., None]) & active[..., None]
    src = jnp.clip(take(starts, exp)[..., None] + rank, 0, na - 1).reshape(nc, ns * MOE_ROWS)
    rows = lambda t: take(t, src).reshape(nc, ns, MOE_ROWS)
    flat = lambda t: t.reshape(nc * ns, 1, MOE_ROWS)
    gather_at = flat(jnp.where(real, rows(tok), 0) * xg)
    scatter_at = flat(jnp.where(real, rows(tok), chunk) * yg)
    weight = flat(jnp.where(real, rows(sw), 0.0))
    return gather_at, scatter_at, weight, exp.reshape(nc * ns), active.astype(jnp.int32).reshape(nc * ns)


def _moe_tick(x_ref, y_ref, idx_ref, w_ref, tick, wgu_ref, wd_ref, gather_to, mm_from, mm_to, scatter_from):
    bm = MOE_ROWS
    stride = bm + 1
    xg = mm_from.shape[0] // stride
    yg = mm_to.shape[0] // stride
    for mi in range(bm):
        src = pl.multiple_of(idx_ref[0, 0, 2 * tick * bm + mi], xg)
        gather_to[pl.ds(mi, xg, stride=stride), :] = x_ref[0, pl.ds(src, xg), :]
    low, high = _unpack_bf16_pairs([mm_from[j * stride:j * stride + bm, :] for j in range(xg)])
    half = low.shape[1]
    ff = wd_ref.shape[1]
    up = _dot(low, wgu_ref[0, :half, :]) + _dot(high, wgu_ref[0, half:, :])
    act = (_silu(up[:, :ff]) * up[:, ff:]).astype(BF16)
    out = _dot(act, wd_ref[0])
    for j in range(yg):
        mm_to[j * stride:j * stride + bm, :] = out[:, j * 128:(j + 1) * 128]
    for base in range(0, bm, MOE_BATCH):
        at = [pl.multiple_of(idx_ref[0, 0, (2 * tick + 1) * bm + mi], yg) for mi in range(base, base + MOE_BATCH)]
        new = [y_ref[0, pl.ds(i, yg), :] + w_ref[0, 0, tick * bm + mi] * scatter_from[pl.ds(mi, yg, stride=stride), :]
               for i, mi in zip(at, range(base, base + MOE_BATCH))]
        for i, v in zip(at, new):
            y_ref[0, pl.ds(i, yg), :] = v


def _moe_routed_kernel(exp_ref, act_ref, x_ref, idx_ref, w_ref, wgua_ref, wda_ref, wgub_ref, wdb_ref,
                       y_ref, xt0, xt1, ot0, ot1, *, ns):
    c, g = pl.program_id(0), pl.program_id(1)

    @pl.when(g == 0)
    def _():
        y_ref[...] = jnp.zeros_like(y_ref)

    @pl.when((c == 0) & (g == 0))
    def _():
        xt1[...] = jnp.zeros_like(xt1)
        ot0[...] = jnp.zeros_like(ot0)
        ot1[...] = jnp.zeros_like(ot1)

    def live(step):
        return (step >= 0) & (step < ns) & (act_ref[c * ns + jnp.clip(step, 0, ns - 1)] != 0)

    t = 2 * g

    @pl.when(live(t - 2) | live(t - 1) | live(t) | live(t + 1))
    def _():
        _moe_tick(x_ref, y_ref, idx_ref, w_ref, 0, wgua_ref, wda_ref,
                  gather_to=xt0, mm_from=xt1, mm_to=ot1, scatter_from=ot0)
        _moe_tick(x_ref, y_ref, idx_ref, w_ref, 1, wgub_ref, wdb_ref,
                  gather_to=xt1, mm_from=xt0, mm_to=ot0, scatter_from=ot1)


def moe_routed(slabs, eid, wgt, wgu, wd):
    n = eid.shape[1]
    xg = slabs.shape[0] // n
    yg = 2 * xg
    dm, ff = wd.shape[2], wd.shape[1]
    chunk = _tile(n, MOE_CHUNK)
    nc = n // chunk
    ns = chunk * TOP_K // MOE_ROWS + N_EXPERTS
    assert ns % 2 == 0
    steps = (ns + 2) // 2
    last = ns - 1
    gather_at, scatter_at, weight, exp, active = moe_plan(eid, wgt, chunk, xg, yg)
    tick = jnp.arange(steps, dtype=jnp.int32)[:, None] * 2 + jnp.arange(2, dtype=jnp.int32)[None, :]
    gathered = jnp.minimum(tick, last)
    scattered = jnp.where(tick >= 2, tick - 2, last)
    per_step = lambda t, at: jnp.take(t.reshape(nc, ns, MOE_ROWS), at, axis=1)
    idx = jnp.concatenate([per_step(gather_at, gathered[:, 0]), per_step(scatter_at, scattered[:, 0]),
                           per_step(gather_at, gathered[:, 1]), per_step(scatter_at, scattered[:, 1])], axis=-1)
    wts = jnp.concatenate([per_step(weight, scattered[:, 0]), per_step(weight, scattered[:, 1])], axis=-1)
    smem = lambda width: pl.BlockSpec((1, 1, width), lambda c, g, e, a: (c * steps + g, 0, 0), memory_space=pltpu.SMEM)
    computed = lambda tk: lambda g: jnp.clip(2 * g + tk - 1, 0, last)
    expert = lambda shape, step: pl.BlockSpec((1,) + shape, lambda c, g, e, a: (e[c * ns + step(g)], 0, 0))
    once = pl.Buffered(1)
    stage = lambda groups, dtype: pltpu.VMEM((-(-groups * (MOE_ROWS + 1) // 8) * 8, 128), dtype)
    return pl.pallas_call(
        functools.partial(_moe_routed_kernel, ns=ns),
        grid_spec=pltpu.PrefetchScalarGridSpec(
            num_scalar_prefetch=2, grid=(nc, steps),
            in_specs=[pl.BlockSpec((1, chunk * xg, 128), lambda c, g, e, a: (c, 0, 0), pipeline_mode=once),
                      smem(4 * MOE_ROWS), smem(2 * MOE_ROWS),
                      expert((dm, 2 * ff), computed(0)), expert((ff, dm), computed(0)),
                      expert((dm, 2 * ff), computed(1)), expert((ff, dm), computed(1))],
            out_specs=pl.BlockSpec((1, chunk * yg + 8, 128), lambda c, g, e, a: (c, 0, 0), pipeline_mode=once),
            scratch_shapes=[stage(xg, jnp.int32), stage(xg, jnp.int32), stage(yg, F32), stage(yg, F32)]),
        out_shape=jax.ShapeDtypeStruct((nc, chunk * yg + 8, 128), F32),
        compiler_params=_params("arbitrary", "arbitrary"), name="moe_routed")(
            exp, active, slabs.reshape(nc, chunk * xg, 128), idx.reshape(nc * steps, 1, 4 * MOE_ROWS),
            wts.reshape(nc * steps, 1, 2 * MOE_ROWS), wgu, wd, wgu, wd)


def _moe_finish_kernel(y_ref, h_ref, x_ref, g2_ref, wg_ref, wu_ref, wd_ref, fg_ref, o_ref, *, final_norm):
    tm = x_ref.shape[0]
    low, high = _unpack_bf16_pairs(_load_row_slabs(h_ref, tm, h_ref.shape[0] // tm))
    routed = jnp.concatenate(_load_row_slabs(y_ref, tm, y_ref.shape[1] // tm, lead=(0,)), axis=1)
    half = low.shape[1]
    up = lambda w_ref: _dot(low, w_ref[:half, :]) + _dot(high, w_ref[half:, :])
    shared = _dot((_silu(up(wg_ref)) * up(wu_ref)).astype(BF16), wd_ref[...])
    x = x_ref[...] + g2_ref[0] * (routed + shared)
    if final_norm:
        x = x * lax.rsqrt(jnp.mean(x * x, axis=-1, keepdims=True) + EPS) * fg_ref[...]
    o_ref[...] = x


def moe_finish(y, slabs, x, g2, sg, su, sd, final_g, final_norm):
    bsz, seq, dm = x.shape
    n = bsz * seq
    tm = _tile(seq, 512)
    nt = seq // tm
    xg, yg = slabs.shape[0] // n, dm // 128
    per_chunk = (y.shape[1] - 8) // yg // tm
    rows = lambda w: pl.BlockSpec((tm, w), lambda i: (i, 0))
    full = lambda t: pl.BlockSpec(t.shape, lambda i: (0,) * t.ndim)
    sg, su, sd, fg = sg.astype(BF16), su.astype(BF16), sd.astype(BF16), final_g.reshape(1, dm)
    out = pl.pallas_call(
        functools.partial(_moe_finish_kernel, final_norm=final_norm), grid=(n // tm,),
        in_specs=[pl.BlockSpec((1, tm * yg, 128), lambda i: (i // per_chunk, i % per_chunk, 0)),
                  pl.BlockSpec((tm * xg, 128), lambda i: (i, 0)), rows(dm),
                  pl.BlockSpec((1, 1, dm), lambda i: (i // nt, 0, 0)), full(sg), full(su), full(sd), full(fg)],
        out_specs=rows(dm), out_shape=jax.ShapeDtypeStruct((n, dm), F32),
        compiler_params=_params("parallel"), name="moe_finish")(y, slabs, x.reshape(n, dm), g2, sg, su, sd, fg)
    return out.reshape(bsz, seq, dm)


def kernel(x, c, ctx, c_ctx, ada_w, ada_b, norm1_g, norm2_g, w_in, w_out, gla_w_a2, gla_b_a, gla_norm_g,
           conv_dw_w, conv_dw_b, conv_ln_g, conv_ln_b, conv_pw_w, conv_pw_b,
           s5_lam_re, s5_lam_im, s5_log_dt, s5_b_re, s5_b_im, s5_c_re, s5_c_im, s5_d, s5_glu_w, s5_glu_b,
           mla_qn_g, mla_kvn_g, mla_w_uq, mla_w_ukv,
           moe_router_w, moe_router_b, moe_w_gate, moe_w_up, moe_w_down,
           shared_w_gate, shared_w_up, shared_w_down, final_g):
    bsz, seq, dm = x.shape
    ctx_len = ctx.shape[1]
    depth = ada_w.shape[0]
    cc = jnp.concatenate([c, c_ctx[None], jnp.zeros((-(bsz + 1) % 8, dm), F32)], axis=0)
    tables = mla_tables(seq, True)
    tables_c = mla_tables(ctx_len, False)
    xc = ctx
    for l in range(depth):
        last = l == depth - 1
        mod = ada_mod(cc, ada_w[l], ada_b[l])
        lat = [t.reshape(bsz, 1, dm) for t in jnp.split(mod[:bsz], 6, axis=-1)]
        con = [jnp.broadcast_to(t.reshape(1, 1, dm), (bsz, 1, dm)) for t in jnp.split(mod[bsz], 6)]
        sh1, sc1, g1, sh2, sc2, g2 = lat
        sh1c, sc1c, g1c, sh2c, sc2c, g2c = con
        w_in_l = assemble_w_in(w_in[l])
        zg, zv, zs, zm = in_projection(x, sh1, sc1, norm1_g[l], w_in_l)
        zgc, zvc, zsc, zmc = in_projection(xc, sh1c, sc1c, norm1_g[l], w_in_l)
        conv_p = (conv_dw_w[l], conv_dw_b[l], conv_ln_g[l], conv_ln_b[l], conv_pw_w[l], conv_pw_b[l])
        o_gla, oc_gla = gla_mixer(zg, zgc, gla_w_a2[l], gla_b_a[l], gla_norm_g[l])
        o_conv = conformer_conv(zv, *conv_p)
        o_s5, oc_s5 = s5_mixer(zs, zsc, s5_lam_re[l], s5_lam_im[l], s5_log_dt[l], s5_b_re[l], s5_b_im[l],
                               s5_c_re[l], s5_c_im[l], s5_d[l], s5_glu_w[l], s5_glu_b[l])
        mla_w = mla_weights(mla_w_uq[l], mla_w_ukv[l])
        q, k, v = mla_prep(zm, mla_qn_g[l], mla_kvn_g[l], mla_w, tables)
        qc, kc, vc = mla_prep(zmc, mla_qn_g[l], mla_kvn_g[l], mla_w, tables_c)
        o_mla = mla_attention(q, [k, kc], [v, vc])
        experts = (jnp.concatenate([moe_w_gate[l], moe_w_up[l]], axis=2).astype(BF16), moe_w_down[l].astype(BF16))
        shared = (shared_w_gate[l], shared_w_up[l], shared_w_down[l])

        def ffn(parts, x_in, gate1, shift, scale, gate2, final_norm):
            x_mid, slabs, logits = out_projection(parts, w_out[l], x_in, gate1, shift, scale, norm2_g[l], moe_router_w[l])
            y = moe_routed(slabs, *route(logits, moe_router_b[l]), *experts)
            return moe_finish(y, slabs, x_mid, gate2, *shared, final_g, final_norm)

        if not last:
            oc_conv = conformer_conv(zvc, *conv_p)
            oc_mla = mla_attention(qc, [kc], [vc])
            xc = ffn([oc_gla, oc_conv, oc_s5, oc_mla], xc, g1c, sh2c, sc2c, g2c, False)
        x = ffn([o_gla, o_conv, o_s5, o_mla], x, g1, sh2, sc2, g2, last)
    return x
```

```python
import functools
import math

import numpy as np
import jax
import jax.numpy as jnp
from jax import lax
from jax.experimental import pallas as pl
from jax.experimental.pallas import tpu as pltpu

F32 = jnp.float32
BF16 = jnp.bfloat16
HIGHEST = lax.Precision.HIGHEST
EPS = 1e-6

GRID_W = 64

GLA_HEADS = 4
GLA_DK = 32
GLA_DV = 64
GLA_LOWRANK = 16
GLA_TAU = 16.0
GLA_CHUNK = 64
GLA_UNROLL = 2
GLA_FINISH_ROWS = 256

CONV_C = 256
CONV_K = 31
CONV_HALO = 16

S5_C = 256
S5_GROUP = 16
S5_NG = 16
S5_P = 64
S5_STATE = S5_NG * S5_P
S5_STRIP = 256

MLA_HEADS = 4
MLA_NOPE = 64
MLA_ROPE = 32
MLA_V = 64
MLA_Q_LORA = 256
MLA_KV_LORA = 128
MLA_HEAD_PAD = 128
ROPE_AXIS = MLA_ROPE // 2
ROPE_BASE = 10000.0

N_EXPERTS = 64
TOP_K = 8
N_GROUPS = 8
TOPK_GROUPS = 4
ROUTED_SCALE = 2.5
MOE_CHUNK = 4096
MOE_ROWS = 256
MOE_BATCH = 8

IN_SPLITS = (128, 128, 256, 256, 16, 16, 512, 256, 256, 128, 32)
W_GLA, W_CONV, W_S5, W_MLA = 1024, 512, 256, 512

VMEM_LIMIT = 48 * 1024 * 1024

NT_DIMS = (((1,), (1,)), ((), ()))
TN_DIMS = (((0,), (0,)), ((), ()))


def _params(*sem, vmem=VMEM_LIMIT):
    return pltpu.CompilerParams(dimension_semantics=sem, vmem_limit_bytes=vmem)


def _silu(v):
    return v * jax.nn.sigmoid(v)


def _dot(a, b, **kw):
    return jnp.dot(a, b, preferred_element_type=F32, **kw)


def _split_bf16(x):
    hi = x.astype(BF16)
    rest = x - hi.astype(F32)
    mid = rest.astype(BF16)
    return hi, mid, (rest - mid.astype(F32)).astype(BF16)


def _dot_exact_lhs(m, x):
    n = x.shape[1]
    prod = _dot(m, jnp.concatenate(_split_bf16(x), axis=1))
    return prod[:, :n] + prod[:, n:2 * n] + prod[:, 2 * n:]


def _dot_exact_rhs(x, m):
    n = x.shape[0]
    prod = _dot(jnp.concatenate(_split_bf16(x), axis=0), m)
    return prod[:n] + prod[n:2 * n] + prod[2 * n:]


def _tile(n, pref):
    return pref if n % pref == 0 else n


def _pack_bf16_pairs(h):
    bits = pltpu.bitcast(h.astype(BF16).astype(F32), jnp.int32)
    w = h.shape[-1] // 2
    return lax.shift_right_logical(bits[:, :w], 16) | bits[:, w:]


def _store_row_slabs(ref, value):
    rows, g = value.shape[0], value.shape[1] // 128
    for j in range(g):
        ref[pl.ds(j, rows, stride=g), :] = value[:, j * 128:(j + 1) * 128]


def _load_row_slabs(ref, rows, g, lead=()):
    return [ref[lead + (pl.ds(j, rows, stride=g), slice(None))] for j in range(g)]


def _unpack_bf16_pairs(words):
    low = jnp.concatenate([pltpu.bitcast(v << 16, F32) for v in words], axis=1)
    high = jnp.concatenate([pltpu.bitcast(v & -65536, F32) for v in words], axis=1)
    return low.astype(BF16), high.astype(BF16)


def _ada_kernel(c_ref, w_ref, b_ref, o_ref):
    o_ref[...] = _dot(_silu(c_ref[...]), w_ref[...], precision=HIGHEST) + b_ref[...]


def ada_mod(cc, w, b):
    rows, dm = cc.shape
    n = w.shape[1]
    tn = _tile(n, 512)
    return pl.pallas_call(
        _ada_kernel, grid=(n // tn,),
        in_specs=[pl.BlockSpec((rows, dm), lambda j: (0, 0)),
                  pl.BlockSpec((dm, tn), lambda j: (0, j)),
                  pl.BlockSpec((1, tn), lambda j: (0, j))],
        out_specs=pl.BlockSpec((rows, tn), lambda j: (0, j)),
        out_shape=jax.ShapeDtypeStruct((rows, n), F32),
        compiler_params=_params("arbitrary"), name="ada_mod")(cc, w, b.reshape(1, n))


def _inproj_kernel(x_ref, sh_ref, sc_ref, g_ref, w_ref, *o_refs):
    x = x_ref[0]
    ms = jnp.mean(x * x, axis=-1, keepdims=True)
    h = (x * lax.rsqrt(ms + EPS) * g_ref[...] * (1.0 + sc_ref[0]) + sh_ref[0]).astype(BF16)
    off = 0
    for o_ref in o_refs:
        w = o_ref.shape[-1]
        o_ref[0] = _dot(h, w_ref[:, off:off + w])
        off += w


def in_projection(x, shift, scale, g, w):
    bsz, seq, dm = x.shape
    tm = _tile(seq, 512)
    widths = (W_GLA, W_CONV, W_S5, W_MLA)
    vec = pl.BlockSpec((1, 1, dm), lambda b, i: (b, 0, 0))
    return pl.pallas_call(
        _inproj_kernel, grid=(bsz, seq // tm),
        in_specs=[pl.BlockSpec((1, tm, dm), lambda b, i: (b, i, 0)), vec, vec,
                  pl.BlockSpec((1, dm), lambda b, i: (0, 0)),
                  pl.BlockSpec(w.shape, lambda b, i: (0, 0))],
        out_specs=[pl.BlockSpec((1, tm, wd), lambda b, i: (b, i, 0)) for wd in widths],
        out_shape=[jax.ShapeDtypeStruct((bsz, seq, wd), F32) for wd in widths],
        compiler_params=_params("parallel", "parallel"), name="in_projection")(x, shift, scale, g.reshape(1, dm), w)


def _rope_swap(t):
    s = t.reshape(t.shape[:-1] + (2, 2, ROPE_AXIS // 2))
    return jnp.stack([-s[..., 1, :], s[..., 0, :]], axis=-2).reshape(t.shape)


def assemble_w_in(w_in):
    edges = np.cumsum((0,) + IN_SPLITS)
    q, k, v, r, a_f, a_b, conv, s5, cq, ckv, kr = [w_in[:, edges[i]:edges[i + 1]] for i in range(len(IN_SPLITS))]
    pad = lambda t, w: jnp.pad(t, ((0, 0), (0, w - t.shape[1])))
    cols = [q, k, v, r, pad(a_f, 128), pad(a_b, 128), conv, s5, cq, ckv, pad(jnp.concatenate([kr, _rope_swap(kr)], 1), 128)]
    return jnp.concatenate(cols, axis=1).astype(BF16)


def _gla_kernel(z_ref, zc_ref, wa_ref, ba_ref, g_ref, o_ref, oc_ref, of_scr, ob_scr, st_scr, *, seq, ctx_len):
    c = GLA_CHUNK
    nh = GLA_HEADS
    kw = nh * GLA_DK
    vw = nh * GLA_DV
    iota = lambda shape, d: lax.broadcasted_iota(jnp.int32, shape, d)
    r2, c2 = iota((2 * c, 2 * c), 0), iota((2 * c, 2 * c), 1)
    tri = (((r2 < c) & (c2 <= r2)) | ((r2 >= c) & (c2 >= r2))).astype(F32).astype(BF16)
    ra, ca = iota((nh * 2 * c, 2 * c), 0) & (2 * c - 1), iota((nh * 2 * c, 2 * c), 1)
    causal = ((ra < c) & (ca <= ra)) | ((ra >= c) & (ca >= ra))
    fwd_rows = iota((2 * c, 1), 0) < c
    k_of = lambda t: t >> int(math.log2(GLA_DK))
    v_of = lambda t: t >> int(math.log2(GLA_DV))
    k_head = [(k_of(iota((1, kw), 1)) == h).astype(F32) for h in range(nh)]
    v_head = [(v_of(iota((1, vw), 1)) == h).astype(F32) for h in range(nh)]
    st_rows = iota((2 * vw, kw), 0)
    st_mask = (v_of(st_rows & (vw - 1)) == k_of(iota((2 * vw, kw), 1))).astype(F32)
    st_fwd = st_rows < vw
    head_mean = ((v_of(iota((vw, vw), 0)) == v_of(iota((vw, vw), 1))).astype(F32) * (1.0 / GLA_DV)).astype(BF16)

    def pair(ref, sf, sb, st):
        blk_f, blk_b = ref[0, pl.ds(sf, c), :], ref[0, pl.ds(sb, c), :]
        blk = jnp.concatenate([blk_f, blk_b], axis=0)
        q = blk[:, 0:kw] * (GLA_DK ** -0.5)
        k = blk[:, kw:2 * kw]
        vb = blk[:, 2 * kw:2 * kw + vw].astype(BF16)
        a0 = 2 * kw + 2 * vw
        a_low = jnp.concatenate([blk_f[:, a0:a0 + 128], blk_b[:, a0 + 128:a0 + 256]], axis=0).astype(BF16)
        z2 = _dot(a_low, wa_ref[...])
        zl = jnp.where(fwd_rows, z2[:, :kw] + ba_ref[0], z2[:, kw:] + ba_ref[1])
        la = (jnp.minimum(zl, 0.0) - jnp.log1p(jnp.exp(-jnp.abs(zl)))) / GLA_TAU
        b = _dot_exact_lhs(tri, la)
        b_end = jnp.where(fwd_rows, b[c - 1:c, :], b[c:c + 1, :])
        q_dec = q * jnp.exp(b)
        k_inv = (k * jnp.exp(-b)).astype(BF16)
        k_tail = (k * jnp.exp(b_end - b)).astype(BF16)
        qs = jnp.concatenate([q_dec * k_head[h] for h in range(nh)], axis=0).astype(BF16)
        att = lax.dot_general(qs, k_inv, NT_DIMS, preferred_element_type=F32)
        att = jnp.where(causal, att, 0.0).astype(BF16)
        o_all = _dot(att, vb)
        o = o_all[0:2 * c] * v_head[0]
        for h in range(1, nh):
            o = o + o_all[h * 2 * c:(h + 1) * 2 * c] * v_head[h]
        both = lax.dot_general(q_dec.astype(BF16), st.astype(BF16), NT_DIMS, preferred_element_type=F32)
        o = o + jnp.where(fwd_rows, both[:, :vw], both[:, vw:])
        v2 = jnp.concatenate([jnp.where(fwd_rows, vb, 0), jnp.where(fwd_rows, 0, vb)], axis=1)
        upd = lax.dot_general(v2, k_tail, TN_DIMS, preferred_element_type=F32)
        decay = jnp.where(st_fwd, jnp.exp(b[c - 1:c, :]), jnp.exp(b[c:c + 1, :]))
        return o, st * decay + upd * st_mask

    def scan(ref, n_chunks, base):
        unroll = GLA_UNROLL if n_chunks % GLA_UNROLL == 0 else 1

        @pl.loop(0, n_chunks // unroll)
        def _(it):
            st = st_scr[...]
            for u in range(unroll):
                n = it * unroll + u
                sf = pl.multiple_of(n * c, c)
                sb = pl.multiple_of((n_chunks - 1 - n) * c, c)
                o, st = pair(ref, sf, sb, st)
                of_scr[pl.ds(pl.multiple_of(base + sf, c), c), :] = o[:c]
                ob_scr[pl.ds(pl.multiple_of(base + sb, c), c), :] = o[c:]
            st_scr[...] = st

    def finish(ref, out_ref, n_rows, base):
        rows = GLA_FINISH_ROWS if n_rows % GLA_FINISH_ROWS == 0 else c

        @pl.loop(0, n_rows // rows)
        def _(t):
            s = pl.multiple_of(t * rows, rows)
            at = pl.ds(pl.multiple_of(base + s, rows), rows)
            o = of_scr[at, :] + ob_scr[at, :]
            ms = _dot_exact_rhs(o * o, head_mean)
            gate = ref[0, pl.ds(s, rows), 2 * kw + vw:2 * kw + 2 * vw]
            out_ref[0, pl.ds(s, rows), :] = (o * lax.rsqrt(ms + EPS) * g_ref[...] * _silu(gate)).astype(BF16)

    st_scr[...] = jnp.zeros_like(st_scr)
    scan(zc_ref, ctx_len // c, 0)
    scan(z_ref, seq // c, ctx_len)
    finish(zc_ref, oc_ref, ctx_len, 0)
    finish(z_ref, o_ref, seq, ctx_len)


def gla_mixer(zg, zgc, w_a2, b_a, norm_g):
    bsz, seq, wd = zg.shape
    ctx_len = zgc.shape[1]
    vw = GLA_HEADS * GLA_DV
    kw = GLA_HEADS * GLA_DK
    wa = jnp.concatenate(list(jnp.pad(w_a2, ((0, 0), (0, 128 - GLA_LOWRANK), (0, 0)))), axis=1).astype(BF16)
    full = lambda shape: pl.BlockSpec(shape, lambda b: (0,) * len(shape))
    return pl.pallas_call(
        functools.partial(_gla_kernel, seq=seq, ctx_len=ctx_len), grid=(bsz,),
        in_specs=[pl.BlockSpec((1, seq, wd), lambda b: (b, 0, 0)),
                  pl.BlockSpec((1, ctx_len, wd), lambda b: (b, 0, 0)),
                  full((128, 2 * kw)), full((2, 1, kw)), full((1, vw))],
        out_specs=[pl.BlockSpec((1, seq, vw), lambda b: (b, 0, 0)),
                   pl.BlockSpec((1, ctx_len, vw), lambda b: (b, 0, 0))],
        out_shape=[jax.ShapeDtypeStruct((bsz, seq, vw), BF16), jax.ShapeDtypeStruct((bsz, ctx_len, vw), BF16)],
        scratch_shapes=[pltpu.VMEM((seq + ctx_len, vw), F32), pltpu.VMEM((seq + ctx_len, vw), F32),
                        pltpu.VMEM((2 * vw, kw), F32)],
        compiler_params=_params("parallel"), name="gla_mixer")(
            zg, zgc, wa, b_a.reshape(2, 1, kw), norm_g.reshape(1, vw))


def _conv_kernel(u_ref, dw_ref, dwb_ref, lng_ref, lnb_ref, pw_ref, pwb_ref, o_ref, h_scr, *, seq, rows):
    halo = CONV_HALO
    h_scr[0:halo, :] = jnp.zeros((halo, CONV_C), F32)
    h_scr[halo + seq:2 * halo + seq, :] = jnp.zeros((halo, CONV_C), F32)

    @pl.loop(0, seq // rows)
    def _(t):
        s = pl.multiple_of(t * rows, rows)
        u = u_ref[0, pl.ds(s, rows), :]
        h_scr[pl.ds(pl.multiple_of(halo + s, 8), rows), :] = u[:, :CONV_C] * jax.nn.sigmoid(u[:, CONV_C:])

    @pl.loop(0, seq // rows)
    def _(t):
        s = pl.multiple_of(t * rows, rows)
        win = h_scr[pl.ds(s, rows + 2 * halo), :]
        first = halo - CONV_K // 2
        acc = jnp.broadcast_to(dwb_ref[...], (rows, CONV_C))
        for r in range(8):
            taps = [k for k in range(CONV_K) if (first + k) % 8 == r]
            shifted = win if r == 0 else pltpu.roll(win, win.shape[0] - r, axis=0)
            for k in taps:
                at = 8 * ((first + k) // 8)
                acc = acc + shifted[at:at + rows] * dw_ref[k:k + 1, :]
        mu = jnp.mean(acc, axis=-1, keepdims=True)
        var = jnp.mean(jnp.square(acc - mu), axis=-1, keepdims=True)
        y = _silu((acc - mu) * lax.rsqrt(var + EPS) * lng_ref[...] + lnb_ref[...])
        o_ref[0, pl.ds(s, rows), :] = (_dot(y.astype(BF16), pw_ref[...]) + pwb_ref[...]).astype(BF16)


def conformer_conv(u, dw_w, dw_b, ln_g, ln_b, pw_w, pw_b):
    bsz, seq, _ = u.shape
    rows = _tile(seq, 128)
    full = lambda shape: pl.BlockSpec(shape, lambda b: (0,) * len(shape))
    row = lambda t: t.reshape(1, CONV_C)
    return pl.pallas_call(
        functools.partial(_conv_kernel, seq=seq, rows=rows), grid=(bsz,),
        in_specs=[pl.BlockSpec((1, seq, 2 * CONV_C), lambda b: (b, 0, 0)),
                  full((CONV_K + 1, CONV_C)), full((1, CONV_C)), full((1, CONV_C)), full((1, CONV_C)),
                  full((CONV_C, CONV_C)), full((1, CONV_C))],
        out_specs=pl.BlockSpec((1, seq, CONV_C), lambda b: (b, 0, 0)),
        out_shape=jax.ShapeDtypeStruct((bsz, seq, CONV_C), BF16),
        scratch_shapes=[pltpu.VMEM((seq + 2 * CONV_HALO, CONV_C), F32)],
        compiler_params=_params("parallel"), name="conformer_conv")(
            u, jnp.pad(dw_w, ((0, 1), (0, 0))), row(dw_b), row(ln_g), row(ln_b), pw_w.astype(BF16), row(pw_b))


def s5_matrices(lam_re, lam_im, log_dt, b_re, b_im, c_re, c_im):
    dt = jnp.exp(log_dt)[:, None]
    mag = jnp.exp(lam_re * dt)
    a_re, a_im = mag * jnp.cos(lam_im * dt), mag * jnp.sin(lam_im * dt)
    den = lam_re * lam_re + lam_im * lam_im
    f_re = ((a_re - 1.0) * lam_re + a_im * lam_im) / den
    f_im = (a_im * lam_re - (a_re - 1.0) * lam_im) / den
    bb_re = f_re[..., None] * b_re - f_im[..., None] * b_im
    bb_im = f_re[..., None] * b_im + f_im[..., None] * b_re
    eye = jnp.eye(S5_NG, dtype=F32)
    blk_b = lambda t: jnp.einsum("gph,gk->ghkp", t, eye).reshape(S5_C, S5_STATE)
    blk_c = lambda t: jnp.einsum("ghp,gk->gpkh", t, eye).reshape(S5_STATE, S5_C)
    a = jnp.stack([a_re.reshape(S5_STATE), a_im.reshape(S5_STATE)])
    b_mat = jnp.concatenate([blk_b(bb_re), blk_b(bb_im)], axis=1).astype(BF16)
    c_mat = jnp.concatenate([blk_c(c_re), -blk_c(c_im)], axis=0).astype(BF16)
    return a, b_mat, c_mat


def _s5_scan_kernel(uf_ref, ub_ref, a_ref, b_ref, c_ref, h0_ref, yf_ref, yb_ref, hl_ref, hs_f, hs_b, st_scr,
                    *, steps, bsz):
    ns = S5_STATE

    @pl.when(pl.program_id(0) == 0)
    def _():
        st_scr[...] = h0_ref[...]

    for d, (u_ref, hs) in enumerate(((uf_ref, hs_f), (ub_ref, hs_b))):
        u = u_ref[...].reshape(steps * bsz, S5_C).astype(BF16)
        hs[...] = _dot(u, b_ref[d])
    for d, hs in enumerate((hs_f, hs_b)):
        for s0 in range(0, ns, S5_STRIP):
            re, im = slice(s0, s0 + S5_STRIP), slice(ns + s0, ns + s0 + S5_STRIP)
            a_re, a_im = a_ref[d, 0:1, re], a_ref[d, 1:2, re]
            h_re, h_im = st_scr[d, :, re], st_scr[d, :, im]
            for j in range(steps):
                rows = pl.ds((steps - 1 - j if d == 1 else j) * bsz, bsz)
                h_re, h_im = (a_re * h_re - a_im * h_im + hs[rows, re], a_re * h_im + a_im * h_re + hs[rows, im])
                hs[rows, re] = h_re
                hs[rows, im] = h_im
            st_scr[d, :, re] = h_re
            st_scr[d, :, im] = h_im
    yf_ref[...] = _dot(hs_f[...].astype(BF16), c_ref[0]).reshape(steps, bsz, S5_C)
    yb_ref[...] = _dot(hs_b[...].astype(BF16), c_ref[1]).reshape(steps, bsz, S5_C)
    hl_ref[...] = st_scr[...]


def s5_scan(u_tm, a, b_mat, c_mat, h0):
    seq, bsz, _ = u_tm.shape
    steps = _tile(seq, 32)
    n = seq // steps
    full = lambda t: pl.BlockSpec(t.shape, lambda i: (0,) * t.ndim)
    fwd = pl.BlockSpec((steps, bsz, S5_C), lambda i: (i, 0, 0))
    bwd = pl.BlockSpec((steps, bsz, S5_C), lambda i: (n - 1 - i, 0, 0))
    y = jax.ShapeDtypeStruct((seq, bsz, S5_C), F32)
    return pl.pallas_call(
        functools.partial(_s5_scan_kernel, steps=steps, bsz=bsz), grid=(n,),
        in_specs=[fwd, bwd, full(a), full(b_mat), full(c_mat), full(h0)],
        out_specs=[fwd, bwd, full(h0)],
        out_shape=[y, y, jax.ShapeDtypeStruct(h0.shape, F32)],
        scratch_shapes=[pltpu.VMEM((steps * bsz, 2 * S5_STATE), F32), pltpu.VMEM((steps * bsz, 2 * S5_STATE), F32),
                        pltpu.VMEM(h0.shape, F32)],
        compiler_params=_params("arbitrary"), name="s5_scan")(u_tm, u_tm, a, b_mat, c_mat, h0)


def _s5_out_kernel(u_ref, yf_ref, yb_ref, d_ref, w_ref, b_ref, o_ref):
    y = d_ref[...] * u_ref[...] + yf_ref[...] + yb_ref[...]
    z = _dot(y.astype(BF16), w_ref[...]) + b_ref[...]
    o_ref[...] = (z[:, :S5_C] * jax.nn.sigmoid(z[:, S5_C:])).astype(BF16)


def s5_output(u, y_f, y_b, d_skip, glu_w, glu_b):
    n = u.shape[0]
    tm = _tile(n, 1024)
    rows = pl.BlockSpec((tm, S5_C), lambda i: (i, 0))
    full = lambda shape: pl.BlockSpec(shape, lambda i: (0,) * len(shape))
    return pl.pallas_call(
        _s5_out_kernel, grid=(n // tm,),
        in_specs=[rows, rows, rows, full((1, S5_C)), full((S5_C, 2 * S5_C)), full((1, 2 * S5_C))],
        out_specs=rows, out_shape=jax.ShapeDtypeStruct((n, S5_C), BF16),
        compiler_params=_params("parallel"), name="s5_output")(
            u, y_f, y_b, d_skip.reshape(1, S5_C), glu_w.astype(BF16), glu_b.reshape(1, 2 * S5_C))


def s5_mixer(zs, zsc, lam_re, lam_im, log_dt, b_re, b_im, c_re, c_im, d_skip, glu_w, glu_b):
    bsz, seq, _ = zs.shape
    ctx_len = zsc.shape[1]
    u = jnp.transpose(zs, (1, 0, 2))
    uc = jnp.transpose(zsc, (1, 0, 2))
    mats = [s5_matrices(lam_re[d], lam_im[d], log_dt[d], b_re[d], b_im[d], c_re[d], c_im[d]) for d in range(2)]
    a, b_mat, c_mat = [jnp.stack(t) for t in zip(*mats)]
    yc_f, yc_b, hc = s5_scan(uc, a, b_mat, c_mat, jnp.zeros((2, bsz, 2 * S5_STATE), F32))
    y_f, y_b, _ = s5_scan(u, a, b_mat, c_mat, hc)
    ys, ycs = (y_f, y_b), (yc_f, yc_b)
    flat = lambda t: t.reshape(-1, S5_C)
    o = s5_output(flat(u), flat(ys[0]), flat(ys[1]), d_skip, glu_w, glu_b).reshape(seq, bsz, S5_C)
    oc = s5_output(flat(uc), flat(ycs[0]), flat(ycs[1]), d_skip, glu_w, glu_b).reshape(ctx_len, bsz, S5_C)
    return jnp.transpose(o, (1, 0, 2)), jnp.transpose(oc, (1, 0, 2))


def rope_tables(seq_len, rotate):
    if not rotate:
        return jnp.ones((seq_len, MLA_ROPE), F32), jnp.zeros((seq_len, MLA_ROPE), F32)
    rows = seq_len // GRID_W
    row = jnp.broadcast_to(jnp.arange(rows, dtype=F32)[:, None], (rows, GRID_W)).reshape(seq_len)
    col = jnp.broadcast_to(jnp.arange(GRID_W, dtype=F32)[None, :], (rows, GRID_W)).reshape(seq_len)
    inv_freq = ROPE_BASE ** (-jnp.arange(ROPE_AXIS // 2, dtype=F32) / (ROPE_AXIS // 2))
    ang = jnp.stack([row[:, None] * inv_freq, col[:, None] * inv_freq], axis=1)
    full = lambda t: jnp.broadcast_to(t[:, :, None, :], (seq_len, 2, 2, ROPE_AXIS // 2)).reshape(seq_len, MLA_ROPE)
    return full(jnp.cos(ang)), full(jnp.sin(ang))


def mla_tables(seq_len, rotate):
    cos, sin = rope_tables(seq_len, rotate)
    head = lambda rope, fill: jnp.concatenate(
        [jnp.full((seq_len, MLA_NOPE), fill, F32), rope, jnp.zeros((seq_len, MLA_HEAD_PAD - MLA_NOPE - MLA_ROPE), F32)], 1)
    cos_q = jnp.tile(head(cos, 1.0), (1, MLA_HEADS))
    sin_q = jnp.tile(head(sin, 0.0), (1, MLA_HEADS))
    cs_k = jnp.concatenate([cos, sin, jnp.zeros((seq_len, 128 - 2 * MLA_ROPE), F32)], 1)
    return cos_q, sin_q, cs_k


def mla_weights(w_uq, w_ukv):
    hd = MLA_NOPE + MLA_ROPE
    zq = jnp.zeros((MLA_Q_LORA, MLA_HEAD_PAD - hd), F32)
    zn = jnp.zeros((MLA_Q_LORA, MLA_NOPE), F32)
    wq, wq_sw, wk, wv = [], [], [], []
    for h in range(MLA_HEADS):
        qh = w_uq[:, h * hd:(h + 1) * hd]
        wq += [qh, zq]
        wq_sw += [zn, _rope_swap(qh[:, MLA_NOPE:]), zq]
        kvh = w_ukv[:, h * (MLA_NOPE + MLA_V):(h + 1) * (MLA_NOPE + MLA_V)]
        wk += [kvh[:, :MLA_NOPE], jnp.zeros((MLA_KV_LORA, MLA_HEAD_PAD - MLA_NOPE), F32)]
        wv += [kvh[:, MLA_NOPE:]]
    place = np.zeros((128, MLA_HEADS * MLA_HEAD_PAD), np.float32)
    for h in range(MLA_HEADS):
        for j in range(MLA_ROPE):
            place[j, h * MLA_HEAD_PAD + MLA_NOPE + j] = 1.0
            place[MLA_ROPE + j, h * MLA_HEAD_PAD + MLA_NOPE + j] = 1.0
    cat = lambda ts: jnp.concatenate(ts, axis=1).astype(BF16)
    return cat(wq), cat(wq_sw), cat(wk), cat(wv), jnp.asarray(place).astype(BF16)


def _mla_prep_kernel(z_ref, qg_ref, kg_ref, wq_ref, wqs_ref, wk_ref, wv_ref, pl_ref, cq_ref, sq_ref, csk_ref,
                     q_ref, k_ref, v_ref):
    z = z_ref[0]
    norm = lambda t, g: (t * lax.rsqrt(jnp.mean(t * t, axis=-1, keepdims=True) + EPS) * g).astype(BF16)
    cq = norm(z[:, :MLA_Q_LORA], qg_ref[...])
    ckv = norm(z[:, MLA_Q_LORA:MLA_Q_LORA + MLA_KV_LORA], kg_ref[...])
    scale = (MLA_NOPE + MLA_ROPE) ** -0.5
    q = _dot(cq, wq_ref[...]) * cq_ref[...] + _dot(cq, wqs_ref[...]) * sq_ref[...]
    q_ref[0] = (q * scale).astype(BF16)
    kr = z[:, MLA_Q_LORA + MLA_KV_LORA:] * csk_ref[...]
    k_ref[0] = (_dot(ckv, wk_ref[...]) + _dot_exact_rhs(kr, pl_ref[...])).astype(BF16)
    v_ref[0] = _dot(ckv, wv_ref[...]).astype(BF16)


def mla_prep(zm, qn_g, kvn_g, weights, tables):
    bsz, seq, wd = zm.shape
    tm = _tile(seq, 512)
    wq, wq_sw, wk, wv, place = weights
    cos_q, sin_q, cs_k = tables
    qw, vw = MLA_HEADS * MLA_HEAD_PAD, MLA_HEADS * MLA_V
    full = lambda t: pl.BlockSpec(t.shape, lambda b, i: (0,) * t.ndim)
    pos = lambda t: pl.BlockSpec((tm, t.shape[1]), lambda b, i: (i, 0))
    out = lambda w: pl.BlockSpec((1, tm, w), lambda b, i: (b, i, 0))
    qg, kg = qn_g.reshape(1, -1), kvn_g.reshape(1, -1)
    return pl.pallas_call(
        _mla_prep_kernel, grid=(bsz, seq // tm),
        in_specs=[pl.BlockSpec((1, tm, wd), lambda b, i: (b, i, 0)), full(qg), full(kg), full(wq), full(wq_sw),
                  full(wk), full(wv), full(place), pos(cos_q), pos(sin_q), pos(cs_k)],
        out_specs=[out(qw), out(qw), out(vw)],
        out_shape=[jax.ShapeDtypeStruct((bsz, seq, qw), BF16), jax.ShapeDtypeStruct((bsz, seq, qw), BF16),
                   jax.ShapeDtypeStruct((bsz, seq, vw), BF16)],
        compiler_params=_params("parallel", "parallel"), name="mla_prep")(
            zm, qg, kg, wq, wq_sw, wk, wv, place, cos_q, sin_q, cs_k)


def _mla_attn_kernel(q_ref, *refs, n_seg):
    k_refs, v_refs, o_ref = refs[:n_seg], refs[n_seg:2 * n_seg], refs[2 * n_seg]
    outs = []
    for h in range(MLA_HEADS):
        q = q_ref[0, :, h * MLA_HEAD_PAD:(h + 1) * MLA_HEAD_PAD]
        s = [lax.dot_general(q, k_ref[0, :, h * MLA_HEAD_PAD:(h + 1) * MLA_HEAD_PAD], NT_DIMS,
                             preferred_element_type=F32) for k_ref in k_refs]
        m = functools.reduce(jnp.maximum, [jnp.max(t, axis=-1, keepdims=True) for t in s])
        p = [jnp.exp(t - m) for t in s]
        den = functools.reduce(jnp.add, [jnp.sum(t, axis=-1, keepdims=True) for t in p])
        o = functools.reduce(jnp.add, [_dot(t.astype(BF16), v_ref[0, :, h * MLA_V:(h + 1) * MLA_V])
                                       for t, v_ref in zip(p, v_refs)])
        outs.append(o / den)
    o_ref[0] = jnp.concatenate(outs, axis=-1).astype(BF16)


def mla_attention(q, ks, vs):
    bsz, seq, qw = q.shape
    tq = _tile(seq, 256)
    vw = MLA_HEADS * MLA_V
    seg = lambda t: pl.BlockSpec((1,) + t.shape[1:], lambda b, i: (b, 0, 0))
    return pl.pallas_call(
        functools.partial(_mla_attn_kernel, n_seg=len(ks)), grid=(bsz, seq // tq),
        in_specs=[pl.BlockSpec((1, tq, qw), lambda b, i: (b, i, 0))] + [seg(t) for t in ks] + [seg(t) for t in vs],
        out_specs=pl.BlockSpec((1, tq, vw), lambda b, i: (b, i, 0)),
        out_shape=jax.ShapeDtypeStruct((bsz, seq, vw), BF16),
        compiler_params=_params("parallel", "parallel"), name="mla_attention")(q, *ks, *vs)


def _outproj_kernel(a_ref, b_ref, c_ref, d_ref, w_ref, x_ref, g1_ref, sh_ref, sc_ref, g_ref, rw_ref,
                    xo_ref, h_ref, lg_ref):
    mix = None
    for j, o_ref in enumerate((a_ref, b_ref, c_ref, d_ref)):
        wd = o_ref.shape[-1]
        part = _dot(o_ref[0], w_ref[j * wd:(j + 1) * wd, :])
        mix = part if mix is None else mix + part
    x = x_ref[0] + g1_ref[0] * mix
    xo_ref[0] = x
    ms = jnp.mean(x * x, axis=-1, keepdims=True)
    h = x * lax.rsqrt(ms + EPS) * g_ref[...] * (1.0 + sc_ref[0]) + sh_ref[0]
    _store_row_slabs(h_ref, _pack_bf16_pairs(h))
    lg_ref[...] =lax.dot_general(rw_ref[...], h, NT_DIMS, precision=HIGHEST, preferred_element_type=F32)


def out_projection(parts, w_out, x, g1, shift, scale, g, router_w):
    bsz, seq, dm = x.shape
    tm = _tile(seq, 512)
    nt = seq // tm
    ne = router_w.shape[1]
    groups = dm // 2 // 128
    rows = lambda w: pl.BlockSpec((1, tm, w), lambda b, i: (b, i, 0))
    vec = pl.BlockSpec((1, 1, dm), lambda b, i: (b, 0, 0))
    full = lambda shape: pl.BlockSpec(shape, lambda b, i: (0,) * len(shape))
    return pl.pallas_call(
        _outproj_kernel, grid=(bsz, nt),
        in_specs=[rows(p.shape[-1]) for p in parts] + [full(w_out.shape), rows(dm), vec, vec, vec, full((1, dm)),
                                                       full((ne, dm))],
        out_specs=[rows(dm), pl.BlockSpec((tm * groups, 128), lambda b, i: (b * nt + i, 0)),
                   pl.BlockSpec((ne, tm), lambda b, i: (0, b * nt + i))],
        out_shape=[jax.ShapeDtypeStruct((bsz, seq, dm), F32), jax.ShapeDtypeStruct((bsz * seq * groups, 128), jnp.int32),
                   jax.ShapeDtypeStruct((ne, bsz * seq), F32)],
        compiler_params=_params("parallel", "parallel"), name="out_projection")(
            *parts, w_out.astype(BF16), x, g1, shift, scale, g.reshape(1, dm), router_w.T)


def _route_kernel(lg_ref, b_ref, e_ref, w_ref):
    ne, tt = lg_ref.shape
    per = ne // N_GROUPS
    neg = -jnp.inf
    scores = jax.nn.sigmoid(lg_ref[...])
    biased = scores + b_ref[...]
    v3 = biased.reshape(N_GROUPS, per, tt)
    e_in = lax.broadcasted_iota(jnp.int32, v3.shape, 1).astype(F32)
    m1 = jnp.max(v3, axis=1, keepdims=True)
    i1 = jnp.min(jnp.where(v3 == m1, e_in, float(per)), axis=1, keepdims=True)
    m2 = jnp.max(jnp.where(e_in == i1, neg, v3), axis=1, keepdims=True)
    grp = (m1 + m2).reshape(N_GROUPS, tt)

    def pick(cur, count):
        ids = lax.broadcasted_iota(jnp.int32, cur.shape, 0).astype(F32)
        marks = jnp.zeros(cur.shape, F32)
        picked = []
        for _ in range(count):
            m = jnp.max(cur, axis=0, keepdims=True)
            first = jnp.min(jnp.where(cur == m, ids, float(cur.shape[0])), axis=0, keepdims=True)
            hit = ids == first
            marks = jnp.where(hit, 1.0, marks)
            cur = jnp.where(hit, neg, cur)
            picked.append(first)
        return marks, picked

    grp_on, _ = pick(grp, TOPK_GROUPS)
    exp_on = jnp.broadcast_to(grp_on.reshape(N_GROUPS, 1, tt), v3.shape).reshape(ne, tt)
    chosen, picked = pick(jnp.where(exp_on > 0.0, biased, neg), TOP_K)
    w = scores * chosen
    gate = w / jnp.sum(w, axis=0, keepdims=True) * ROUTED_SCALE
    ids = lax.broadcasted_iota(jnp.int32, gate.shape, 0).astype(F32)
    e_ref[...] = jnp.concatenate(picked, axis=0).astype(jnp.int32)
    w_ref[...] = jnp.concatenate([jnp.sum(jnp.where(ids == p, gate, 0.0), axis=0, keepdims=True) for p in picked], axis=0)


def route(logits_t, router_b):
    ne, n = logits_t.shape
    tt = _tile(n, 1024)
    out = pl.BlockSpec((TOP_K, tt), lambda i: (0, i))
    return pl.pallas_call(
        _route_kernel, grid=(n // tt,),
        in_specs=[pl.BlockSpec((ne, tt), lambda i: (0, i)), pl.BlockSpec((ne, 1), lambda i: (0, 0))],
        out_specs=[out, out],
        out_shape=[jax.ShapeDtypeStruct((TOP_K, n), jnp.int32), jax.ShapeDtypeStruct((TOP_K, n), F32)],
        compiler_params=_params("parallel"), name="route")(logits_t, router_b.reshape(ne, 1))


def moe_plan(eid, wgt, chunk, xg, yg):
    n = eid.shape[1]
    nc, na = n // chunk, chunk * TOP_K
    ns = na // MOE_ROWS + N_EXPERTS
    key = eid.T.reshape(nc, na) * na + jnp.arange(na, dtype=jnp.int32)
    skey, sw = lax.sort((key, wgt.T.reshape(nc, na)), dimension=1, num_keys=1)
    e_sorted, tok = skey // na, (skey % na) // TOP_K
    experts = jnp.arange(N_EXPERTS, dtype=jnp.int32)
    ends = jnp.sum(e_sorted[:, None, :] <= experts[None, :, None], axis=-1, dtype=jnp.int32)
    starts = jnp.concatenate([jnp.zeros((nc, 1), jnp.int32), ends[:, :-1]], axis=1)
    pad_end = jnp.cumsum(-(-(ends - starts) // MOE_ROWS) * MOE_ROWS, axis=1)
    pad_start = jnp.concatenate([jnp.zeros((nc, 1), jnp.int32), pad_end[:, :-1]], axis=1)
    first = jnp.arange(ns, dtype=jnp.int32) * MOE_ROWS
    exp = jnp.minimum(jnp.sum(pad_end[:, None, :] <= first[None, :, None], axis=-1, dtype=jnp.int32), N_EXPERTS - 1)
    active = first[None, :] < pad_end[:, -1:]
    take = lambda t, i: jnp.take_along_axis(t, i, axis=1)
    rank = first[None, :, None] + jnp.arange(MOE_ROWS, dtype=jnp.int32) - take(pad_start, exp)[..., None]
    real = (rank < take(ends - starts, exp)[..., None]) & active[..., None]
    src = jnp.clip(take(starts, exp)[..., None] + rank, 0, na - 1).reshape(nc, ns * MOE_ROWS)
    rows = lambda t: take(t, src).reshape(nc, ns, MOE_ROWS)
    flat = lambda t: t.reshape(nc * ns, 1, MOE_ROWS)
    gather_at = flat(jnp.where(real, rows(tok), 0) * xg)
    scatter_at = flat(jnp.where(real, rows(tok), chunk) * yg)
    weight = flat(jnp.where(real, rows(sw), 0.0))
    return gather_at, scatter_at, weight, exp.reshape(nc * ns), active.astype(jnp.int32).reshape(nc * ns)


def _moe_tick(x_ref, y_ref, idx_ref, w_ref, tick, wgu_ref, wd_ref, gather_to, mm_from, mm_to, scatter_from):
    bm = MOE_ROWS
    stride = bm + 1
    xg = mm_from.shape[0] // stride
    yg = mm_to.shape[0] // stride
    for mi in range(bm):
        src = pl.multiple_of(idx_ref[0, 0, 2 * tick * bm + mi], xg)
        gather_to[pl.ds(mi, xg, stride=stride), :] = x_ref[0, pl.ds(src, xg), :]
    low, high = _unpack_bf16_pairs([mm_from[j * stride:j * stride + bm, :] for j in range(xg)])
    half = low.shape[1]
    ff = wd_ref.shape[1]
    up = _dot(low, wgu_ref[0, :half, :]) + _dot(high, wgu_ref[0, half:, :])
    act = (_silu(up[:, :ff]) * up[:, ff:]).astype(BF16)
    out = _dot(act, wd_ref[0])
    for j in range(yg):
        mm_to[j * stride:j * stride + bm, :] = out[:, j * 128:(j + 1) * 128]
    for base in range(0, bm, MOE_BATCH):
        at = [pl.multiple_of(idx_ref[0, 0, (2 * tick + 1) * bm + mi], yg) for mi in range(base, base + MOE_BATCH)]
        new = [y_ref[0, pl.ds(i, yg), :] + w_ref[0, 0, tick * bm + mi] * scatter_from[pl.ds(mi, yg, stride=stride), :]
               for i, mi in zip(at, range(base, base + MOE_BATCH))]
        for i, v in zip(at, new):
            y_ref[0, pl.ds(i, yg), :] = v


def _moe_routed_kernel(exp_ref, act_ref, x_ref, idx_ref, w_ref, wgua_ref, wda_ref, wgub_ref, wdb_ref,
                       y_ref, xt0, xt1, ot0, ot1, *, ns):
    c, g = pl.program_id(0), pl.program_id(1)

    @pl.when(g == 0)
    def _():
        y_ref[...] = jnp.zeros_like(y_ref)

    @pl.when((c == 0) & (g == 0))
    def _():
        xt1[...] = jnp.zeros_like(xt1)
        ot0[...] = jnp.zeros_like(ot0)
        ot1[...] = jnp.zeros_like(ot1)

    def live(step):
        return (step >= 0) & (step < ns) & (act_ref[c * ns + jnp.clip(step, 0, ns - 1)] != 0)

    t = 2 * g

    @pl.when(live(t - 2) | live(t - 1) | live(t) | live(t + 1))
    def _():
        _moe_tick(x_ref, y_ref, idx_ref, w_ref, 0, wgua_ref, wda_ref,
                  gather_to=xt0, mm_from=xt1, mm_to=ot1, scatter_from=ot0)
        _moe_tick(x_ref, y_ref, idx_ref, w_ref, 1, wgub_ref, wdb_ref,
                  gather_to=xt1, mm_from=xt0, mm_to=ot0, scatter_from=ot1)


def moe_routed(slabs, eid, wgt, wgu, wd):
    n = eid.shape[1]
    xg = slabs.shape[0] // n
    yg = 2 * xg
    dm, ff = wd.shape[2], wd.shape[1]
    chunk = _tile(n, MOE_CHUNK)
    nc = n // chunk
    ns = chunk * TOP_K // MOE_ROWS + N_EXPERTS
    assert ns % 2 == 0
    steps = (ns + 2) // 2
    last = ns - 1
    gather_at, scatter_at, weight, exp, active = moe_plan(eid, wgt, chunk, xg, yg)
    tick = jnp.arange(steps, dtype=jnp.int32)[:, None] * 2 + jnp.arange(2, dtype=jnp.int32)[None, :]
    gathered = jnp.minimum(tick, last)
    scattered = jnp.where(tick >= 2, tick - 2, last)
    per_step = lambda t, at: jnp.take(t.reshape(nc, ns, MOE_ROWS), at, axis=1)
    idx = jnp.concatenate([per_step(gather_at, gathered[:, 0]), per_step(scatter_at, scattered[:, 0]),
                           per_step(gather_at, gathered[:, 1]), per_step(scatter_at, scattered[:, 1])], axis=-1)
    wts = jnp.concatenate([per_step(weight, scattered[:, 0]), per_step(weight, scattered[:, 1])], axis=-1)
    smem = lambda width: pl.BlockSpec((1, 1, width), lambda c, g, e, a: (c * steps + g, 0, 0), memory_space=pltpu.SMEM)
    computed = lambda tk: lambda g: jnp.clip(2 * g + tk - 1, 0, last)
    expert = lambda shape, step: pl.BlockSpec((1,) + shape, lambda c, g, e, a: (e[c * ns + step(g)], 0, 0))
    once = pl.Buffered(1)
    stage = lambda groups, dtype: pltpu.VMEM((-(-groups * (MOE_ROWS + 1) // 8) * 8, 128), dtype)
    return pl.pallas_call(
        functools.partial(_moe_routed_kernel, ns=ns),
        grid_spec=pltpu.PrefetchScalarGridSpec(
            num_scalar_prefetch=2, grid=(nc, steps),
            in_specs=[pl.BlockSpec((1, chunk * xg, 128), lambda c, g, e, a: (c, 0, 0), pipeline_mode=once),
                      smem(4 * MOE_ROWS), smem(2 * MOE_ROWS),
                      expert((dm, 2 * ff), computed(0)), expert((ff, dm), computed(0)),
                      expert((dm, 2 * ff), computed(1)), expert((ff, dm), computed(1))],
            out_specs=pl.BlockSpec((1, chunk * yg + 8, 128), lambda c, g, e, a: (c, 0, 0), pipeline_mode=once),
            scratch_shapes=[stage(xg, jnp.int32), stage(xg, jnp.int32), stage(yg, F32), stage(yg, F32)]),
        out_shape=jax.ShapeDtypeStruct((nc, chunk * yg + 8, 128), F32),
        compiler_params=_params("arbitrary", "arbitrary"), name="moe_routed")(
            exp, active, slabs.reshape(nc, chunk * xg, 128), idx.reshape(nc * steps, 1, 4 * MOE_ROWS),
            wts.reshape(nc * steps, 1, 2 * MOE_ROWS), wgu, wd, wgu, wd)


def _moe_finish_kernel(y_ref, h_ref, x_ref, g2_ref, wg_ref, wu_ref, wd_ref, fg_ref, o_ref, *, final_norm):
    tm = x_ref.shape[0]
    low, high = _unpack_bf16_pairs(_load_row_slabs(h_ref, tm, h_ref.shape[0] // tm))
    routed = jnp.concatenate(_load_row_slabs(y_ref, tm, y_ref.shape[1] // tm, lead=(0,)), axis=1)
    half = low.shape[1]
    up = lambda w_ref: _dot(low, w_ref[:half, :]) + _dot(high, w_ref[half:, :])
    shared = _dot((_silu(up(wg_ref)) * up(wu_ref)).astype(BF16), wd_ref[...])
    x = x_ref[...] + g2_ref[0] * (routed + shared)
    if final_norm:
        x = x * lax.rsqrt(jnp.mean(x * x, axis=-1, keepdims=True) + EPS) * fg_ref[...]
    o_ref[...] = x


def moe_finish(y, slabs, x, g2, sg, su, sd, final_g, final_norm):
    bsz, seq, dm = x.shape
    n = bsz * seq
    tm = _tile(seq, 512)
    nt = seq // tm
    xg, yg = slabs.shape[0] // n, dm // 128
    per_chunk = (y.shape[1] - 8) // yg // tm
    rows = lambda w: pl.BlockSpec((tm, w), lambda i: (i, 0))
    full = lambda t: pl.BlockSpec(t.shape, lambda i: (0,) * t.ndim)
    sg, su, sd, fg = sg.astype(BF16), su.astype(BF16), sd.astype(BF16), final_g.reshape(1, dm)
    out = pl.pallas_call(
        functools.partial(_moe_finish_kernel, final_norm=final_norm), grid=(n // tm,),
        in_specs=[pl.BlockSpec((1, tm * yg, 128), lambda i: (i // per_chunk, i % per_chunk, 0)),
                  pl.BlockSpec((tm * xg, 128), lambda i: (i, 0)), rows(dm),
                  pl.BlockSpec((1, 1, dm), lambda i: (i // nt, 0, 0)), full(sg), full(su), full(sd), full(fg)],
        out_specs=rows(dm), out_shape=jax.ShapeDtypeStruct((n, dm), F32),
        compiler_params=_params("parallel"), name="moe_finish")(y, slabs, x.reshape(n, dm), g2, sg, su, sd, fg)
    return out.reshape(bsz, seq, dm)


def kernel(x, c, ctx, c_ctx, ada_w, ada_b, norm1_g, norm2_g, w_in, w_out, gla_w_a2, gla_b_a, gla_norm_g,
           conv_dw_w, conv_dw_b, conv_ln_g, conv_ln_b, conv_pw_w, conv_pw_b,
           s5_lam_re, s5_lam_im, s5_log_dt, s5_b_re, s5_b_im, s5_c_re, s5_c_im, s5_d, s5_glu_w, s5_glu_b,
           mla_qn_g, mla_kvn_g, mla_w_uq, mla_w_ukv,
           moe_router_w, moe_router_b, moe_w_gate, moe_w_up, moe_w_down,
           shared_w_gate, shared_w_up, shared_w_down, final_g):
    bsz, seq, dm = x.shape
    ctx_len = ctx.shape[1]
    depth = ada_w.shape[0]
    cc = jnp.concatenate([c, c_ctx[None], jnp.zeros((-(bsz + 1) % 8, dm), F32)], axis=0)
    tables = mla_tables(seq, True)
    tables_c = mla_tables(ctx_len, False)
    xc = ctx
    for l in range(depth):
        last = l == depth - 1
        mod = ada_mod(cc, ada_w[l], ada_b[l])
        lat = [t.reshape(bsz, 1, dm) for t in jnp.split(mod[:bsz], 6, axis=-1)]
        con = [jnp.broadcast_to(t.reshape(1, 1, dm), (bsz, 1, dm)) for t in jnp.split(mod[bsz], 6)]
        sh1, sc1, g1, sh2, sc2, g2 = lat
        sh1c, sc1c, g1c, sh2c, sc2c, g2c = con
        w_in_l = assemble_w_in(w_in[l])
        zg, zv, zs, zm = in_projection(x, sh1, sc1, norm1_g[l], w_in_l)
        zgc, zvc, zsc, zmc = in_projection(xc, sh1c, sc1c, norm1_g[l], w_in_l)
        conv_p = (conv_dw_w[l], conv_dw_b[l], conv_ln_g[l], conv_ln_b[l], conv_pw_w[l], conv_pw_b[l])
        o_gla, oc_gla = gla_mixer(zg, zgc, gla_w_a2[l], gla_b_a[l], gla_norm_g[l])
        o_conv = conformer_conv(zv, *conv_p)
        o_s5, oc_s5 = s5_mixer(zs, zsc, s5_lam_re[l], s5_lam_im[l], s5_log_dt[l], s5_b_re[l], s5_b_im[l],
                               s5_c_re[l], s5_c_im[l], s5_d[l], s5_glu_w[l], s5_glu_b[l])
        mla_w = mla_weights(mla_w_uq[l], mla_w_ukv[l])
        q, k, v = mla_prep(zm, mla_qn_g[l], mla_kvn_g[l], mla_w, tables)
        qc, kc, vc = mla_prep(zmc, mla_qn_g[l], mla_kvn_g[l], mla_w, tables_c)
        o_mla = mla_attention(q, [k, kc], [v, vc])
        experts = (jnp.concatenate([moe_w_gate[l], moe_w_up[l]], axis=2).astype(BF16), moe_w_down[l].astype(BF16))
        shared = (shared_w_gate[l], shared_w_up[l], shared_w_down[l])

        def ffn(parts, x_in, gate1, shift, scale, gate2, final_norm):
            x_mid, slabs, logits = out_projection(parts, w_out[l], x_in, gate1, shift, scale, norm2_g[l], moe_router_w[l])
            y = moe_routed(slabs, *route(logits, moe_router_b[l]), *experts)
            return moe_finish(y, slabs, x_mid, gate2, *shared, final_g, final_norm)

        if not last:
            oc_conv = conformer_conv(zvc, *conv_p)
            oc_mla = mla_attention(qc, [kc], [vc])
            xc = ffn([oc_gla, oc_conv, oc_s5, oc_mla], xc, g1c, sh2c, sc2c, g2c, False)
        x = ffn([o_gla, o_conv, o_s5, o_mla], x, g1, sh2, sc2, g2, last)
    return x
```

```python
import functools
import math

import numpy as np
import jax
import jax.numpy as jnp
from jax import lax
from jax.experimental import pallas as pl
from jax.experimental.pallas import tpu as pltpu

F32 = jnp.float32
BF16 = jnp.bfloat16
HIGHEST = lax.Precision.HIGHEST
EPS = 1e-6

GRID_W = 64

GLA_HEADS = 4
GLA_DK = 32
GLA_DV = 64
GLA_LOWRANK = 16
GLA_TAU = 16.0
GLA_CHUNK = 64
GLA_UNROLL = 2
GLA_FINISH_ROWS = 256

CONV_C = 256
CONV_K = 31
CONV_HALO = 16

S5_C = 256
S5_GROUP = 16
S5_NG = 16
S5_P = 64
S5_STATE = S5_NG * S5_P
S5_STRIP = 256

MLA_HEADS = 4
MLA_NOPE = 64
MLA_ROPE = 32
MLA_V = 64
MLA_Q_LORA = 256
MLA_KV_LORA = 128
MLA_HEAD_PAD = 128
ROPE_AXIS = MLA_ROPE // 2
ROPE_BASE = 10000.0

N_EXPERTS = 64
TOP_K = 8
N_GROUPS = 8
TOPK_GROUPS = 4
ROUTED_SCALE = 2.5
MOE_CHUNK = 4096
MOE_ROWS = 256
MOE_BATCH = 8

IN_SPLITS = (128, 128, 256, 256, 16, 16, 512, 256, 256, 128, 32)
W_GLA, W_CONV, W_S5, W_MLA = 1024, 512, 256, 512

VMEM_LIMIT = 48 * 1024 * 1024

NT_DIMS = (((1,), (1,)), ((), ()))
TN_DIMS = (((0,), (0,)), ((), ()))


def _params(*sem, vmem=VMEM_LIMIT):
    return pltpu.CompilerParams(dimension_semantics=sem, vmem_limit_bytes=vmem)


def _silu(v):
    return v * jax.nn.sigmoid(v)


def _dot(a, b, **kw):
    return jnp.dot(a, b, preferred_element_type=F32, **kw)


def _split_bf16(x):
    hi = x.astype(BF16)
    rest = x - hi.astype(F32)
    mid = rest.astype(BF16)
    return hi, mid, (rest - mid.astype(F32)).astype(BF16)


def _dot_exact_lhs(m, x):
    n = x.shape[1]
    prod = _dot(m, jnp.concatenate(_split_bf16(x), axis=1))
    return prod[:, :n] + prod[:, n:2 * n] + prod[:, 2 * n:]


def _dot_exact_rhs(x, m):
    n = x.shape[0]
    prod = _dot(jnp.concatenate(_split_bf16(x), axis=0), m)
    return prod[:n] + prod[n:2 * n] + prod[2 * n:]


def _tile(n, pref):
    return pref if n % pref == 0 else n


def _pack_bf16_pairs(h):
    bits = pltpu.bitcast(h.astype(BF16).astype(F32), jnp.int32)
    w = h.shape[-1] // 2
    return lax.shift_right_logical(bits[:, :w], 16) | bits[:, w:]


def _store_row_slabs(ref, value):
    rows, g = value.shape[0], value.shape[1] // 128
    for j in range(g):
        ref[pl.ds(j, rows, stride=g), :] = value[:, j * 128:(j + 1) * 128]


def _load_row_slabs(ref, rows, g, lead=()):
    return [ref[lead + (pl.ds(j, rows, stride=g), slice(None))] for j in range(g)]


def _unpack_bf16_pairs(words):
    low = jnp.concatenate([pltpu.bitcast(v << 16, F32) for v in words], axis=1)
    high = jnp.concatenate([pltpu.bitcast(v & -65536, F32) for v in words], axis=1)
    return low.astype(BF16), high.astype(BF16)


def _ada_kernel(c_ref, w_ref, b_ref, o_ref):
    o_ref[...] = _dot(_silu(c_ref[...]), w_ref[...], precision=HIGHEST) + b_ref[...]


def ada_mod(cc, w, b):
    rows, dm = cc.shape
    n = w.shape[1]
    tn = _tile(n, 512)
    return pl.pallas_call(
        _ada_kernel, grid=(n // tn,),
        in_specs=[pl.BlockSpec((rows, dm), lambda j: (0, 0)),
                  pl.BlockSpec((dm, tn), lambda j: (0, j)),
                  pl.BlockSpec((1, tn), lambda j: (0, j))],
        out_specs=pl.BlockSpec((rows, tn), lambda j: (0, j)),
        out_shape=jax.ShapeDtypeStruct((rows, n), F32),
        compiler_params=_params("arbitrary"), name="ada_mod")(cc, w, b.reshape(1, n))


def _inproj_kernel(x_ref, sh_ref, sc_ref, g_ref, w_ref, *o_refs):
    x = x_ref[0]
    ms = jnp.mean(x * x, axis=-1, keepdims=True)
    h = (x * lax.rsqrt(ms + EPS) * g_ref[...] * (1.0 + sc_ref[0]) + sh_ref[0]).astype(BF16)
    off = 0
    for o_ref in o_refs:
        w = o_ref.shape[-1]
        o_ref[0] = _dot(h, w_ref[:, off:off + w])
        off += w


def in_projection(x, shift, scale, g, w):
    bsz, seq, dm = x.shape
    tm = _tile(seq, 512)
    widths = (W_GLA, W_CONV, W_S5, W_MLA)
    vec = pl.BlockSpec((1, 1, dm), lambda b, i: (b, 0, 0))
    return pl.pallas_call(
        _inproj_kernel, grid=(bsz, seq // tm),
        in_specs=[pl.BlockSpec((1, tm, dm), lambda b, i: (b, i, 0)), vec, vec,
                  pl.BlockSpec((1, dm), lambda b, i: (0, 0)),
                  pl.BlockSpec(w.shape, lambda b, i: (0, 0))],
        out_specs=[pl.BlockSpec((1, tm, wd), lambda b, i: (b, i, 0)) for wd in widths],
        out_shape=[jax.ShapeDtypeStruct((bsz, seq, wd), F32) for wd in widths],
        compiler_params=_params("parallel", "parallel"), name="in_projection")(x, shift, scale, g.reshape(1, dm), w)


def _rope_swap(t):
    s = t.reshape(t.shape[:-1] + (2, 2, ROPE_AXIS // 2))
    return jnp.stack([-s[..., 1, :], s[..., 0, :]], axis=-2).reshape(t.shape)


def assemble_w_in(w_in):
    edges = np.cumsum((0,) + IN_SPLITS)
    q, k, v, r, a_f, a_b, conv, s5, cq, ckv, kr = [w_in[:, edges[i]:edges[i + 1]] for i in range(len(IN_SPLITS))]
    pad = lambda t, w: jnp.pad(t, ((0, 0), (0, w - t.shape[1])))
    cols = [q, k, v, r, pad(a_f, 128), pad(a_b, 128), conv, s5, cq, ckv, pad(jnp.concatenate([kr, _rope_swap(kr)], 1), 128)]
    return jnp.concatenate(cols, axis=1).astype(BF16)


def _gla_kernel(z_ref, zc_ref, wa_ref, ba_ref, g_ref, o_ref, oc_ref, of_scr, ob_scr, dec_scr, st_scr, *, seq, ctx_len):
    c = GLA_CHUNK
    nh = GLA_HEADS
    kw = nh * GLA_DK
    vw = nh * GLA_DV
    iota = lambda shape, d: lax.broadcasted_iota(jnp.int32, shape, d)
    ra, ca = iota((nh * 2 * c, 2 * c), 0) & (2 * c - 1), iota((nh * 2 * c, 2 * c), 1)
    causal = ((ra < c) & (ca <= ra)) | ((ra >= c) & (ca >= ra))
    fwd_rows = iota((2 * c, 1), 0) < c
    k_of = lambda t: t >> int(math.log2(GLA_DK))
    v_of = lambda t: t >> int(math.log2(GLA_DV))
    k_head = [(k_of(iota((1, kw), 1)) == h).astype(F32) for h in range(nh)]
    v_head = [(v_of(iota((1, vw), 1)) == h).astype(F32) for h in range(nh)]
    st_rows = iota((2 * vw, kw), 0)
    st_mask = (v_of(st_rows & (vw - 1)) == k_of(iota((2 * vw, kw), 1))).astype(F32)
    st_fwd = st_rows < vw
    head_mean = ((v_of(iota((vw, vw), 0)) == v_of(iota((vw, vw), 1))).astype(F32) * (1.0 / GLA_DV)).astype(BF16)

    def decays(ref, n_rows, base):
        rows = GLA_FINISH_ROWS if n_rows % GLA_FINISH_ROWS == 0 else c
        rr, cc = iota((rows, rows), 0), iota((rows, rows), 1)
        same = (rr >> int(math.log2(c))) == (cc >> int(math.log2(c)))
        tri_f = (same & (cc <= rr)).astype(F32).astype(BF16)
        tri_b = (same & (cc >= rr)).astype(F32).astype(BF16)

        @pl.loop(0, n_rows // rows)
        def _(t):
            s = pl.multiple_of(t * rows, rows)
            a_low = ref[0, pl.ds(s, rows), 2 * kw + 2 * vw:2 * kw + 2 * vw + 256].astype(BF16)
            zl = _dot(a_low, wa_ref[...]) + ba_ref[...]
            la = (jnp.minimum(zl, 0.0) - jnp.log1p(jnp.exp(-jnp.abs(zl)))) / GLA_TAU
            at = pl.ds(pl.multiple_of(base + s, rows), rows)
            dec_scr[at, :kw] = _dot_exact_lhs(tri_f, la[:, :kw])
            dec_scr[at, kw:] = _dot_exact_lhs(tri_b, la[:, kw:])

    def pair(ref, sf, sb, base, st):
        blk_f, blk_b = ref[0, pl.ds(sf, c), :], ref[0, pl.ds(sb, c), :]
        blk = jnp.concatenate([blk_f, blk_b], axis=0)
        q = blk[:, 0:kw] * (GLA_DK ** -0.5)
        k = blk[:, kw:2 * kw]
        vb = blk[:, 2 * kw:2 * kw + vw].astype(BF16)
        b = jnp.concatenate([dec_scr[pl.ds(pl.multiple_of(base + sf, c), c), :kw],
                             dec_scr[pl.ds(pl.multiple_of(base + sb, c), c), kw:]], axis=0)
        b_end = jnp.where(fwd_rows, b[c - 1:c, :], b[c:c + 1, :])
        q_dec = q * jnp.exp(b)
        k_inv = (k * jnp.exp(-b)).astype(BF16)
        k_tail = (k * jnp.exp(b_end - b)).astype(BF16)
        qs = jnp.concatenate([q_dec * k_head[h] for h in range(nh)], axis=0).astype(BF16)
        att = lax.dot_general(qs, k_inv, NT_DIMS, preferred_element_type=F32)
        att = jnp.where(causal, att, 0.0).astype(BF16)
        o_all = _dot(att, vb)
        o = o_all[0:2 * c] * v_head[0]
        for h in range(1, nh):
            o = o + o_all[h * 2 * c:(h + 1) * 2 * c] * v_head[h]
        both = lax.dot_general(q_dec.astype(BF16), st.astype(BF16), NT_DIMS, preferred_element_type=F32)
        o = o + jnp.where(fwd_rows, both[:, :vw], both[:, vw:])
        v2 = jnp.concatenate([jnp.where(fwd_rows, vb, 0), jnp.where(fwd_rows, 0, vb)], axis=1)
        upd = lax.dot_general(v2, k_tail, TN_DIMS, preferred_element_type=F32)
        decay = jnp.where(st_fwd, jnp.exp(b[c - 1:c, :]), jnp.exp(b[c:c + 1, :]))
        return o, st * decay + upd * st_mask

    def scan(ref, n_chunks, base):
        unroll = GLA_UNROLL if n_chunks % GLA_UNROLL == 0 else 1

        @pl.loop(0, n_chunks // unroll)
        def _(it):
            st = st_scr[...]
            for u in range(unroll):
                n = it * unroll + u
                sf = pl.multiple_of(n * c, c)
                sb = pl.multiple_of((n_chunks - 1 - n) * c, c)
                o, st = pair(ref, sf, sb, base, st)
                of_scr[pl.ds(pl.multiple_of(base + sf, c), c), :] = o[:c]
                ob_scr[pl.ds(pl.multiple_of(base + sb, c), c), :] = o[c:]
            st_scr[...] = st

    def finish(ref, out_ref, n_rows, base):
        rows = GLA_FINISH_ROWS if n_rows % GLA_FINISH_ROWS == 0 else c

        @pl.loop(0, n_rows // rows)
        def _(t):
            s = pl.multiple_of(t * rows, rows)
            at = pl.ds(pl.multiple_of(base + s, rows), rows)
            o = of_scr[at, :] + ob_scr[at, :]
            ms = _dot_exact_rhs(o * o, head_mean)
            gate = ref[0, pl.ds(s, rows), 2 * kw + vw:2 * kw + 2 * vw]
            out_ref[0, pl.ds(s, rows), :] = (o * lax.rsqrt(ms + EPS) * g_ref[...] * _silu(gate)).astype(BF16)

    st_scr[...] = jnp.zeros_like(st_scr)
    decays(zc_ref, ctx_len, 0)
    decays(z_ref, seq, ctx_len)
    scan(zc_ref, ctx_len // c, 0)
    scan(z_ref, seq // c, ctx_len)
    finish(zc_ref, oc_ref, ctx_len, 0)
    finish(z_ref, o_ref, seq, ctx_len)


def gla_mixer(zg, zgc, w_a2, b_a, norm_g):
    bsz, seq, wd = zg.shape
    ctx_len = zgc.shape[1]
    vw = GLA_HEADS * GLA_DV
    kw = GLA_HEADS * GLA_DK
    wa = jnp.pad(w_a2, ((0, 0), (0, 128 - GLA_LOWRANK), (0, 0)))
    zero = jnp.zeros_like(wa[0])
    wa = jnp.block([[wa[0], zero], [zero, wa[1]]]).astype(BF16)
    full = lambda shape: pl.BlockSpec(shape, lambda b: (0,) * len(shape))
    return pl.pallas_call(
        functools.partial(_gla_kernel, seq=seq, ctx_len=ctx_len), grid=(bsz,),
        in_specs=[pl.BlockSpec((1, seq, wd), lambda b: (b, 0, 0)),
                  pl.BlockSpec((1, ctx_len, wd), lambda b: (b, 0, 0)),
                  full((256, 2 * kw)), full((1, 2 * kw)), full((1, vw))],
        out_specs=[pl.BlockSpec((1, seq, vw), lambda b: (b, 0, 0)),
                   pl.BlockSpec((1, ctx_len, vw), lambda b: (b, 0, 0))],
        out_shape=[jax.ShapeDtypeStruct((bsz, seq, vw), BF16), jax.ShapeDtypeStruct((bsz, ctx_len, vw), BF16)],
        scratch_shapes=[pltpu.VMEM((seq + ctx_len, vw), F32), pltpu.VMEM((seq + ctx_len, vw), F32),
                        pltpu.VMEM((seq + ctx_len, 2 * kw), F32), pltpu.VMEM((2 * vw, kw), F32)],
        compiler_params=_params("parallel"), name="gla_mixer")(
            zg, zgc, wa, b_a.reshape(1, 2 * kw), norm_g.reshape(1, vw))


def _conv_kernel(u_ref, dw_ref, dwb_ref, lng_ref, lnb_ref, pw_ref, pwb_ref, o_ref, h_scr, *, seq, rows):
    halo = CONV_HALO
    h_scr[0:halo, :] = jnp.zeros((halo, CONV_C), F32)
    h_scr[halo + seq:2 * halo + seq, :] = jnp.zeros((halo, CONV_C), F32)

    @pl.loop(0, seq // rows)
    def _(t):
        s = pl.multiple_of(t * rows, rows)
        u = u_ref[0, pl.ds(s, rows), :]
        h_scr[pl.ds(pl.multiple_of(halo + s, 8), rows), :] = u[:, :CONV_C] * jax.nn.sigmoid(u[:, CONV_C:])

    @pl.loop(0, seq // rows)
    def _(t):
        s = pl.multiple_of(t * rows, rows)
        win = h_scr[pl.ds(s, rows + 2 * halo), :]
        first = halo - CONV_K // 2
        acc = jnp.broadcast_to(dwb_ref[...], (rows, CONV_C))
        for r in range(8):
            taps = [k for k in range(CONV_K) if (first + k) % 8 == r]
            shifted = win if r == 0 else pltpu.roll(win, win.shape[0] - r, axis=0)
            for k in taps:
                at = 8 * ((first + k) // 8)
                acc = acc + shifted[at:at + rows] * dw_ref[k:k + 1, :]
        mu = jnp.mean(acc, axis=-1, keepdims=True)
        var = jnp.mean(jnp.square(acc - mu), axis=-1, keepdims=True)
        y = _silu((acc - mu) * lax.rsqrt(var + EPS) * lng_ref[...] + lnb_ref[...])
        o_ref[0, pl.ds(s, rows), :] = (_dot(y.astype(BF16), pw_ref[...]) + pwb_ref[...]).astype(BF16)


def conformer_conv(u, dw_w, dw_b, ln_g, ln_b, pw_w, pw_b):
    bsz, seq, _ = u.shape
    rows = _tile(seq, 128)
    full = lambda shape: pl.BlockSpec(shape, lambda b: (0,) * len(shape))
    row = lambda t: t.reshape(1, CONV_C)
    return pl.pallas_call(
        functools.partial(_conv_kernel, seq=seq, rows=rows), grid=(bsz,),
        in_specs=[pl.BlockSpec((1, seq, 2 * CONV_C), lambda b: (b, 0, 0)),
                  full((CONV_K + 1, CONV_C)), full((1, CONV_C)), full((1, CONV_C)), full((1, CONV_C)),
                  full((CONV_C, CONV_C)), full((1, CONV_C))],
        out_specs=pl.BlockSpec((1, seq, CONV_C), lambda b: (b, 0, 0)),
        out_shape=jax.ShapeDtypeStruct((bsz, seq, CONV_C), BF16),
        scratch_shapes=[pltpu.VMEM((seq + 2 * CONV_HALO, CONV_C), F32)],
        compiler_params=_params("parallel"), name="conformer_conv")(
            u, jnp.pad(dw_w, ((0, 1), (0, 0))), row(dw_b), row(ln_g), row(ln_b), pw_w.astype(BF16), row(pw_b))


def s5_matrices(lam_re, lam_im, log_dt, b_re, b_im, c_re, c_im):
    dt = jnp.exp(log_dt)[:, None]
    mag = jnp.exp(lam_re * dt)
    a_re, a_im = mag * jnp.cos(lam_im * dt), mag * jnp.sin(lam_im * dt)
    den = lam_re * lam_re + lam_im * lam_im
    f_re = ((a_re - 1.0) * lam_re + a_im * lam_im) / den
    f_im = (a_im * lam_re - (a_re - 1.0) * lam_im) / den
    bb_re = f_re[..., None] * b_re - f_im[..., None] * b_im
    bb_im = f_re[..., None] * b_im + f_im[..., None] * b_re
    eye = jnp.eye(S5_NG, dtype=F32)
    blk_b = lambda t: jnp.einsum("gph,gk->ghkp", t, eye).reshape(S5_C, S5_STATE)
    blk_c = lambda t: jnp.einsum("ghp,gk->gpkh", t, eye).reshape(S5_STATE, S5_C)
    a = jnp.stack([a_re.reshape(S5_STATE), a_im.reshape(S5_STATE)])
    b_mat = jnp.concatenate([blk_b(bb_re), blk_b(bb_im)], axis=1).astype(BF16)
    c_mat = jnp.concatenate([blk_c(c_re), -blk_c(c_im)], axis=0).astype(BF16)
    return a, b_mat, c_mat


def _s5_scan_kernel(uf_ref, ub_ref, a_ref, b_ref, c_ref, h0_ref, yf_ref, yb_ref, hl_ref, hs_f, hs_b, st_scr,
                    *, steps, bsz):
    ns = S5_STATE

    @pl.when(pl.program_id(0) == 0)
    def _():
        st_scr[...] = h0_ref[...]

    for d, (u_ref, hs) in enumerate(((uf_ref, hs_f), (ub_ref, hs_b))):
        u = u_ref[...].reshape(steps * bsz, S5_C).astype(BF16)
        hs[...] = _dot(u, b_ref[d])
    for d, hs in enumerate((hs_f, hs_b)):
        for s0 in range(0, ns, S5_STRIP):
            re, im = slice(s0, s0 + S5_STRIP), slice(ns + s0, ns + s0 + S5_STRIP)
            a_re, a_im = a_ref[d, 0:1, re], a_ref[d, 1:2, re]
            h_re, h_im = st_scr[d, :, re], st_scr[d, :, im]
            for j in range(steps):
                rows = pl.ds((steps - 1 - j if d == 1 else j) * bsz, bsz)
                h_re, h_im = (a_re * h_re - a_im * h_im + hs[rows, re], a_re * h_im + a_im * h_re + hs[rows, im])
                hs[rows, re] = h_re
                hs[rows, im] = h_im
            st_scr[d, :, re] = h_re
            st_scr[d, :, im] = h_im
    yf_ref[...] = _dot(hs_f[...].astype(BF16), c_ref[0]).reshape(steps, bsz, S5_C)
    yb_ref[...] = _dot(hs_b[...].astype(BF16), c_ref[1]).reshape(steps, bsz, S5_C)
    hl_ref[...] = st_scr[...]


def s5_scan(u_tm, a, b_mat, c_mat, h0):
    seq, bsz, _ = u_tm.shape
    steps = _tile(seq, 32)
    n = seq // steps
    full = lambda t: pl.BlockSpec(t.shape, lambda i: (0,) * t.ndim)
    fwd = pl.BlockSpec((steps, bsz, S5_C), lambda i: (i, 0, 0))
    bwd = pl.BlockSpec((steps, bsz, S5_C), lambda i: (n - 1 - i, 0, 0))
    y = jax.ShapeDtypeStruct((seq, bsz, S5_C), F32)
    return pl.pallas_call(
        functools.partial(_s5_scan_kernel, steps=steps, bsz=bsz), grid=(n,),
        in_specs=[fwd, bwd, full(a), full(b_mat), full(c_mat), full(h0)],
        out_specs=[fwd, bwd, full(h0)],
        out_shape=[y, y, jax.ShapeDtypeStruct(h0.shape, F32)],
        scratch_shapes=[pltpu.VMEM((steps * bsz, 2 * S5_STATE), F32), pltpu.VMEM((steps * bsz, 2 * S5_STATE), F32),
                        pltpu.VMEM(h0.shape, F32)],
        compiler_params=_params("arbitrary"), name="s5_scan")(u_tm, u_tm, a, b_mat, c_mat, h0)


def _s5_out_kernel(u_ref, yf_ref, yb_ref, d_ref, w_ref, b_ref, o_ref):
    y = d_ref[...] * u_ref[...] + yf_ref[...] + yb_ref[...]
    z = _dot(y.astype(BF16), w_ref[...]) + b_ref[...]
    o_ref[...] = (z[:, :S5_C] * jax.nn.sigmoid(z[:, S5_C:])).astype(BF16)


def s5_output(u, y_f, y_b, d_skip, glu_w, glu_b):
    n = u.shape[0]
    tm = _tile(n, 1024)
    rows = pl.BlockSpec((tm, S5_C), lambda i: (i, 0))
    full = lambda shape: pl.BlockSpec(shape, lambda i: (0,) * len(shape))
    return pl.pallas_call(
        _s5_out_kernel, grid=(n // tm,),
        in_specs=[rows, rows, rows, full((1, S5_C)), full((S5_C, 2 * S5_C)), full((1, 2 * S5_C))],
        out_specs=rows, out_shape=jax.ShapeDtypeStruct((n, S5_C), BF16),
        compiler_params=_params("parallel"), name="s5_output")(
            u, y_f, y_b, d_skip.reshape(1, S5_C), glu_w.astype(BF16), glu_b.reshape(1, 2 * S5_C))


def s5_mixer(zs, zsc, lam_re, lam_im, log_dt, b_re, b_im, c_re, c_im, d_skip, glu_w, glu_b):
    bsz, seq, _ = zs.shape
    ctx_len = zsc.shape[1]
    u = jnp.transpose(zs, (1, 0, 2))
    uc = jnp.transpose(zsc, (1, 0, 2))
    mats = [s5_matrices(lam_re[d], lam_im[d], log_dt[d], b_re[d], b_im[d], c_re[d], c_im[d]) for d in range(2)]
    a, b_mat, c_mat = [jnp.stack(t) for t in zip(*mats)]
    yc_f, yc_b, hc = s5_scan(uc, a, b_mat, c_mat, jnp.zeros((2, bsz, 2 * S5_STATE), F32))
    y_f, y_b, _ = s5_scan(u, a, b_mat, c_mat, hc)
    ys, ycs = (y_f, y_b), (yc_f, yc_b)
    flat = lambda t: t.reshape(-1, S5_C)
    o = s5_output(flat(u), flat(ys[0]), flat(ys[1]), d_skip, glu_w, glu_b).reshape(seq, bsz, S5_C)
    oc = s5_output(flat(uc), flat(ycs[0]), flat(ycs[1]), d_skip, glu_w, glu_b).reshape(ctx_len, bsz, S5_C)
    return jnp.transpose(o, (1, 0, 2)), jnp.transpose(oc, (1, 0, 2))


def rope_tables(seq_len, rotate):
    if not rotate:
        return jnp.ones((seq_len, MLA_ROPE), F32), jnp.zeros((seq_len, MLA_ROPE), F32)
    rows = seq_len // GRID_W
    row = jnp.broadcast_to(jnp.arange(rows, dtype=F32)[:, None], (rows, GRID_W)).reshape(seq_len)
    col = jnp.broadcast_to(jnp.arange(GRID_W, dtype=F32)[None, :], (rows, GRID_W)).reshape(seq_len)
    inv_freq = ROPE_BASE ** (-jnp.arange(ROPE_AXIS // 2, dtype=F32) / (ROPE_AXIS // 2))
    ang = jnp.stack([row[:, None] * inv_freq, col[:, None] * inv_freq], axis=1)
    full = lambda t: jnp.broadcast_to(t[:, :, None, :], (seq_len, 2, 2, ROPE_AXIS // 2)).reshape(seq_len, MLA_ROPE)
    return full(jnp.cos(ang)), full(jnp.sin(ang))


def mla_tables(seq_len, rotate):
    cos, sin = rope_tables(seq_len, rotate)
    head = lambda rope, fill: jnp.concatenate(
        [jnp.full((seq_len, MLA_NOPE), fill, F32), rope, jnp.zeros((seq_len, MLA_HEAD_PAD - MLA_NOPE - MLA_ROPE), F32)], 1)
    cos_q = jnp.tile(head(cos, 1.0), (1, MLA_HEADS))
    sin_q = jnp.tile(head(sin, 0.0), (1, MLA_HEADS))
    cs_k = jnp.concatenate([cos, sin, jnp.zeros((seq_len, 128 - 2 * MLA_ROPE), F32)], 1)
    return cos_q, sin_q, cs_k


def mla_weights(w_uq, w_ukv):
    hd = MLA_NOPE + MLA_ROPE
    zq = jnp.zeros((MLA_Q_LORA, MLA_HEAD_PAD - hd), F32)
    zn = jnp.zeros((MLA_Q_LORA, MLA_NOPE), F32)
    wq, wq_sw, wk, wv = [], [], [], []
    for h in range(MLA_HEADS):
        qh = w_uq[:, h * hd:(h + 1) * hd]
        wq += [qh, zq]
        wq_sw += [zn, _rope_swap(qh[:, MLA_NOPE:]), zq]
        kvh = w_ukv[:, h * (MLA_NOPE + MLA_V):(h + 1) * (MLA_NOPE + MLA_V)]
        wk += [kvh[:, :MLA_NOPE], jnp.zeros((MLA_KV_LORA, MLA_HEAD_PAD - MLA_NOPE), F32)]
        wv += [kvh[:, MLA_NOPE:]]
    place = np.zeros((128, MLA_HEADS * MLA_HEAD_PAD), np.float32)
    for h in range(MLA_HEADS):
        for j in range(MLA_ROPE):
            place[j, h * MLA_HEAD_PAD + MLA_NOPE + j] = 1.0
            place[MLA_ROPE + j, h * MLA_HEAD_PAD + MLA_NOPE + j] = 1.0
    cat = lambda ts: jnp.concatenate(ts, axis=1).astype(BF16)
    return cat(wq), cat(wq_sw), cat(wk), cat(wv), jnp.asarray(place).astype(BF16)


def _mla_prep_kernel(z_ref, qg_ref, kg_ref, wq_ref, wqs_ref, wk_ref, wv_ref, pl_ref, cq_ref, sq_ref, csk_ref,
                     q_ref, k_ref, v_ref):
    z = z_ref[0]
    norm = lambda t, g: (t * lax.rsqrt(jnp.mean(t * t, axis=-1, keepdims=True) + EPS) * g).astype(BF16)
    cq = norm(z[:, :MLA_Q_LORA], qg_ref[...])
    ckv = norm(z[:, MLA_Q_LORA:MLA_Q_LORA + MLA_KV_LORA], kg_ref[...])
    scale = (MLA_NOPE + MLA_ROPE) ** -0.5
    q = _dot(cq, wq_ref[...]) * cq_ref[...] + _dot(cq, wqs_ref[...]) * sq_ref[...]
    q_ref[0] = (q * scale).astype(BF16)
    kr = z[:, MLA_Q_LORA + MLA_KV_LORA:] * csk_ref[...]
    k_ref[0] = (_dot(ckv, wk_ref[...]) + _dot_exact_rhs(kr, pl_ref[...])).astype(BF16)
    v_ref[0] = _dot(ckv, wv_ref[...]).astype(BF16)


def mla_prep(zm, qn_g, kvn_g, weights, tables):
    bsz, seq, wd = zm.shape
    tm = _tile(seq, 512)
    wq, wq_sw, wk, wv, place = weights
    cos_q, sin_q, cs_k = tables
    qw, vw = MLA_HEADS * MLA_HEAD_PAD, MLA_HEADS * MLA_V
    full = lambda t: pl.BlockSpec(t.shape, lambda b, i: (0,) * t.ndim)
    pos = lambda t: pl.BlockSpec((tm, t.shape[1]), lambda b, i: (i, 0))
    out = lambda w: pl.BlockSpec((1, tm, w), lambda b, i: (b, i, 0))
    qg, kg = qn_g.reshape(1, -1), kvn_g.reshape(1, -1)
    return pl.pallas_call(
        _mla_prep_kernel, grid=(bsz, seq // tm),
        in_specs=[pl.BlockSpec((1, tm, wd), lambda b, i: (b, i, 0)), full(qg), full(kg), full(wq), full(wq_sw),
                  full(wk), full(wv), full(place), pos(cos_q), pos(sin_q), pos(cs_k)],
        out_specs=[out(qw), out(qw), out(vw)],
        out_shape=[jax.ShapeDtypeStruct((bsz, seq, qw), BF16), jax.ShapeDtypeStruct((bsz, seq, qw), BF16),
                   jax.ShapeDtypeStruct((bsz, seq, vw), BF16)],
        compiler_params=_params("parallel", "parallel"), name="mla_prep")(
            zm, qg, kg, wq, wq_sw, wk, wv, place, cos_q, sin_q, cs_k)


def _mla_attn_kernel(q_ref, *refs, n_seg):
    k_refs, v_refs, o_ref = refs[:n_seg], refs[n_seg:2 * n_seg], refs[2 * n_seg]
    outs = []
    for h in range(MLA_HEADS):
        q = q_ref[0, :, h * MLA_HEAD_PAD:(h + 1) * MLA_HEAD_PAD]
        s = [lax.dot_general(q, k_ref[0, :, h * MLA_HEAD_PAD:(h + 1) * MLA_HEAD_PAD], NT_DIMS,
                             preferred_element_type=F32) for k_ref in k_refs]
        m = functools.reduce(jnp.maximum, [jnp.max(t, axis=-1, keepdims=True) for t in s])
        p = [jnp.exp(t - m) for t in s]
        den = functools.reduce(jnp.add, [jnp.sum(t, axis=-1, keepdims=True) for t in p])
        o = functools.reduce(jnp.add, [_dot(t.astype(BF16), v_ref[0, :, h * MLA_V:(h + 1) * MLA_V])
                                       for t, v_ref in zip(p, v_refs)])
        outs.append(o / den)
    o_ref[0] = jnp.concatenate(outs, axis=-1).astype(BF16)


def mla_attention(q, ks, vs):
    bsz, seq, qw = q.shape
    tq = _tile(seq, 512)
    vw = MLA_HEADS * MLA_V
    seg = lambda t: pl.BlockSpec((1,) + t.shape[1:], lambda b, i: (b, 0, 0))
    return pl.pallas_call(
        functools.partial(_mla_attn_kernel, n_seg=len(ks)), grid=(bsz, seq // tq),
        in_specs=[pl.BlockSpec((1, tq, qw), lambda b, i: (b, i, 0))] + [seg(t) for t in ks] + [seg(t) for t in vs],
        out_specs=pl.BlockSpec((1, tq, vw), lambda b, i: (b, i, 0)),
        out_shape=jax.ShapeDtypeStruct((bsz, seq, vw), BF16),
        compiler_params=_params("parallel", "parallel"), name="mla_attention")(q, *ks, *vs)


def _outproj_kernel(a_ref, b_ref, c_ref, d_ref, w_ref, x_ref, g1_ref, sh_ref, sc_ref, g_ref, rw_ref,
                    xo_ref, h_ref, lg_ref):
    mix = None
    for j, o_ref in enumerate((a_ref, b_ref, c_ref, d_ref)):
        wd = o_ref.shape[-1]
        part = _dot(o_ref[0], w_ref[j * wd:(j + 1) * wd, :])
        mix = part if mix is None else mix + part
    x = x_ref[0] + g1_ref[0] * mix
    xo_ref[0] = x
    ms = jnp.mean(x * x, axis=-1, keepdims=True)
    h = x * lax.rsqrt(ms + EPS) * g_ref[...] * (1.0 + sc_ref[0]) + sh_ref[0]
    _store_row_slabs(h_ref, _pack_bf16_pairs(h))
    lg_ref[...] =lax.dot_general(rw_ref[...], h, NT_DIMS, precision=HIGHEST, preferred_element_type=F32)


def out_projection(parts, w_out, x, g1, shift, scale, g, router_w):
    bsz, seq, dm = x.shape
    tm = _tile(seq, 512)
    nt = seq // tm
    ne = router_w.shape[1]
    groups = dm // 2 // 128
    rows = lambda w: pl.BlockSpec((1, tm, w), lambda b, i: (b, i, 0))
    vec = pl.BlockSpec((1, 1, dm), lambda b, i: (b, 0, 0))
    full = lambda shape: pl.BlockSpec(shape, lambda b, i: (0,) * len(shape))
    return pl.pallas_call(
        _outproj_kernel, grid=(bsz, nt),
        in_specs=[rows(p.shape[-1]) for p in parts] + [full(w_out.shape), rows(dm), vec, vec, vec, full((1, dm)),
                                                       full((ne, dm))],
        out_specs=[rows(dm), pl.BlockSpec((tm * groups, 128), lambda b, i: (b * nt + i, 0)),
                   pl.BlockSpec((ne, tm), lambda b, i: (0, b * nt + i))],
        out_shape=[jax.ShapeDtypeStruct((bsz, seq, dm), F32), jax.ShapeDtypeStruct((bsz * seq * groups, 128), jnp.int32),
                   jax.ShapeDtypeStruct((ne, bsz * seq), F32)],
        compiler_params=_params("parallel", "parallel"), name="out_projection")(
            *parts, w_out.astype(BF16), x, g1, shift, scale, g.reshape(1, dm), router_w.T)


def _route_kernel(lg_ref, b_ref, e_ref, w_ref):
    ne, tt = lg_ref.shape
    per = ne // N_GROUPS
    neg = -jnp.inf
    scores = jax.nn.sigmoid(lg_ref[...])
    biased = scores + b_ref[...]
    v3 = biased.reshape(N_GROUPS, per, tt)
    e_in = lax.broadcasted_iota(jnp.int32, v3.shape, 1).astype(F32)
    m1 = jnp.max(v3, axis=1, keepdims=True)
    i1 = jnp.min(jnp.where(v3 == m1, e_in, float(per)), axis=1, keepdims=True)
    m2 = jnp.max(jnp.where(e_in == i1, neg, v3), axis=1, keepdims=True)
    grp = (m1 + m2).reshape(N_GROUPS, tt)

    def pick(cur, count):
        ids = lax.broadcasted_iota(jnp.int32, cur.shape, 0).astype(F32)
        marks = jnp.zeros(cur.shape, F32)
        picked = []
        for _ in range(count):
            m = jnp.max(cur, axis=0, keepdims=True)
            first = jnp.min(jnp.where(cur == m, ids, float(cur.shape[0])), axis=0, keepdims=True)
            hit = ids == first
            marks = jnp.where(hit, 1.0, marks)
            cur = jnp.where(hit, neg, cur)
            picked.append(first)
        return marks, picked

    grp_on, _ = pick(grp, TOPK_GROUPS)
    exp_on = jnp.broadcast_to(grp_on.reshape(N_GROUPS, 1, tt), v3.shape).reshape(ne, tt)
    chosen, picked = pick(jnp.where(exp_on > 0.0, biased, neg), TOP_K)
    w = scores * chosen
    gate = w / jnp.sum(w, axis=0, keepdims=True) * ROUTED_SCALE
    ids = lax.broadcasted_iota(jnp.int32, gate.shape, 0).astype(F32)
    e_ref[...] = jnp.concatenate(picked, axis=0).astype(jnp.int32)
    w_ref[...] = jnp.concatenate([jnp.sum(jnp.where(ids == p, gate, 0.0), axis=0, keepdims=True) for p in picked], axis=0)


def route(logits_t, router_b):
    ne, n = logits_t.shape
    tt = _tile(n, 1024)
    out = pl.BlockSpec((TOP_K, tt), lambda i: (0, i))
    return pl.pallas_call(
        _route_kernel, grid=(n // tt,),
        in_specs=[pl.BlockSpec((ne, tt), lambda i: (0, i)), pl.BlockSpec((ne, 1), lambda i: (0, 0))],
        out_specs=[out, out],
        out_shape=[jax.ShapeDtypeStruct((TOP_K, n), jnp.int32), jax.ShapeDtypeStruct((TOP_K, n), F32)],
        compiler_params=_params("parallel"), name="route")(logits_t, router_b.reshape(ne, 1))


def moe_plan(eid, wgt, chunk, xg, yg):
    n = eid.shape[1]
    nc, na = n // chunk, chunk * TOP_K
    ns = na // MOE_ROWS + N_EXPERTS
    key = eid.T.reshape(nc, na) * na + jnp.arange(na, dtype=jnp.int32)
    skey, sw = lax.sort((key, wgt.T.reshape(nc, na)), dimension=1, num_keys=1)
    e_sorted, tok = skey // na, (skey % na) // TOP_K
    experts = jnp.arange(N_EXPERTS, dtype=jnp.int32)
    ends = jnp.sum(e_sorted[:, None, :] <= experts[None, :, None], axis=-1, dtype=jnp.int32)
    starts = jnp.concatenate([jnp.zeros((nc, 1), jnp.int32), ends[:, :-1]], axis=1)
    pad_end = jnp.cumsum(-(-(ends - starts) // MOE_ROWS) * MOE_ROWS, axis=1)
    pad_start = jnp.concatenate([jnp.zeros((nc, 1), jnp.int32), pad_end[:, :-1]], axis=1)
    first = jnp.arange(ns, dtype=jnp.int32) * MOE_ROWS
    exp = jnp.minimum(jnp.sum(pad_end[:, None, :] <= first[None, :, None], axis=-1, dtype=jnp.int32), N_EXPERTS - 1)
    active = first[None, :] < pad_end[:, -1:]
    take = lambda t, i: jnp.take_along_axis(t, i, axis=1)
    rank = first[None, :, None] + jnp.arange(MOE_ROWS, dtype=jnp.int32) - take(pad_start, exp)[..., None]
    real = (rank < take(ends - starts, exp)[..., None]) & active[..., None]
    src = jnp.clip(take(starts, exp)[..., None] + rank, 0, na - 1).reshape(nc, ns * MOE_ROWS)
    rows = lambda t: take(t, src).reshape(nc, ns, MOE_ROWS)
    flat = lambda t: t.reshape(nc * ns, 1, MOE_ROWS)
    gather_at = flat(jnp.where(real, rows(tok), 0) * xg)
    scatter_at = flat(jnp.where(real, rows(tok), chunk) * yg)
    weight = flat(jnp.where(real, rows(sw), 0.0))
    return gather_at, scatter_at, weight, exp.reshape(nc * ns), active.astype(jnp.int32).reshape(nc * ns)


def _moe_tick(x_ref, y_ref, idx_ref, w_ref, tick, wgu_ref, wd_ref, gather_to, mm_from, mm_to, scatter_from):
    bm = MOE_ROWS
    stride = bm + 1
    xg = mm_from.shape[0] // stride
    yg = mm_to.shape[0] // stride
    for mi in range(bm):
        src = pl.multiple_of(idx_ref[0, 0, 2 * tick * bm + mi], xg)
        gather_to[pl.ds(mi, xg, stride=stride), :] = x_ref[0, pl.ds(src, xg), :]
    low, high = _unpack_bf16_pairs([mm_from[j * stride:j * stride + bm, :] for j in range(xg)])
    half = low.shape[1]
    ff = wd_ref.shape[1]
    up = _dot(low, wgu_ref[0, :half, :]) + _dot(high, wgu_ref[0, half:, :])
    act = (_silu(up[:, :ff]) * up[:, ff:]).astype(BF16)
    out = _dot(act, wd_ref[0])
    for j in range(yg):
        mm_to[j * stride:j * stride + bm, :] = out[:, j * 128:(j + 1) * 128]
    for base in range(0, bm, MOE_BATCH):
        at = [pl.multiple_of(idx_ref[0, 0, (2 * tick + 1) * bm + mi], yg) for mi in range(base, base + MOE_BATCH)]
        new = [y_ref[0, pl.ds(i, yg), :] + w_ref[0, 0, tick * bm + mi] * scatter_from[pl.ds(mi, yg, stride=stride), :]
               for i, mi in zip(at, range(base, base + MOE_BATCH))]
        for i, v in zip(at, new):
            y_ref[0, pl.ds(i, yg), :] = v


def _moe_routed_kernel(exp_ref, act_ref, x_ref, idx_ref, w_ref, wgua_ref, wda_ref, wgub_ref, wdb_ref,
                       y_ref, xt0, xt1, ot0, ot1, *, ns):
    c, g = pl.program_id(0), pl.program_id(1)

    @pl.when(g == 0)
    def _():
        y_ref[...] = jnp.zeros_like(y_ref)

    @pl.when((c == 0) & (g == 0))
    def _():
        xt1[...] = jnp.zeros_like(xt1)
        ot0[...] = jnp.zeros_like(ot0)
        ot1[...] = jnp.zeros_like(ot1)

    def live(step):
        return (step >= 0) & (step < ns) & (act_ref[c * ns + jnp.clip(step, 0, ns - 1)] != 0)

    t = 2 * g

    @pl.when(live(t - 2) | live(t - 1) | live(t) | live(t + 1))
    def _():
        _moe_tick(x_ref, y_ref, idx_ref, w_ref, 0, wgua_ref, wda_ref,
                  gather_to=xt0, mm_from=xt1, mm_to=ot1, scatter_from=ot0)
        _moe_tick(x_ref, y_ref, idx_ref, w_ref, 1, wgub_ref, wdb_ref,
                  gather_to=xt1, mm_from=xt0, mm_to=ot0, scatter_from=ot1)


def moe_routed(slabs, eid, wgt, wgu, wd):
    n = eid.shape[1]
    xg = slabs.shape[0] // n
    yg = 2 * xg
    dm, ff = wd.shape[2], wd.shape[1]
    chunk = _tile(n, MOE_CHUNK)
    nc = n // chunk
    ns = chunk * TOP_K // MOE_ROWS + N_EXPERTS
    assert ns % 2 == 0
    steps = (ns + 2) // 2
    last = ns - 1
    gather_at, scatter_at, weight, exp, active = moe_plan(eid, wgt, chunk, xg, yg)
    tick = jnp.arange(steps, dtype=jnp.int32)[:, None] * 2 + jnp.arange(2, dtype=jnp.int32)[None, :]
    gathered = jnp.minimum(tick, last)
    scattered = jnp.where(tick >= 2, tick - 2, last)
    per_step = lambda t, at: jnp.take(t.reshape(nc, ns, MOE_ROWS), at, axis=1)
    idx = jnp.concatenate([per_step(gather_at, gathered[:, 0]), per_step(scatter_at, scattered[:, 0]),
                           per_step(gather_at, gathered[:, 1]), per_step(scatter_at, scattered[:, 1])], axis=-1)
    wts = jnp.concatenate([per_step(weight, scattered[:, 0]), per_step(weight, scattered[:, 1])], axis=-1)
    smem = lambda width: pl.BlockSpec((1, 1, width), lambda c, g, e, a: (c * steps + g, 0, 0), memory_space=pltpu.SMEM)
    computed = lambda tk: lambda g: jnp.clip(2 * g + tk - 1, 0, last)
    expert = lambda shape, step: pl.BlockSpec((1,) + shape, lambda c, g, e, a: (e[c * ns + step(g)], 0, 0))
    once = pl.Buffered(1)
    stage = lambda groups, dtype: pltpu.VMEM((-(-groups * (MOE_ROWS + 1) // 8) * 8, 128), dtype)
    return pl.pallas_call(
        functools.partial(_moe_routed_kernel, ns=ns),
        grid_spec=pltpu.PrefetchScalarGridSpec(
            num_scalar_prefetch=2, grid=(nc, steps),
            in_specs=[pl.BlockSpec((1, chunk * xg, 128), lambda c, g, e, a: (c, 0, 0), pipeline_mode=once),
                      smem(4 * MOE_ROWS), smem(2 * MOE_ROWS),
                      expert((dm, 2 * ff), computed(0)), expert((ff, dm), computed(0)),
                      expert((dm, 2 * ff), computed(1)), expert((ff, dm), computed(1))],
            out_specs=pl.BlockSpec((1, chunk * yg + 8, 128), lambda c, g, e, a: (c, 0, 0), pipeline_mode=once),
            scratch_shapes=[stage(xg, jnp.int32), stage(xg, jnp.int32), stage(yg, F32), stage(yg, F32)]),
        out_shape=jax.ShapeDtypeStruct((nc, chunk * yg + 8, 128), F32),
        compiler_params=_params("arbitrary", "arbitrary"), name="moe_routed")(
            exp, active, slabs.reshape(nc, chunk * xg, 128), idx.reshape(nc * steps, 1, 4 * MOE_ROWS),
            wts.reshape(nc * steps, 1, 2 * MOE_ROWS), wgu, wd, wgu, wd)


def _moe_finish_kernel(y_ref, h_ref, x_ref, g2_ref, wg_ref, wu_ref, wd_ref, fg_ref, o_ref, *, final_norm):
    tm = x_ref.shape[0]
    low, high = _unpack_bf16_pairs(_load_row_slabs(h_ref, tm, h_ref.shape[0] // tm))
    routed = jnp.concatenate(_load_row_slabs(y_ref, tm, y_ref.shape[1] // tm, lead=(0,)), axis=1)
    half = low.shape[1]
    up = lambda w_ref: _dot(low, w_ref[:half, :]) + _dot(high, w_ref[half:, :])
    shared = _dot((_silu(up(wg_ref)) * up(wu_ref)).astype(BF16), wd_ref[...])
    x = x_ref[...] + g2_ref[0] * (routed + shared)
    if final_norm:
        x = x * lax.rsqrt(jnp.mean(x * x, axis=-1, keepdims=True) + EPS) * fg_ref[...]
    o_ref[...] = x


def moe_finish(y, slabs, x, g2, sg, su, sd, final_g, final_norm):
    bsz, seq, dm = x.shape
    n = bsz * seq
    tm = _tile(seq, 512)
    nt = seq // tm
    xg, yg = slabs.shape[0] // n, dm // 128
    per_chunk = (y.shape[1] - 8) // yg // tm
    rows = lambda w: pl.BlockSpec((tm, w), lambda i: (i, 0))
    full = lambda t: pl.BlockSpec(t.shape, lambda i: (0,) * t.ndim)
    sg, su, sd, fg = sg.astype(BF16), su.astype(BF16), sd.astype(BF16), final_g.reshape(1, dm)
    out = pl.pallas_call(
        functools.partial(_moe_finish_kernel, final_norm=final_norm), grid=(n // tm,),
        in_specs=[pl.BlockSpec((1, tm * yg, 128), lambda i: (i // per_chunk, i % per_chunk, 0)),
                  pl.BlockSpec((tm * xg, 128), lambda i: (i, 0)), rows(dm),
                  pl.BlockSpec((1, 1, dm), lambda i: (i // nt, 0, 0)), full(sg), full(su), full(sd), full(fg)],
        out_specs=rows(dm), out_shape=jax.ShapeDtypeStruct((n, dm), F32),
        compiler_params=_params("parallel"), name="moe_finish")(y, slabs, x.reshape(n, dm), g2, sg, su, sd, fg)
    return out.reshape(bsz, seq, dm)


def kernel(x, c, ctx, c_ctx, ada_w, ada_b, norm1_g, norm2_g, w_in, w_out, gla_w_a2, gla_b_a, gla_norm_g,
           conv_dw_w, conv_dw_b, conv_ln_g, conv_ln_b, conv_pw_w, conv_pw_b,
           s5_lam_re, s5_lam_im, s5_log_dt, s5_b_re, s5_b_im, s5_c_re, s5_c_im, s5_d, s5_glu_w, s5_glu_b,
           mla_qn_g, mla_kvn_g, mla_w_uq, mla_w_ukv,
           moe_router_w, moe_router_b, moe_w_gate, moe_w_up, moe_w_down,
           shared_w_gate, shared_w_up, shared_w_down, final_g):
    bsz, seq, dm = x.shape
    ctx_len = ctx.shape[1]
    depth = ada_w.shape[0]
    cc = jnp.concatenate([c, c_ctx[None], jnp.zeros((-(bsz + 1) % 8, dm), F32)], axis=0)
    tables = mla_tables(seq, True)
    tables_c = mla_tables(ctx_len, False)
    xc = ctx
    for l in range(depth):
        last = l == depth - 1
        mod = ada_mod(cc, ada_w[l], ada_b[l])
        lat = [t.reshape(bsz, 1, dm) for t in jnp.split(mod[:bsz], 6, axis=-1)]
        con = [jnp.broadcast_to(t.reshape(1, 1, dm), (bsz, 1, dm)) for t in jnp.split(mod[bsz], 6)]
        sh1, sc1, g1, sh2, sc2, g2 = lat
        sh1c, sc1c, g1c, sh2c, sc2c, g2c = con
        w_in_l = assemble_w_in(w_in[l])
        zg, zv, zs, zm = in_projection(x, sh1, sc1, norm1_g[l], w_in_l)
        zgc, zvc, zsc, zmc = in_projection(xc, sh1c, sc1c, norm1_g[l], w_in_l)
        conv_p = (conv_dw_w[l], conv_dw_b[l], conv_ln_g[l], conv_ln_b[l], conv_pw_w[l], conv_pw_b[l])
        o_gla, oc_gla = gla_mixer(zg, zgc, gla_w_a2[l], gla_b_a[l], gla_norm_g[l])
        o_conv = conformer_conv(zv, *conv_p)
        o_s5, oc_s5 = s5_mixer(zs, zsc, s5_lam_re[l], s5_lam_im[l], s5_log_dt[l], s5_b_re[l], s5_b_im[l],
                               s5_c_re[l], s5_c_im[l], s5_d[l], s5_glu_w[l], s5_glu_b[l])
        mla_w = mla_weights(mla_w_uq[l], mla_w_ukv[l])
        q, k, v = mla_prep(zm, mla_qn_g[l], mla_kvn_g[l], mla_w, tables)
        qc, kc, vc = mla_prep(zmc, mla_qn_g[l], mla_kvn_g[l], mla_w, tables_c)
        o_mla = mla_attention(q, [k, kc], [v, vc])
        experts = (jnp.concatenate([moe_w_gate[l], moe_w_up[l]], axis=2).astype(BF16), moe_w_down[l].astype(BF16))
        shared = (shared_w_gate[l], shared_w_up[l], shared_w_down[l])

        def ffn(parts, x_in, gate1, shift, scale, gate2, final_norm):
            x_mid, slabs, logits = out_projection(parts, w_out[l], x_in, gate1, shift, scale, norm2_g[l], moe_router_w[l])
            y = moe_routed(slabs, *route(logits, moe_router_b[l]), *experts)
            return moe_finish(y, slabs, x_mid, gate2, *shared, final_g, final_norm)

        if not last:
            oc_conv = conformer_conv(zvc, *conv_p)
            oc_mla = mla_attention(qc, [kc], [vc])
            xc = ffn([oc_gla, oc_conv, oc_s5, oc_mla], xc, g1c, sh2c, sc2c, g2c, False)
        x = ffn([o_gla, o_conv, o_s5, o_mla], x, g1, sh2, sc2, g2, last)
    return x
```

```python
import functools
import math

import numpy as np
import jax
import jax.numpy as jnp
from jax import lax
from jax.experimental import pallas as pl
from jax.experimental.pallas import tpu as pltpu

F32 = jnp.float32
BF16 = jnp.bfloat16
HIGHEST = lax.Precision.HIGHEST
EPS = 1e-6

GRID_W = 64

GLA_HEADS = 4
GLA_DK = 32
GLA_DV = 64
GLA_LOWRANK = 16
GLA_TAU = 16.0
GLA_CHUNK = 64
GLA_UNROLL = 2
GLA_FINISH_ROWS = 256

CONV_C = 256
CONV_K = 31
CONV_HALO = 16

S5_C = 256
S5_GROUP = 16
S5_NG = 16
S5_P = 64
S5_STATE = S5_NG * S5_P
S5_STRIP = 256

MLA_HEADS = 4
MLA_NOPE = 64
MLA_ROPE = 32
MLA_V = 64
MLA_Q_LORA = 256
MLA_KV_LORA = 128
MLA_HEAD_PAD = 128
ROPE_AXIS = MLA_ROPE // 2
ROPE_BASE = 10000.0

N_EXPERTS = 64
TOP_K = 8
N_GROUPS = 8
TOPK_GROUPS = 4
ROUTED_SCALE = 2.5
MOE_CHUNK = 4096
MOE_ROWS = 256
MOE_BATCH = 8

IN_SPLITS = (128, 128, 256, 256, 16, 16, 512, 256, 256, 128, 32)
W_GLA, W_CONV, W_S5, W_MLA = 1024, 512, 256, 512

VMEM_LIMIT = 48 * 1024 * 1024

NT_DIMS = (((1,), (1,)), ((), ()))
TN_DIMS = (((0,), (0,)), ((), ()))


def _params(*sem, vmem=VMEM_LIMIT):
    return pltpu.CompilerParams(dimension_semantics=sem, vmem_limit_bytes=vmem)


def _silu(v):
    return v * jax.nn.sigmoid(v)


def _dot(a, b, **kw):
    return jnp.dot(a, b, preferred_element_type=F32, **kw)


def _split_bf16(x):
    hi = x.astype(BF16)
    rest = x - hi.astype(F32)
    mid = rest.astype(BF16)
    return hi, mid, (rest - mid.astype(F32)).astype(BF16)


def _dot_exact_lhs(m, x):
    n = x.shape[1]
    prod = _dot(m, jnp.concatenate(_split_bf16(x), axis=1))
    return prod[:, :n] + prod[:, n:2 * n] + prod[:, 2 * n:]


def _dot_exact_rhs(x, m):
    n = x.shape[0]
    prod = _dot(jnp.concatenate(_split_bf16(x), axis=0), m)
    return prod[:n] + prod[n:2 * n] + prod[2 * n:]


def _tile(n, pref):
    return pref if n % pref == 0 else n


def _pack_bf16_pairs(h):
    bits = pltpu.bitcast(h.astype(BF16).astype(F32), jnp.int32)
    w = h.shape[-1] // 2
    return lax.shift_right_logical(bits[:, :w], 16) | bits[:, w:]


def _store_row_slabs(ref, value):
    rows, g = value.shape[0], value.shape[1] // 128
    for j in range(g):
        ref[pl.ds(j, rows, stride=g), :] = value[:, j * 128:(j + 1) * 128]


def _load_row_slabs(ref, rows, g, lead=()):
    return [ref[lead + (pl.ds(j, rows, stride=g), slice(None))] for j in range(g)]


def _unpack_bf16_pairs(words):
    low = jnp.concatenate([pltpu.bitcast(v << 16, F32) for v in words], axis=1)
    high = jnp.concatenate([pltpu.bitcast(v & -65536, F32) for v in words], axis=1)
    return low.astype(BF16), high.astype(BF16)


def _ada_kernel(c_ref, w_ref, b_ref, o_ref):
    o_ref[...] = _dot(_silu(c_ref[...]), w_ref[...], precision=HIGHEST) + b_ref[...]


def ada_mod(cc, w, b):
    rows, dm = cc.shape
    n = w.shape[1]
    tn = _tile(n, 512)
    return pl.pallas_call(
        _ada_kernel, grid=(n // tn,),
        in_specs=[pl.BlockSpec((rows, dm), lambda j: (0, 0)),
                  pl.BlockSpec((dm, tn), lambda j: (0, j)),
                  pl.BlockSpec((1, tn), lambda j: (0, j))],
        out_specs=pl.BlockSpec((rows, tn), lambda j: (0, j)),
        out_shape=jax.ShapeDtypeStruct((rows, n), F32),
        compiler_params=_params("arbitrary"), name="ada_mod")(cc, w, b.reshape(1, n))


def _inproj_kernel(x_ref, sh_ref, sc_ref, g_ref, w_ref, *o_refs):
    x = x_ref[0]
    ms = jnp.mean(x * x, axis=-1, keepdims=True)
    h = (x * lax.rsqrt(ms + EPS) * g_ref[...] * (1.0 + sc_ref[0]) + sh_ref[0]).astype(BF16)
    off = 0
    for o_ref in o_refs:
        w = o_ref.shape[-1]
        o_ref[0] = _dot(h, w_ref[:, off:off + w])
        off += w


def in_projection(x, shift, scale, g, w):
    bsz, seq, dm = x.shape
    tm = _tile(seq, 1024)
    widths = (W_GLA, W_CONV, W_S5, W_MLA)
    vec = pl.BlockSpec((1, 1, dm), lambda b, i: (b, 0, 0))
    return pl.pallas_call(
        _inproj_kernel, grid=(bsz, seq // tm),
        in_specs=[pl.BlockSpec((1, tm, dm), lambda b, i: (b, i, 0)), vec, vec,
                  pl.BlockSpec((1, dm), lambda b, i: (0, 0)),
                  pl.BlockSpec(w.shape, lambda b, i: (0, 0))],
        out_specs=[pl.BlockSpec((1, tm, wd), lambda b, i: (b, i, 0)) for wd in widths],
        out_shape=[jax.ShapeDtypeStruct((bsz, seq, wd), F32) for wd in widths],
        compiler_params=_params("parallel", "parallel"), name="in_projection")(x, shift, scale, g.reshape(1, dm), w)


def _rope_swap(t):
    s = t.reshape(t.shape[:-1] + (2, 2, ROPE_AXIS // 2))
    return jnp.stack([-s[..., 1, :], s[..., 0, :]], axis=-2).reshape(t.shape)


def assemble_w_in(w_in):
    edges = np.cumsum((0,) + IN_SPLITS)
    q, k, v, r, a_f, a_b, conv, s5, cq, ckv, kr = [w_in[:, edges[i]:edges[i + 1]] for i in range(len(IN_SPLITS))]
    pad = lambda t, w: jnp.pad(t, ((0, 0), (0, w - t.shape[1])))
    cols = [q, k, v, r, pad(a_f, 128), pad(a_b, 128), conv, s5, cq, ckv, pad(jnp.concatenate([kr, _rope_swap(kr)], 1), 128)]
    return jnp.concatenate(cols, axis=1).astype(BF16)


def _gla_kernel(z_ref, zc_ref, wa_ref, ba_ref, g_ref, o_ref, oc_ref, of_scr, ob_scr, dec_scr, st_scr, *, seq, ctx_len):
    c = GLA_CHUNK
    nh = GLA_HEADS
    kw = nh * GLA_DK
    vw = nh * GLA_DV
    iota = lambda shape, d: lax.broadcasted_iota(jnp.int32, shape, d)
    ra, ca = iota((nh * 2 * c, 2 * c), 0) & (2 * c - 1), iota((nh * 2 * c, 2 * c), 1)
    causal = ((ra < c) & (ca <= ra)) | ((ra >= c) & (ca >= ra))
    fwd_rows = iota((2 * c, 1), 0) < c
    k_of = lambda t: t >> int(math.log2(GLA_DK))
    v_of = lambda t: t >> int(math.log2(GLA_DV))
    k_head = [(k_of(iota((1, kw), 1)) == h).astype(F32) for h in range(nh)]
    v_head = [(v_of(iota((1, vw), 1)) == h).astype(F32) for h in range(nh)]
    st_rows = iota((2 * vw, kw), 0)
    st_mask = (v_of(st_rows & (vw - 1)) == k_of(iota((2 * vw, kw), 1))).astype(F32)
    st_fwd = st_rows < vw
    head_mean = ((v_of(iota((vw, vw), 0)) == v_of(iota((vw, vw), 1))).astype(F32) * (1.0 / GLA_DV)).astype(BF16)

    def decays(ref, n_rows, base):
        rows = GLA_FINISH_ROWS if n_rows % GLA_FINISH_ROWS == 0 else c
        rr, cc = iota((rows, rows), 0), iota((rows, rows), 1)
        same = (rr >> int(math.log2(c))) == (cc >> int(math.log2(c)))
        tri_f = (same & (cc <= rr)).astype(F32).astype(BF16)
        tri_b = (same & (cc >= rr)).astype(F32).astype(BF16)

        @pl.loop(0, n_rows // rows)
        def _(t):
            s = pl.multiple_of(t * rows, rows)
            a_low = ref[0, pl.ds(s, rows), 2 * kw + 2 * vw:2 * kw + 2 * vw + 256].astype(BF16)
            zl = _dot(a_low, wa_ref[...]) + ba_ref[...]
            la = (jnp.minimum(zl, 0.0) - jnp.log1p(jnp.exp(-jnp.abs(zl)))) / GLA_TAU
            at = pl.ds(pl.multiple_of(base + s, rows), rows)
            dec_scr[at, :kw] = _dot_exact_lhs(tri_f, la[:, :kw])
            dec_scr[at, kw:] = _dot_exact_lhs(tri_b, la[:, kw:])

    def pair(ref, sf, sb, base, st):
        blk_f, blk_b = ref[0, pl.ds(sf, c), :], ref[0, pl.ds(sb, c), :]
        blk = jnp.concatenate([blk_f, blk_b], axis=0)
        q = blk[:, 0:kw] * (GLA_DK ** -0.5)
        k = blk[:, kw:2 * kw]
        vb = blk[:, 2 * kw:2 * kw + vw].astype(BF16)
        b = jnp.concatenate([dec_scr[pl.ds(pl.multiple_of(base + sf, c), c), :kw],
                             dec_scr[pl.ds(pl.multiple_of(base + sb, c), c), kw:]], axis=0)
        b_end = jnp.where(fwd_rows, b[c - 1:c, :], b[c:c + 1, :])
        q_dec = q * jnp.exp(b)
        k_inv = (k * jnp.exp(-b)).astype(BF16)
        k_tail = (k * jnp.exp(b_end - b)).astype(BF16)
        qs = jnp.concatenate([q_dec * k_head[h] for h in range(nh)], axis=0).astype(BF16)
        att = lax.dot_general(qs, k_inv, NT_DIMS, preferred_element_type=F32)
        att = jnp.where(causal, att, 0.0).astype(BF16)
        o_all = _dot(att, vb)
        o = o_all[0:2 * c] * v_head[0]
        for h in range(1, nh):
            o = o + o_all[h * 2 * c:(h + 1) * 2 * c] * v_head[h]
        both = lax.dot_general(q_dec.astype(BF16), st.astype(BF16), NT_DIMS, preferred_element_type=F32)
        o = o + jnp.where(fwd_rows, both[:, :vw], both[:, vw:])
        v2 = jnp.concatenate([jnp.where(fwd_rows, vb, 0), jnp.where(fwd_rows, 0, vb)], axis=1)
        upd = lax.dot_general(v2, k_tail, TN_DIMS, preferred_element_type=F32)
        decay = jnp.where(st_fwd, jnp.exp(b[c - 1:c, :]), jnp.exp(b[c:c + 1, :]))
        return o, st * decay + upd * st_mask

    def scan(ref, n_chunks, base):
        unroll = GLA_UNROLL if n_chunks % GLA_UNROLL == 0 else 1

        @pl.loop(0, n_chunks // unroll)
        def _(it):
            st = st_scr[...]
            for u in range(unroll):
                n = it * unroll + u
                sf = pl.multiple_of(n * c, c)
                sb = pl.multiple_of((n_chunks - 1 - n) * c, c)
                o, st = pair(ref, sf, sb, base, st)
                of_scr[pl.ds(pl.multiple_of(base + sf, c), c), :] = o[:c]
                ob_scr[pl.ds(pl.multiple_of(base + sb, c), c), :] = o[c:]
            st_scr[...] = st

    def finish(ref, out_ref, n_rows, base):
        rows = GLA_FINISH_ROWS if n_rows % GLA_FINISH_ROWS == 0 else c

        @pl.loop(0, n_rows // rows)
        def _(t):
            s = pl.multiple_of(t * rows, rows)
            at = pl.ds(pl.multiple_of(base + s, rows), rows)
            o = of_scr[at, :] + ob_scr[at, :]
            ms = _dot_exact_rhs(o * o, head_mean)
            gate = ref[0, pl.ds(s, rows), 2 * kw + vw:2 * kw + 2 * vw]
            out_ref[0, pl.ds(s, rows), :] = (o * lax.rsqrt(ms + EPS) * g_ref[...] * _silu(gate)).astype(BF16)

    st_scr[...] = jnp.zeros_like(st_scr)
    decays(zc_ref, ctx_len, 0)
    decays(z_ref, seq, ctx_len)
    scan(zc_ref, ctx_len // c, 0)
    scan(z_ref, seq // c, ctx_len)
    finish(zc_ref, oc_ref, ctx_len, 0)
    finish(z_ref, o_ref, seq, ctx_len)


def gla_mixer(zg, zgc, w_a2, b_a, norm_g):
    bsz, seq, wd = zg.shape
    ctx_len = zgc.shape[1]
    vw = GLA_HEADS * GLA_DV
    kw = GLA_HEADS * GLA_DK
    wa = jnp.pad(w_a2, ((0, 0), (0, 128 - GLA_LOWRANK), (0, 0)))
    zero = jnp.zeros_like(wa[0])
    wa = jnp.block([[wa[0], zero], [zero, wa[1]]]).astype(BF16)
    full = lambda shape: pl.BlockSpec(shape, lambda b: (0,) * len(shape))
    return pl.pallas_call(
        functools.partial(_gla_kernel, seq=seq, ctx_len=ctx_len), grid=(bsz,),
        in_specs=[pl.BlockSpec((1, seq, wd), lambda b: (b, 0, 0)),
                  pl.BlockSpec((1, ctx_len, wd), lambda b: (b, 0, 0)),
                  full((256, 2 * kw)), full((1, 2 * kw)), full((1, vw))],
        out_specs=[pl.BlockSpec((1, seq, vw), lambda b: (b, 0, 0)),
                   pl.BlockSpec((1, ctx_len, vw), lambda b: (b, 0, 0))],
        out_shape=[jax.ShapeDtypeStruct((bsz, seq, vw), BF16), jax.ShapeDtypeStruct((bsz, ctx_len, vw), BF16)],
        scratch_shapes=[pltpu.VMEM((seq + ctx_len, vw), F32), pltpu.VMEM((seq + ctx_len, vw), F32),
                        pltpu.VMEM((seq + ctx_len, 2 * kw), F32), pltpu.VMEM((2 * vw, kw), F32)],
        compiler_params=_params("parallel"), name="gla_mixer")(
            zg, zgc, wa, b_a.reshape(1, 2 * kw), norm_g.reshape(1, vw))


def _conv_kernel(u_ref, dw_ref, dwb_ref, lng_ref, lnb_ref, pw_ref, pwb_ref, o_ref, h_scr, *, seq, rows):
    halo = CONV_HALO
    h_scr[0:halo, :] = jnp.zeros((halo, CONV_C), F32)
    h_scr[halo + seq:2 * halo + seq, :] = jnp.zeros((halo, CONV_C), F32)

    @pl.loop(0, seq // rows)
    def _(t):
        s = pl.multiple_of(t * rows, rows)
        u = u_ref[0, pl.ds(s, rows), :]
        h_scr[pl.ds(pl.multiple_of(halo + s, 8), rows), :] = u[:, :CONV_C] * jax.nn.sigmoid(u[:, CONV_C:])

    @pl.loop(0, seq // rows)
    def _(t):
        s = pl.multiple_of(t * rows, rows)
        win = h_scr[pl.ds(s, rows + 2 * halo), :]
        first = halo - CONV_K // 2
        acc = jnp.broadcast_to(dwb_ref[...], (rows, CONV_C))
        for r in range(8):
            taps = [k for k in range(CONV_K) if (first + k) % 8 == r]
            shifted = win if r == 0 else pltpu.roll(win, win.shape[0] - r, axis=0)
            for k in taps:
                at = 8 * ((first + k) // 8)
                acc = acc + shifted[at:at + rows] * dw_ref[k:k + 1, :]
        mu = jnp.mean(acc, axis=-1, keepdims=True)
        var = jnp.mean(jnp.square(acc - mu), axis=-1, keepdims=True)
        y = _silu((acc - mu) * lax.rsqrt(var + EPS) * lng_ref[...] + lnb_ref[...])
        o_ref[0, pl.ds(s, rows), :] = (_dot(y.astype(BF16), pw_ref[...]) + pwb_ref[...]).astype(BF16)


def conformer_conv(u, dw_w, dw_b, ln_g, ln_b, pw_w, pw_b):
    bsz, seq, _ = u.shape
    rows = _tile(seq, 128)
    full = lambda shape: pl.BlockSpec(shape, lambda b: (0,) * len(shape))
    row = lambda t: t.reshape(1, CONV_C)
    return pl.pallas_call(
        functools.partial(_conv_kernel, seq=seq, rows=rows), grid=(bsz,),
        in_specs=[pl.BlockSpec((1, seq, 2 * CONV_C), lambda b: (b, 0, 0)),
                  full((CONV_K + 1, CONV_C)), full((1, CONV_C)), full((1, CONV_C)), full((1, CONV_C)),
                  full((CONV_C, CONV_C)), full((1, CONV_C))],
        out_specs=pl.BlockSpec((1, seq, CONV_C), lambda b: (b, 0, 0)),
        out_shape=jax.ShapeDtypeStruct((bsz, seq, CONV_C), BF16),
        scratch_shapes=[pltpu.VMEM((seq + 2 * CONV_HALO, CONV_C), F32)],
        compiler_params=_params("parallel"), name="conformer_conv")(
            u, jnp.pad(dw_w, ((0, 1), (0, 0))), row(dw_b), row(ln_g), row(ln_b), pw_w.astype(BF16), row(pw_b))


def s5_matrices(lam_re, lam_im, log_dt, b_re, b_im, c_re, c_im):
    dt = jnp.exp(log_dt)[:, None]
    mag = jnp.exp(lam_re * dt)
    a_re, a_im = mag * jnp.cos(lam_im * dt), mag * jnp.sin(lam_im * dt)
    den = lam_re * lam_re + lam_im * lam_im
    f_re = ((a_re - 1.0) * lam_re + a_im * lam_im) / den
    f_im = (a_im * lam_re - (a_re - 1.0) * lam_im) / den
    bb_re = f_re[..., None] * b_re - f_im[..., None] * b_im
    bb_im = f_re[..., None] * b_im + f_im[..., None] * b_re
    eye = jnp.eye(S5_NG, dtype=F32)
    blk_b = lambda t: jnp.einsum("gph,gk->ghkp", t, eye).reshape(S5_C, S5_STATE)
    blk_c = lambda t: jnp.einsum("ghp,gk->gpkh", t, eye).reshape(S5_STATE, S5_C)
    a = jnp.stack([a_re.reshape(S5_STATE), a_im.reshape(S5_STATE)])
    b_mat = jnp.concatenate([blk_b(bb_re), blk_b(bb_im)], axis=1).astype(BF16)
    c_mat = jnp.concatenate([blk_c(c_re), -blk_c(c_im)], axis=0).astype(BF16)
    return a, b_mat, c_mat


def _s5_scan_kernel(uf_ref, ub_ref, a_ref, b_ref, c_ref, h0_ref, yf_ref, yb_ref, hl_ref, hs_f, hs_b, st_scr,
                    *, steps, bsz):
    ns = S5_STATE

    @pl.when(pl.program_id(0) == 0)
    def _():
        st_scr[...] = h0_ref[...]

    for d, (u_ref, hs) in enumerate(((uf_ref, hs_f), (ub_ref, hs_b))):
        u = u_ref[...].reshape(steps * bsz, S5_C).astype(BF16)
        hs[...] = _dot(u, b_ref[d])
    for d, hs in enumerate((hs_f, hs_b)):
        for s0 in range(0, ns, S5_STRIP):
            re, im = slice(s0, s0 + S5_STRIP), slice(ns + s0, ns + s0 + S5_STRIP)
            a_re, a_im = a_ref[d, 0:1, re], a_ref[d, 1:2, re]
            h_re, h_im = st_scr[d, :, re], st_scr[d, :, im]
            for j in range(steps):
                rows = pl.ds((steps - 1 - j if d == 1 else j) * bsz, bsz)
                h_re, h_im = (a_re * h_re - a_im * h_im + hs[rows, re], a_re * h_im + a_im * h_re + hs[rows, im])
                hs[rows, re] = h_re
                hs[rows, im] = h_im
            st_scr[d, :, re] = h_re
            st_scr[d, :, im] = h_im
    yf_ref[...] = _dot(hs_f[...].astype(BF16), c_ref[0]).reshape(steps, bsz, S5_C)
    yb_ref[...] = _dot(hs_b[...].astype(BF16), c_ref[1]).reshape(steps, bsz, S5_C)
    hl_ref[...] = st_scr[...]


def s5_scan(u_tm, a, b_mat, c_mat, h0):
    seq, bsz, _ = u_tm.shape
    steps = _tile(seq, 32)
    n = seq // steps
    full = lambda t: pl.BlockSpec(t.shape, lambda i: (0,) * t.ndim)
    fwd = pl.BlockSpec((steps, bsz, S5_C), lambda i: (i, 0, 0))
    bwd = pl.BlockSpec((steps, bsz, S5_C), lambda i: (n - 1 - i, 0, 0))
    y = jax.ShapeDtypeStruct((seq, bsz, S5_C), F32)
    return pl.pallas_call(
        functools.partial(_s5_scan_kernel, steps=steps, bsz=bsz), grid=(n,),
        in_specs=[fwd, bwd, full(a), full(b_mat), full(c_mat), full(h0)],
        out_specs=[fwd, bwd, full(h0)],
        out_shape=[y, y, jax.ShapeDtypeStruct(h0.shape, F32)],
        scratch_shapes=[pltpu.VMEM((steps * bsz, 2 * S5_STATE), F32), pltpu.VMEM((steps * bsz, 2 * S5_STATE), F32),
                        pltpu.VMEM(h0.shape, F32)],
        compiler_params=_params("arbitrary"), name="s5_scan")(u_tm, u_tm, a, b_mat, c_mat, h0)


def _s5_out_kernel(u_ref, yf_ref, yb_ref, d_ref, w_ref, b_ref, o_ref):
    y = d_ref[...] * u_ref[...] + yf_ref[...] + yb_ref[...]
    z = _dot(y.astype(BF16), w_ref[...]) + b_ref[...]
    o_ref[...] = (z[:, :S5_C] * jax.nn.sigmoid(z[:, S5_C:])).astype(BF16)


def s5_output(u, y_f, y_b, d_skip, glu_w, glu_b):
    n = u.shape[0]
    tm = _tile(n, 1024)
    rows = pl.BlockSpec((tm, S5_C), lambda i: (i, 0))
    full = lambda shape: pl.BlockSpec(shape, lambda i: (0,) * len(shape))
    return pl.pallas_call(
        _s5_out_kernel, grid=(n // tm,),
        in_specs=[rows, rows, rows, full((1, S5_C)), full((S5_C, 2 * S5_C)), full((1, 2 * S5_C))],
        out_specs=rows, out_shape=jax.ShapeDtypeStruct((n, S5_C), BF16),
        compiler_params=_params("parallel"), name="s5_output")(
            u, y_f, y_b, d_skip.reshape(1, S5_C), glu_w.astype(BF16), glu_b.reshape(1, 2 * S5_C))


def s5_mixer(zs, zsc, lam_re, lam_im, log_dt, b_re, b_im, c_re, c_im, d_skip, glu_w, glu_b):
    bsz, seq, _ = zs.shape
    ctx_len = zsc.shape[1]
    u = jnp.transpose(zs, (1, 0, 2))
    uc = jnp.transpose(zsc, (1, 0, 2))
    mats = [s5_matrices(lam_re[d], lam_im[d], log_dt[d], b_re[d], b_im[d], c_re[d], c_im[d]) for d in range(2)]
    a, b_mat, c_mat = [jnp.stack(t) for t in zip(*mats)]
    yc_f, yc_b, hc = s5_scan(uc, a, b_mat, c_mat, jnp.zeros((2, bsz, 2 * S5_STATE), F32))
    y_f, y_b, _ = s5_scan(u, a, b_mat, c_mat, hc)
    ys, ycs = (y_f, y_b), (yc_f, yc_b)
    flat = lambda t: t.reshape(-1, S5_C)
    o = s5_output(flat(u), flat(ys[0]), flat(ys[1]), d_skip, glu_w, glu_b).reshape(seq, bsz, S5_C)
    oc = s5_output(flat(uc), flat(ycs[0]), flat(ycs[1]), d_skip, glu_w, glu_b).reshape(ctx_len, bsz, S5_C)
    return jnp.transpose(o, (1, 0, 2)), jnp.transpose(oc, (1, 0, 2))


def rope_tables(seq_len, rotate):
    if not rotate:
        return jnp.ones((seq_len, MLA_ROPE), F32), jnp.zeros((seq_len, MLA_ROPE), F32)
    rows = seq_len // GRID_W
    row = jnp.broadcast_to(jnp.arange(rows, dtype=F32)[:, None], (rows, GRID_W)).reshape(seq_len)
    col = jnp.broadcast_to(jnp.arange(GRID_W, dtype=F32)[None, :], (rows, GRID_W)).reshape(seq_len)
    inv_freq = ROPE_BASE ** (-jnp.arange(ROPE_AXIS // 2, dtype=F32) / (ROPE_AXIS // 2))
    ang = jnp.stack([row[:, None] * inv_freq, col[:, None] * inv_freq], axis=1)
    full = lambda t: jnp.broadcast_to(t[:, :, None, :], (seq_len, 2, 2, ROPE_AXIS // 2)).reshape(seq_len, MLA_ROPE)
    return full(jnp.cos(ang)), full(jnp.sin(ang))


def mla_tables(seq_len, rotate):
    cos, sin = rope_tables(seq_len, rotate)
    head = lambda rope, fill: jnp.concatenate(
        [jnp.full((seq_len, MLA_NOPE), fill, F32), rope, jnp.zeros((seq_len, MLA_HEAD_PAD - MLA_NOPE - MLA_ROPE), F32)], 1)
    cos_q = jnp.tile(head(cos, 1.0), (1, MLA_HEADS))
    sin_q = jnp.tile(head(sin, 0.0), (1, MLA_HEADS))
    cs_k = jnp.concatenate([cos, sin, jnp.zeros((seq_len, 128 - 2 * MLA_ROPE), F32)], 1)
    return cos_q, sin_q, cs_k


def mla_weights(w_uq, w_ukv):
    hd = MLA_NOPE + MLA_ROPE
    zq = jnp.zeros((MLA_Q_LORA, MLA_HEAD_PAD - hd), F32)
    zn = jnp.zeros((MLA_Q_LORA, MLA_NOPE), F32)
    wq, wq_sw, wk, wv = [], [], [], []
    for h in range(MLA_HEADS):
        qh = w_uq[:, h * hd:(h + 1) * hd]
        wq += [qh, zq]
        wq_sw += [zn, _rope_swap(qh[:, MLA_NOPE:]), zq]
        kvh = w_ukv[:, h * (MLA_NOPE + MLA_V):(h + 1) * (MLA_NOPE + MLA_V)]
        wk += [kvh[:, :MLA_NOPE], jnp.zeros((MLA_KV_LORA, MLA_HEAD_PAD - MLA_NOPE), F32)]
        wv += [kvh[:, MLA_NOPE:]]
    place = np.zeros((128, MLA_HEADS * MLA_HEAD_PAD), np.float32)
    for h in range(MLA_HEADS):
        for j in range(MLA_ROPE):
            place[j, h * MLA_HEAD_PAD + MLA_NOPE + j] = 1.0
            place[MLA_ROPE + j, h * MLA_HEAD_PAD + MLA_NOPE + j] = 1.0
    cat = lambda ts: jnp.concatenate(ts, axis=1).astype(BF16)
    return cat(wq), cat(wq_sw), cat(wk), cat(wv), jnp.asarray(place).astype(BF16)


def _mla_prep_kernel(z_ref, qg_ref, kg_ref, wq_ref, wqs_ref, wk_ref, wv_ref, pl_ref, cq_ref, sq_ref, csk_ref,
                     q_ref, k_ref, v_ref):
    z = z_ref[0]
    norm = lambda t, g: (t * lax.rsqrt(jnp.mean(t * t, axis=-1, keepdims=True) + EPS) * g).astype(BF16)
    cq = norm(z[:, :MLA_Q_LORA], qg_ref[...])
    ckv = norm(z[:, MLA_Q_LORA:MLA_Q_LORA + MLA_KV_LORA], kg_ref[...])
    scale = (MLA_NOPE + MLA_ROPE) ** -0.5
    q = _dot(cq, wq_ref[...]) * cq_ref[...] + _dot(cq, wqs_ref[...]) * sq_ref[...]
    q_ref[0] = (q * scale).astype(BF16)
    kr = z[:, MLA_Q_LORA + MLA_KV_LORA:] * csk_ref[...]
    k_ref[0] = (_dot(ckv, wk_ref[...]) + _dot_exact_rhs(kr, pl_ref[...])).astype(BF16)
    v_ref[0] = _dot(ckv, wv_ref[...]).astype(BF16)


def mla_prep(zm, qn_g, kvn_g, weights, tables):
    bsz, seq, wd = zm.shape
    tm = _tile(seq, 1024)
    wq, wq_sw, wk, wv, place = weights
    cos_q, sin_q, cs_k = tables
    qw, vw = MLA_HEADS * MLA_HEAD_PAD, MLA_HEADS * MLA_V
    full = lambda t: pl.BlockSpec(t.shape, lambda b, i: (0,) * t.ndim)
    pos = lambda t: pl.BlockSpec((tm, t.shape[1]), lambda b, i: (i, 0))
    out = lambda w: pl.BlockSpec((1, tm, w), lambda b, i: (b, i, 0))
    qg, kg = qn_g.reshape(1, -1), kvn_g.reshape(1, -1)
    return pl.pallas_call(
        _mla_prep_kernel, grid=(bsz, seq // tm),
        in_specs=[pl.BlockSpec((1, tm, wd), lambda b, i: (b, i, 0)), full(qg), full(kg), full(wq), full(wq_sw),
                  full(wk), full(wv), full(place), pos(cos_q), pos(sin_q), pos(cs_k)],
        out_specs=[out(qw), out(qw), out(vw)],
        out_shape=[jax.ShapeDtypeStruct((bsz, seq, qw), BF16), jax.ShapeDtypeStruct((bsz, seq, qw), BF16),
                   jax.ShapeDtypeStruct((bsz, seq, vw), BF16)],
        compiler_params=_params("parallel", "parallel"), name="mla_prep")(
            zm, qg, kg, wq, wq_sw, wk, wv, place, cos_q, sin_q, cs_k)


def _mla_attn_kernel(q_ref, *refs, n_seg):
    k_refs, v_refs, o_ref = refs[:n_seg], refs[n_seg:2 * n_seg], refs[2 * n_seg]
    outs = []
    for h in range(MLA_HEADS):
        q = q_ref[0, :, h * MLA_HEAD_PAD:(h + 1) * MLA_HEAD_PAD]
        s = [lax.dot_general(q, k_ref[0, :, h * MLA_HEAD_PAD:(h + 1) * MLA_HEAD_PAD], NT_DIMS,
                             preferred_element_type=F32) for k_ref in k_refs]
        m = functools.reduce(jnp.maximum, [jnp.max(t, axis=-1, keepdims=True) for t in s])
        p = [jnp.exp(t - m) for t in s]
        den = functools.reduce(jnp.add, [jnp.sum(t, axis=-1, keepdims=True) for t in p])
        o = functools.reduce(jnp.add, [_dot(t.astype(BF16), v_ref[0, :, h * MLA_V:(h + 1) * MLA_V])
                                       for t, v_ref in zip(p, v_refs)])
        outs.append(o / den)
    o_ref[0] = jnp.concatenate(outs, axis=-1).astype(BF16)


def mla_attention(q, ks, vs):
    bsz, seq, qw = q.shape
    tq = _tile(seq, 1024)
    vw = MLA_HEADS * MLA_V
    seg = lambda t: pl.BlockSpec((1,) + t.shape[1:], lambda b, i: (b, 0, 0))
    return pl.pallas_call(
        functools.partial(_mla_attn_kernel, n_seg=len(ks)), grid=(bsz, seq // tq),
        in_specs=[pl.BlockSpec((1, tq, qw), lambda b, i: (b, i, 0))] + [seg(t) for t in ks] + [seg(t) for t in vs],
        out_specs=pl.BlockSpec((1, tq, vw), lambda b, i: (b, i, 0)),
        out_shape=jax.ShapeDtypeStruct((bsz, seq, vw), BF16),
        compiler_params=_params("parallel", "parallel"), name="mla_attention")(q, *ks, *vs)


def _outproj_kernel(a_ref, b_ref, c_ref, d_ref, w_ref, x_ref, g1_ref, sh_ref, sc_ref, g_ref, rw_ref,
                    xo_ref, h_ref, lg_ref):
    mix = None
    for j, o_ref in enumerate((a_ref, b_ref, c_ref, d_ref)):
        wd = o_ref.shape[-1]
        part = _dot(o_ref[0], w_ref[j * wd:(j + 1) * wd, :])
        mix = part if mix is None else mix + part
    x = x_ref[0] + g1_ref[0] * mix
    xo_ref[0] = x
    ms = jnp.mean(x * x, axis=-1, keepdims=True)
    h = x * lax.rsqrt(ms + EPS) * g_ref[...] * (1.0 + sc_ref[0]) + sh_ref[0]
    _store_row_slabs(h_ref, _pack_bf16_pairs(h))
    lg_ref[...] =lax.dot_general(rw_ref[...], h, NT_DIMS, precision=HIGHEST, preferred_element_type=F32)


def out_projection(parts, w_out, x, g1, shift, scale, g, router_w):
    bsz, seq, dm = x.shape
    tm = _tile(seq, 1024)
    nt = seq // tm
    ne = router_w.shape[1]
    groups = dm // 2 // 128
    rows = lambda w: pl.BlockSpec((1, tm, w), lambda b, i: (b, i, 0))
    vec = pl.BlockSpec((1, 1, dm), lambda b, i: (b, 0, 0))
    full = lambda shape: pl.BlockSpec(shape, lambda b, i: (0,) * len(shape))
    return pl.pallas_call(
        _outproj_kernel, grid=(bsz, nt),
        in_specs=[rows(p.shape[-1]) for p in parts] + [full(w_out.shape), rows(dm), vec, vec, vec, full((1, dm)),
                                                       full((ne, dm))],
        out_specs=[rows(dm), pl.BlockSpec((tm * groups, 128), lambda b, i: (b * nt + i, 0)),
                   pl.BlockSpec((ne, tm), lambda b, i: (0, b * nt + i))],
        out_shape=[jax.ShapeDtypeStruct((bsz, seq, dm), F32), jax.ShapeDtypeStruct((bsz * seq * groups, 128), jnp.int32),
                   jax.ShapeDtypeStruct((ne, bsz * seq), F32)],
        compiler_params=_params("parallel", "parallel"), name="out_projection")(
            *parts, w_out.astype(BF16), x, g1, shift, scale, g.reshape(1, dm), router_w.T)


def _route_kernel(lg_ref, b_ref, e_ref, w_ref):
    ne, tt = lg_ref.shape
    per = ne // N_GROUPS
    neg = -jnp.inf
    scores = jax.nn.sigmoid(lg_ref[...])
    biased = scores + b_ref[...]
    v3 = biased.reshape(N_GROUPS, per, tt)
    e_in = lax.broadcasted_iota(jnp.int32, v3.shape, 1).astype(F32)
    m1 = jnp.max(v3, axis=1, keepdims=True)
    i1 = jnp.min(jnp.where(v3 == m1, e_in, float(per)), axis=1, keepdims=True)
    m2 = jnp.max(jnp.where(e_in == i1, neg, v3), axis=1, keepdims=True)
    grp = (m1 + m2).reshape(N_GROUPS, tt)

    def pick(cur, count):
        ids = lax.broadcasted_iota(jnp.int32, cur.shape, 0).astype(F32)
        marks = jnp.zeros(cur.shape, F32)
        picked = []
        for _ in range(count):
            m = jnp.max(cur, axis=0, keepdims=True)
            first = jnp.min(jnp.where(cur == m, ids, float(cur.shape[0])), axis=0, keepdims=True)
            hit = ids == first
            marks = jnp.where(hit, 1.0, marks)
            cur = jnp.where(hit, neg, cur)
            picked.append(first)
        return marks, picked

    grp_on, _ = pick(grp, TOPK_GROUPS)
    exp_on = jnp.broadcast_to(grp_on.reshape(N_GROUPS, 1, tt), v3.shape).reshape(ne, tt)
    chosen, picked = pick(jnp.where(exp_on > 0.0, biased, neg), TOP_K)
    w = scores * chosen
    gate = w / jnp.sum(w, axis=0, keepdims=True) * ROUTED_SCALE
    ids = lax.broadcasted_iota(jnp.int32, gate.shape, 0).astype(F32)
    e_ref[...] = jnp.concatenate(picked, axis=0).astype(jnp.int32)
    w_ref[...] = jnp.concatenate([jnp.sum(jnp.where(ids == p, gate, 0.0), axis=0, keepdims=True) for p in picked], axis=0)


def route(logits_t, router_b):
    ne, n = logits_t.shape
    tt = _tile(n, 1024)
    out = pl.BlockSpec((TOP_K, tt), lambda i: (0, i))
    return pl.pallas_call(
        _route_kernel, grid=(n // tt,),
        in_specs=[pl.BlockSpec((ne, tt), lambda i: (0, i)), pl.BlockSpec((ne, 1), lambda i: (0, 0))],
        out_specs=[out, out],
        out_shape=[jax.ShapeDtypeStruct((TOP_K, n), jnp.int32), jax.ShapeDtypeStruct((TOP_K, n), F32)],
        compiler_params=_params("parallel"), name="route")(logits_t, router_b.reshape(ne, 1))


def moe_plan(eid, wgt, chunk, xg, yg):
    n = eid.shape[1]
    nc, na = n // chunk, chunk * TOP_K
    ns = na // MOE_ROWS + N_EXPERTS
    key = eid.T.reshape(nc, na) * na + jnp.arange(na, dtype=jnp.int32)
    skey, sw = lax.sort((key, wgt.T.reshape(nc, na)), dimension=1, num_keys=1)
    e_sorted, tok = skey // na, (skey % na) // TOP_K
    experts = jnp.arange(N_EXPERTS, dtype=jnp.int32)
    ends = jnp.sum(e_sorted[:, None, :] <= experts[None, :, None], axis=-1, dtype=jnp.int32)
    starts = jnp.concatenate([jnp.zeros((nc, 1), jnp.int32), ends[:, :-1]], axis=1)
    pad_end = jnp.cumsum(-(-(ends - starts) // MOE_ROWS) * MOE_ROWS, axis=1)
    pad_start = jnp.concatenate([jnp.zeros((nc, 1), jnp.int32), pad_end[:, :-1]], axis=1)
    first = jnp.arange(ns, dtype=jnp.int32) * MOE_ROWS
    exp = jnp.minimum(jnp.sum(pad_end[:, None, :] <= first[None, :, None], axis=-1, dtype=jnp.int32), N_EXPERTS - 1)
    active = first[None, :] < pad_end[:, -1:]
    take = lambda t, i: jnp.take_along_axis(t, i, axis=1)
    rank = first[None, :, None] + jnp.arange(MOE_ROWS, dtype=jnp.int32) - take(pad_start, exp)[..., None]
    real = (rank < take(ends - starts, exp)[..., None]) & active[..., None]
    src = jnp.clip(take(starts, exp)[..., None] + rank, 0, na - 1).reshape(nc, ns * MOE_ROWS)
    rows = lambda t: take(t, src).reshape(nc, ns, MOE_ROWS)
    flat = lambda t: t.reshape(nc * ns, 1, MOE_ROWS)
    gather_at = flat(jnp.where(real, rows(tok), 0) * xg)
    scatter_at = flat(jnp.where(real, rows(tok), chunk) * yg)
    weight = flat(jnp.where(real, rows(sw), 0.0))
    return gather_at, scatter_at, weight, exp.reshape(nc * ns), active.astype(jnp.int32).reshape(nc * ns)


def _moe_tick(x_ref, y_ref, idx_ref, w_ref, tick, wgu_ref, wd_ref, gather_to, mm_from, mm_to, scatter_from):
    bm = MOE_ROWS
    stride = bm + 1
    xg = mm_from.shape[0] // stride
    yg = mm_to.shape[0] // stride
    for mi in range(bm):
        src = pl.multiple_of(idx_ref[0, 0, 2 * tick * bm + mi], xg)
        gather_to[pl.ds(mi, xg, stride=stride), :] = x_ref[0, pl.ds(src, xg), :]
    low, high = _unpack_bf16_pairs([mm_from[j * stride:j * stride + bm, :] for j in range(xg)])
    half = low.shape[1]
    ff = wd_ref.shape[1]
    up = _dot(low, wgu_ref[0, :half, :]) + _dot(high, wgu_ref[0, half:, :])
    act = (_silu(up[:, :ff]) * up[:, ff:]).astype(BF16)
    out = _dot(act, wd_ref[0])
    for j in range(yg):
        mm_to[j * stride:j * stride + bm, :] = out[:, j * 128:(j + 1) * 128]
    for base in range(0, bm, MOE_BATCH):
        at = [pl.multiple_of(idx_ref[0, 0, (2 * tick + 1) * bm + mi], yg) for mi in range(base, base + MOE_BATCH)]
        new = [y_ref[0, pl.ds(i, yg), :] + w_ref[0, 0, tick * bm + mi] * scatter_from[pl.ds(mi, yg, stride=stride), :]
               for i, mi in zip(at, range(base, base + MOE_BATCH))]
        for i, v in zip(at, new):
            y_ref[0, pl.ds(i, yg), :] = v


def _moe_routed_kernel(exp_ref, act_ref, x_ref, idx_ref, w_ref, wgua_ref, wda_ref, wgub_ref, wdb_ref,
                       y_ref, xt0, xt1, ot0, ot1, *, ns):
    c, g = pl.program_id(0), pl.program_id(1)

    @pl.when(g == 0)
    def _():
        y_ref[...] = jnp.zeros_like(y_ref)

    @pl.when((c == 0) & (g == 0))
    def _():
        xt1[...] = jnp.zeros_like(xt1)
        ot0[...] = jnp.zeros_like(ot0)
        ot1[...] = jnp.zeros_like(ot1)

    def live(step):
        return (step >= 0) & (step < ns) & (act_ref[c * ns + jnp.clip(step, 0, ns - 1)] != 0)

    t = 2 * g

    @pl.when(live(t - 2) | live(t - 1) | live(t) | live(t + 1))
    def _():
        _moe_tick(x_ref, y_ref, idx_ref, w_ref, 0, wgua_ref, wda_ref,
                  gather_to=xt0, mm_from=xt1, mm_to=ot1, scatter_from=ot0)
        _moe_tick(x_ref, y_ref, idx_ref, w_ref, 1, wgub_ref, wdb_ref,
                  gather_to=xt1, mm_from=xt0, mm_to=ot0, scatter_from=ot1)


def moe_routed(slabs, eid, wgt, wgu, wd):
    n = eid.shape[1]
    xg = slabs.shape[0] // n
    yg = 2 * xg
    dm, ff = wd.shape[2], wd.shape[1]
    chunk = _tile(n, MOE_CHUNK)
    nc = n // chunk
    ns = chunk * TOP_K // MOE_ROWS + N_EXPERTS
    assert ns % 2 == 0
    steps = (ns + 2) // 2
    last = ns - 1
    gather_at, scatter_at, weight, exp, active = moe_plan(eid, wgt, chunk, xg, yg)
    tick = jnp.arange(steps, dtype=jnp.int32)[:, None] * 2 + jnp.arange(2, dtype=jnp.int32)[None, :]
    gathered = jnp.minimum(tick, last)
    scattered = jnp.where(tick >= 2, tick - 2, last)
    per_step = lambda t, at: jnp.take(t.reshape(nc, ns, MOE_ROWS), at, axis=1)
    idx = jnp.concatenate([per_step(gather_at, gathered[:, 0]), per_step(scatter_at, scattered[:, 0]),
                           per_step(gather_at, gathered[:, 1]), per_step(scatter_at, scattered[:, 1])], axis=-1)
    wts = jnp.concatenate([per_step(weight, scattered[:, 0]), per_step(weight, scattered[:, 1])], axis=-1)
    smem = lambda width: pl.BlockSpec((1, 1, width), lambda c, g, e, a: (c * steps + g, 0, 0), memory_space=pltpu.SMEM)
    computed = lambda tk: lambda g: jnp.clip(2 * g + tk - 1, 0, last)
    expert = lambda shape, step: pl.BlockSpec((1,) + shape, lambda c, g, e, a: (e[c * ns + step(g)], 0, 0))
    once = pl.Buffered(1)
    stage = lambda groups, dtype: pltpu.VMEM((-(-groups * (MOE_ROWS + 1) // 8) * 8, 128), dtype)
    return pl.pallas_call(
        functools.partial(_moe_routed_kernel, ns=ns),
        grid_spec=pltpu.PrefetchScalarGridSpec(
            num_scalar_prefetch=2, grid=(nc, steps),
            in_specs=[pl.BlockSpec((1, chunk * xg, 128), lambda c, g, e, a: (c, 0, 0), pipeline_mode=once),
                      smem(4 * MOE_ROWS), smem(2 * MOE_ROWS),
                      expert((dm, 2 * ff), computed(0)), expert((ff, dm), computed(0)),
                      expert((dm, 2 * ff), computed(1)), expert((ff, dm), computed(1))],
            out_specs=pl.BlockSpec((1, chunk * yg + 8, 128), lambda c, g, e, a: (c, 0, 0), pipeline_mode=once),
            scratch_shapes=[stage(xg, jnp.int32), stage(xg, jnp.int32), stage(yg, F32), stage(yg, F32)]),
        out_shape=jax.ShapeDtypeStruct((nc, chunk * yg + 8, 128), F32),
        compiler_params=_params("arbitrary", "arbitrary"), name="moe_routed")(
            exp, active, slabs.reshape(nc, chunk * xg, 128), idx.reshape(nc * steps, 1, 4 * MOE_ROWS),
            wts.reshape(nc * steps, 1, 2 * MOE_ROWS), wgu, wd, wgu, wd)


def _moe_finish_kernel(y_ref, h_ref, x_ref, g2_ref, wg_ref, wu_ref, wd_ref, fg_ref, o_ref, *, final_norm):
    tm = x_ref.shape[0]
    low, high = _unpack_bf16_pairs(_load_row_slabs(h_ref, tm, h_ref.shape[0] // tm))
    routed = jnp.concatenate(_load_row_slabs(y_ref, tm, y_ref.shape[1] // tm, lead=(0,)), axis=1)
    half = low.shape[1]
    up = lambda w_ref: _dot(low, w_ref[:half, :]) + _dot(high, w_ref[half:, :])
    shared = _dot((_silu(up(wg_ref)) * up(wu_ref)).astype(BF16), wd_ref[...])
    x = x_ref[...] + g2_ref[0] * (routed + shared)
    if final_norm:
        x = x * lax.rsqrt(jnp.mean(x * x, axis=-1, keepdims=True) + EPS) * fg_ref[...]
    o_ref[...] = x


def moe_finish(y, slabs, x, g2, sg, su, sd, final_g, final_norm):
    bsz, seq, dm = x.shape
    n = bsz * seq
    tm = _tile(seq, 1024)
    nt = seq // tm
    xg, yg = slabs.shape[0] // n, dm // 128
    per_chunk = (y.shape[1] - 8) // yg // tm
    rows = lambda w: pl.BlockSpec((tm, w), lambda i: (i, 0))
    full = lambda t: pl.BlockSpec(t.shape, lambda i: (0,) * t.ndim)
    sg, su, sd, fg = sg.astype(BF16), su.astype(BF16), sd.astype(BF16), final_g.reshape(1, dm)
    out = pl.pallas_call(
        functools.partial(_moe_finish_kernel, final_norm=final_norm), grid=(n // tm,),
        in_specs=[pl.BlockSpec((1, tm * yg, 128), lambda i: (i // per_chunk, i % per_chunk, 0)),
                  pl.BlockSpec((tm * xg, 128), lambda i: (i, 0)), rows(dm),
                  pl.BlockSpec((1, 1, dm), lambda i: (i // nt, 0, 0)), full(sg), full(su), full(sd), full(fg)],
        out_specs=rows(dm), out_shape=jax.ShapeDtypeStruct((n, dm), F32),
        compiler_params=_params("parallel"), name="moe_finish")(y, slabs, x.reshape(n, dm), g2, sg, su, sd, fg)
    return out.reshape(bsz, seq, dm)


def kernel(x, c, ctx, c_ctx, ada_w, ada_b, norm1_g, norm2_g, w_in, w_out, gla_w_a2, gla_b_a, gla_norm_g,
           conv_dw_w, conv_dw_b, conv_ln_g, conv_ln_b, conv_pw_w, conv_pw_b,
           s5_lam_re, s5_lam_im, s5_log_dt, s5_b_re, s5_b_im, s5_c_re, s5_c_im, s5_d, s5_glu_w, s5_glu_b,
           mla_qn_g, mla_kvn_g, mla_w_uq, mla_w_ukv,
           moe_router_w, moe_router_b, moe_w_gate, moe_w_up, moe_w_down,
           shared_w_gate, shared_w_up, shared_w_down, final_g):
    bsz, seq, dm = x.shape
    ctx_len = ctx.shape[1]
    depth = ada_w.shape[0]
    cc = jnp.concatenate([c, c_ctx[None], jnp.zeros((-(bsz + 1) % 8, dm), F32)], axis=0)
    tables = mla_tables(seq, True)
    tables_c = mla_tables(ctx_len, False)
    xc = ctx
    for l in range(depth):
        last = l == depth - 1
        mod = ada_mod(cc, ada_w[l], ada_b[l])
        lat = [t.reshape(bsz, 1, dm) for t in jnp.split(mod[:bsz], 6, axis=-1)]
        con = [jnp.broadcast_to(t.reshape(1, 1, dm), (bsz, 1, dm)) for t in jnp.split(mod[bsz], 6)]
        sh1, sc1, g1, sh2, sc2, g2 = lat
        sh1c, sc1c, g1c, sh2c, sc2c, g2c = con
        w_in_l = assemble_w_in(w_in[l])
        zg, zv, zs, zm = in_projection(x, sh1, sc1, norm1_g[l], w_in_l)
        zgc, zvc, zsc, zmc = in_projection(xc, sh1c, sc1c, norm1_g[l], w_in_l)
        conv_p = (conv_dw_w[l], conv_dw_b[l], conv_ln_g[l], conv_ln_b[l], conv_pw_w[l], conv_pw_b[l])
        o_gla, oc_gla = gla_mixer(zg, zgc, gla_w_a2[l], gla_b_a[l], gla_norm_g[l])
        o_conv = conformer_conv(zv, *conv_p)
        o_s5, oc_s5 = s5_mixer(zs, zsc, s5_lam_re[l], s5_lam_im[l], s5_log_dt[l], s5_b_re[l], s5_b_im[l],
                               s5_c_re[l], s5_c_im[l], s5_d[l], s5_glu_w[l], s5_glu_b[l])
        mla_w = mla_weights(mla_w_uq[l], mla_w_ukv[l])
        q, k, v = mla_prep(zm, mla_qn_g[l], mla_kvn_g[l], mla_w, tables)
        qc, kc, vc = mla_prep(zmc, mla_qn_g[l], mla_kvn_g[l], mla_w, tables_c)
        o_mla = mla_attention(q, [k, kc], [v, vc])
        experts = (jnp.concatenate([moe_w_gate[l], moe_w_up[l]], axis=2).astype(BF16), moe_w_down[l].astype(BF16))
        shared = (shared_w_gate[l], shared_w_up[l], shared_w_down[l])

        def ffn(parts, x_in, gate1, shift, scale, gate2, final_norm):
            x_mid, slabs, logits = out_projection(parts, w_out[l], x_in, gate1, shift, scale, norm2_g[l], moe_router_w[l])
            y = moe_routed(slabs, *route(logits, moe_router_b[l]), *experts)
            return moe_finish(y, slabs, x_mid, gate2, *shared, final_g, final_norm)

        if not last:
            oc_conv = conformer_conv(zvc, *conv_p)
            oc_mla = mla_attention(qc, [kc], [vc])
            xc = ffn([oc_gla, oc_conv, oc_s5, oc_mla], xc, g1c, sh2c, sc2c, g2c, False)
        x = ffn([o_gla, o_conv, o_s5, o_mla], x, g1, sh2, sc2, g2, last)
    return x
```

```python
import functools
import math

import numpy as np
import jax
import jax.numpy as jnp
from jax import lax
from jax.experimental import pallas as pl
from jax.experimental.pallas import tpu as pltpu

F32 = jnp.float32
BF16 = jnp.bfloat16
HIGHEST = lax.Precision.HIGHEST
EPS = 1e-6

GRID_W = 64

GLA_HEADS = 4
GLA_DK = 32
GLA_DV = 64
GLA_LOWRANK = 16
GLA_TAU = 16.0
GLA_CHUNK = 64
GLA_UNROLL = 2
GLA_FINISH_ROWS = 256

CONV_C = 256
CONV_K = 31
CONV_HALO = 16

S5_C = 256
S5_GROUP = 16
S5_NG = 16
S5_P = 64
S5_STATE = S5_NG * S5_P
S5_STRIP = 256

MLA_HEADS = 4
MLA_NOPE = 64
MLA_ROPE = 32
MLA_V = 64
MLA_Q_LORA = 256
MLA_KV_LORA = 128
MLA_HEAD_PAD = 128
ROPE_AXIS = MLA_ROPE // 2
ROPE_BASE = 10000.0

N_EXPERTS = 64
TOP_K = 8
N_GROUPS = 8
TOPK_GROUPS = 4
ROUTED_SCALE = 2.5
MOE_CHUNK = 4096
MOE_ROWS = 256
MOE_BATCH = 8

IN_SPLITS = (128, 128, 256, 256, 16, 16, 512, 256, 256, 128, 32)
W_GLA, W_CONV, W_S5, W_MLA = 1024, 512, 256, 512

VMEM_LIMIT = 48 * 1024 * 1024

NT_DIMS = (((1,), (1,)), ((), ()))
TN_DIMS = (((0,), (0,)), ((), ()))


def _params(*sem, vmem=VMEM_LIMIT):
    return pltpu.CompilerParams(dimension_semantics=sem, vmem_limit_bytes=vmem)


def _silu(v):
    return v * jax.nn.sigmoid(v)


def _dot(a, b, **kw):
    return jnp.dot(a, b, preferred_element_type=F32, **kw)


def _split_bf16(x):
    hi = x.astype(BF16)
    rest = x - hi.astype(F32)
    mid = rest.astype(BF16)
    return hi, mid, (rest - mid.astype(F32)).astype(BF16)


def _dot_exact_lhs(m, x):
    n = x.shape[1]
    prod = _dot(m, jnp.concatenate(_split_bf16(x), axis=1))
    return prod[:, :n] + prod[:, n:2 * n] + prod[:, 2 * n:]


def _dot_exact_rhs(x, m):
    n = x.shape[0]
    prod = _dot(jnp.concatenate(_split_bf16(x), axis=0), m)
    return prod[:n] + prod[n:2 * n] + prod[2 * n:]


def _tile(n, pref):
    return pref if n % pref == 0 else n


def _pack_bf16_pairs(h):
    bits = pltpu.bitcast(h.astype(BF16).astype(F32), jnp.int32)
    w = h.shape[-1] // 2
    return lax.shift_right_logical(bits[:, :w], 16) | bits[:, w:]


def _store_row_slabs(ref, value):
    rows, g = value.shape[0], value.shape[1] // 128
    for j in range(g):
        ref[pl.ds(j, rows, stride=g), :] = value[:, j * 128:(j + 1) * 128]


def _load_row_slabs(ref, rows, g, lead=()):
    return [ref[lead + (pl.ds(j, rows, stride=g), slice(None))] for j in range(g)]


def _unpack_bf16_pairs(words):
    low = jnp.concatenate([pltpu.bitcast(v << 16, F32) for v in words], axis=1)
    high = jnp.concatenate([pltpu.bitcast(v & -65536, F32) for v in words], axis=1)
    return low.astype(BF16), high.astype(BF16)


def _ada_kernel(c_ref, w_ref, b_ref, o_ref):
    o_ref[...] = _dot(_silu(c_ref[...]), w_ref[...], precision=HIGHEST) + b_ref[...]


def ada_mod(cc, w, b):
    rows, dm = cc.shape
    n = w.shape[1]
    tn = _tile(n, 512)
    return pl.pallas_call(
        _ada_kernel, grid=(n // tn,),
        in_specs=[pl.BlockSpec((rows, dm), lambda j: (0, 0)),
                  pl.BlockSpec((dm, tn), lambda j: (0, j)),
                  pl.BlockSpec((1, tn), lambda j: (0, j))],
        out_specs=pl.BlockSpec((rows, tn), lambda j: (0, j)),
        out_shape=jax.ShapeDtypeStruct((rows, n), F32),
        compiler_params=_params("arbitrary"), name="ada_mod")(cc, w, b.reshape(1, n))


def _inproj_kernel(x_ref, sh_ref, sc_ref, g_ref, w_ref, *o_refs):
    x = x_ref[0]
    ms = jnp.mean(x * x, axis=-1, keepdims=True)
    h = (x * lax.rsqrt(ms + EPS) * g_ref[...] * (1.0 + sc_ref[0]) + sh_ref[0]).astype(BF16)
    off = 0
    for o_ref in o_refs:
        w = o_ref.shape[-1]
        o_ref[(0,) if len(o_ref.shape) == 3 else (...,)] = _dot(h, w_ref[:, off:off + w])
        off += w


def in_projection(x, shift, scale, g, w):
    bsz, seq, dm = x.shape
    tm = _tile(seq, 1024)
    widths = (W_GLA, W_CONV, W_S5, W_MLA)
    vec = pl.BlockSpec((1, 1, dm), lambda b, i: (b, 0, 0))
    return pl.pallas_call(
        _inproj_kernel, grid=(bsz, seq // tm),
        in_specs=[pl.BlockSpec((1, tm, dm), lambda b, i: (b, i, 0)), vec, vec,
                  pl.BlockSpec((1, dm), lambda b, i: (0, 0)),
                  pl.BlockSpec(w.shape, lambda b, i: (0, 0))],
        out_specs=[pl.BlockSpec((tm, wd), lambda b, i: (i, b)) if wd == W_S5 else
                   pl.BlockSpec((1, tm, wd), lambda b, i: (b, i, 0)) for wd in widths],
        out_shape=[jax.ShapeDtypeStruct((seq, bsz * wd) if wd == W_S5 else (bsz, seq, wd), F32) for wd in widths],
        compiler_params=_params("parallel", "parallel"), name="in_projection")(x, shift, scale, g.reshape(1, dm), w)


def _rope_swap(t):
    s = t.reshape(t.shape[:-1] + (2, 2, ROPE_AXIS // 2))
    return jnp.stack([-s[..., 1, :], s[..., 0, :]], axis=-2).reshape(t.shape)


def assemble_w_in(w_in):
    edges = np.cumsum((0,) + IN_SPLITS)
    q, k, v, r, a_f, a_b, conv, s5, cq, ckv, kr = [w_in[:, edges[i]:edges[i + 1]] for i in range(len(IN_SPLITS))]
    pad = lambda t, w: jnp.pad(t, ((0, 0), (0, w - t.shape[1])))
    cols = [q, k, v, r, pad(a_f, 128), pad(a_b, 128), conv, s5, cq, ckv, pad(jnp.concatenate([kr, _rope_swap(kr)], 1), 128)]
    return jnp.concatenate(cols, axis=1).astype(BF16)


def _gla_kernel(z_ref, zc_ref, wa_ref, ba_ref, g_ref, o_ref, oc_ref, of_scr, ob_scr, dec_scr, st_scr, *, seq, ctx_len):
    c = GLA_CHUNK
    nh = GLA_HEADS
    kw = nh * GLA_DK
    vw = nh * GLA_DV
    iota = lambda shape, d: lax.broadcasted_iota(jnp.int32, shape, d)
    ra, ca = iota((nh * 2 * c, 2 * c), 0) & (2 * c - 1), iota((nh * 2 * c, 2 * c), 1)
    causal = ((ra < c) & (ca <= ra)) | ((ra >= c) & (ca >= ra))
    fwd_rows = iota((2 * c, 1), 0) < c
    k_of = lambda t: t >> int(math.log2(GLA_DK))
    v_of = lambda t: t >> int(math.log2(GLA_DV))
    k_head = [(k_of(iota((1, kw), 1)) == h).astype(F32) for h in range(nh)]
    v_head = [(v_of(iota((1, vw), 1)) == h).astype(F32) for h in range(nh)]
    st_rows = iota((2 * vw, kw), 0)
    st_mask = (v_of(st_rows & (vw - 1)) == k_of(iota((2 * vw, kw), 1))).astype(F32)
    st_fwd = st_rows < vw
    head_mean = ((v_of(iota((vw, vw), 0)) == v_of(iota((vw, vw), 1))).astype(F32) * (1.0 / GLA_DV)).astype(BF16)

    def decays(ref, n_rows, base):
        rows = GLA_FINISH_ROWS if n_rows % GLA_FINISH_ROWS == 0 else c
        rr, cc = iota((rows, rows), 0), iota((rows, rows), 1)
        same = (rr >> int(math.log2(c))) == (cc >> int(math.log2(c)))
        tri_f = (same & (cc <= rr)).astype(F32).astype(BF16)
        tri_b = (same & (cc >= rr)).astype(F32).astype(BF16)

        @pl.loop(0, n_rows // rows)
        def _(t):
            s = pl.multiple_of(t * rows, rows)
            a_low = ref[0, pl.ds(s, rows), 2 * kw + 2 * vw:2 * kw + 2 * vw + 256].astype(BF16)
            zl = _dot(a_low, wa_ref[...]) + ba_ref[...]
            la = (jnp.minimum(zl, 0.0) - jnp.log1p(jnp.exp(-jnp.abs(zl)))) / GLA_TAU
            at = pl.ds(pl.multiple_of(base + s, rows), rows)
            dec_scr[at, :kw] = _dot_exact_lhs(tri_f, la[:, :kw])
            dec_scr[at, kw:] = _dot_exact_lhs(tri_b, la[:, kw:])

    def pair(ref, sf, sb, base, st):
        blk_f, blk_b = ref[0, pl.ds(sf, c), :], ref[0, pl.ds(sb, c), :]
        blk = jnp.concatenate([blk_f, blk_b], axis=0)
        q = blk[:, 0:kw] * (GLA_DK ** -0.5)
        k = blk[:, kw:2 * kw]
        vb = blk[:, 2 * kw:2 * kw + vw].astype(BF16)
        b = jnp.concatenate([dec_scr[pl.ds(pl.multiple_of(base + sf, c), c), :kw],
                             dec_scr[pl.ds(pl.multiple_of(base + sb, c), c), kw:]], axis=0)
        b_end = jnp.where(fwd_rows, b[c - 1:c, :], b[c:c + 1, :])
        q_dec = q * jnp.exp(b)
        k_inv = (k * jnp.exp(-b)).astype(BF16)
        k_tail = (k * jnp.exp(b_end - b)).astype(BF16)
        qs = jnp.concatenate([q_dec * k_head[h] for h in range(nh)], axis=0).astype(BF16)
        att = lax.dot_general(qs, k_inv, NT_DIMS, preferred_element_type=F32)
        att = jnp.where(causal, att, 0.0).astype(BF16)
        o_all = _dot(att, vb)
        o = o_all[0:2 * c] * v_head[0]
        for h in range(1, nh):
            o = o + o_all[h * 2 * c:(h + 1) * 2 * c] * v_head[h]
        both = lax.dot_general(q_dec.astype(BF16), st.astype(BF16), NT_DIMS, preferred_element_type=F32)
        o = o + jnp.where(fwd_rows, both[:, :vw], both[:, vw:])
        v2 = jnp.concatenate([jnp.where(fwd_rows, vb, 0), jnp.where(fwd_rows, 0, vb)], axis=1)
        upd = lax.dot_general(v2, k_tail, TN_DIMS, preferred_element_type=F32)
        decay = jnp.where(st_fwd, jnp.exp(b[c - 1:c, :]), jnp.exp(b[c:c + 1, :]))
        return o, st * decay + upd * st_mask

    def scan(ref, n_chunks, base):
        unroll = GLA_UNROLL if n_chunks % GLA_UNROLL == 0 else 1

        @pl.loop(0, n_chunks // unroll)
        def _(it):
            st = st_scr[...]
            for u in range(unroll):
                n = it * unroll + u
                sf = pl.multiple_of(n * c, c)
                sb = pl.multiple_of((n_chunks - 1 - n) * c, c)
                o, st = pair(ref, sf, sb, base, st)
                of_scr[pl.ds(pl.multiple_of(base + sf, c), c), :] = o[:c]
                ob_scr[pl.ds(pl.multiple_of(base + sb, c), c), :] = o[c:]
            st_scr[...] = st

    def finish(ref, out_ref, n_rows, base):
        rows = GLA_FINISH_ROWS if n_rows % GLA_FINISH_ROWS == 0 else c

        @pl.loop(0, n_rows // rows)
        def _(t):
            s = pl.multiple_of(t * rows, rows)
            at = pl.ds(pl.multiple_of(base + s, rows), rows)
            o = of_scr[at, :] + ob_scr[at, :]
            ms = _dot_exact_rhs(o * o, head_mean)
            gate = ref[0, pl.ds(s, rows), 2 * kw + vw:2 * kw + 2 * vw]
            out_ref[0, pl.ds(s, rows), :] = (o * lax.rsqrt(ms + EPS) * g_ref[...] * _silu(gate)).astype(BF16)

    st_scr[...] = jnp.zeros_like(st_scr)
    decays(zc_ref, ctx_len, 0)
    decays(z_ref, seq, ctx_len)
    scan(zc_ref, ctx_len // c, 0)
    scan(z_ref, seq // c, ctx_len)
    finish(zc_ref, oc_ref, ctx_len, 0)
    finish(z_ref, o_ref, seq, ctx_len)


def gla_mixer(zg, zgc, w_a2, b_a, norm_g):
    bsz, seq, wd = zg.shape
    ctx_len = zgc.shape[1]
    vw = GLA_HEADS * GLA_DV
    kw = GLA_HEADS * GLA_DK
    wa = jnp.pad(w_a2, ((0, 0), (0, 128 - GLA_LOWRANK), (0, 0)))
    zero = jnp.zeros_like(wa[0])
    wa = jnp.block([[wa[0], zero], [zero, wa[1]]]).astype(BF16)
    full = lambda shape: pl.BlockSpec(shape, lambda b: (0,) * len(shape))
    return pl.pallas_call(
        functools.partial(_gla_kernel, seq=seq, ctx_len=ctx_len), grid=(bsz,),
        in_specs=[pl.BlockSpec((1, seq, wd), lambda b: (b, 0, 0)),
                  pl.BlockSpec((1, ctx_len, wd), lambda b: (b, 0, 0)),
                  full((256, 2 * kw)), full((1, 2 * kw)), full((1, vw))],
        out_specs=[pl.BlockSpec((1, seq, vw), lambda b: (b, 0, 0)),
                   pl.BlockSpec((1, ctx_len, vw), lambda b: (b, 0, 0))],
        out_shape=[jax.ShapeDtypeStruct((bsz, seq, vw), BF16), jax.ShapeDtypeStruct((bsz, ctx_len, vw), BF16)],
        scratch_shapes=[pltpu.VMEM((seq + ctx_len, vw), F32), pltpu.VMEM((seq + ctx_len, vw), F32),
                        pltpu.VMEM((seq + ctx_len, 2 * kw), F32), pltpu.VMEM((2 * vw, kw), F32)],
        compiler_params=_params("parallel"), name="gla_mixer")(
            zg, zgc, wa, b_a.reshape(1, 2 * kw), norm_g.reshape(1, vw))


def _conv_kernel(u_ref, dw_ref, dwb_ref, lng_ref, lnb_ref, pw_ref, pwb_ref, o_ref, h_scr, *, seq, rows):
    halo = CONV_HALO
    h_scr[0:halo, :] = jnp.zeros((halo, CONV_C), F32)
    h_scr[halo + seq:2 * halo + seq, :] = jnp.zeros((halo, CONV_C), F32)

    @pl.loop(0, seq // rows)
    def _(t):
        s = pl.multiple_of(t * rows, rows)
        u = u_ref[0, pl.ds(s, rows), :]
        h_scr[pl.ds(pl.multiple_of(halo + s, 8), rows), :] = u[:, :CONV_C] * jax.nn.sigmoid(u[:, CONV_C:])

    @pl.loop(0, seq // rows)
    def _(t):
        s = pl.multiple_of(t * rows, rows)
        win = h_scr[pl.ds(s, rows + 2 * halo), :]
        first = halo - CONV_K // 2
        acc = jnp.broadcast_to(dwb_ref[...], (rows, CONV_C))
        for r in range(8):
            taps = [k for k in range(CONV_K) if (first + k) % 8 == r]
            shifted = win if r == 0 else pltpu.roll(win, win.shape[0] - r, axis=0)
            for k in taps:
                at = 8 * ((first + k) // 8)
                acc = acc + shifted[at:at + rows] * dw_ref[k:k + 1, :]
        mu = jnp.mean(acc, axis=-1, keepdims=True)
        var = jnp.mean(jnp.square(acc - mu), axis=-1, keepdims=True)
        y = _silu((acc - mu) * lax.rsqrt(var + EPS) * lng_ref[...] + lnb_ref[...])
        o_ref[0, pl.ds(s, rows), :] = (_dot(y.astype(BF16), pw_ref[...]) + pwb_ref[...]).astype(BF16)


def conformer_conv(u, dw_w, dw_b, ln_g, ln_b, pw_w, pw_b):
    bsz, seq, _ = u.shape
    rows = _tile(seq, 128)
    full = lambda shape: pl.BlockSpec(shape, lambda b: (0,) * len(shape))
    row = lambda t: t.reshape(1, CONV_C)
    return pl.pallas_call(
        functools.partial(_conv_kernel, seq=seq, rows=rows), grid=(bsz,),
        in_specs=[pl.BlockSpec((1, seq, 2 * CONV_C), lambda b: (b, 0, 0)),
                  full((CONV_K + 1, CONV_C)), full((1, CONV_C)), full((1, CONV_C)), full((1, CONV_C)),
                  full((CONV_C, CONV_C)), full((1, CONV_C))],
        out_specs=pl.BlockSpec((1, seq, CONV_C), lambda b: (b, 0, 0)),
        out_shape=jax.ShapeDtypeStruct((bsz, seq, CONV_C), BF16),
        scratch_shapes=[pltpu.VMEM((seq + 2 * CONV_HALO, CONV_C), F32)],
        compiler_params=_params("parallel"), name="conformer_conv")(
            u, jnp.pad(dw_w, ((0, 1), (0, 0))), row(dw_b), row(ln_g), row(ln_b), pw_w.astype(BF16), row(pw_b))


def s5_matrices(lam_re, lam_im, log_dt, b_re, b_im, c_re, c_im):
    dt = jnp.exp(log_dt)[:, None]
    mag = jnp.exp(lam_re * dt)
    a_re, a_im = mag * jnp.cos(lam_im * dt), mag * jnp.sin(lam_im * dt)
    den = lam_re * lam_re + lam_im * lam_im
    f_re = ((a_re - 1.0) * lam_re + a_im * lam_im) / den
    f_im = (a_im * lam_re - (a_re - 1.0) * lam_im) / den
    bb_re = f_re[..., None] * b_re - f_im[..., None] * b_im
    bb_im = f_re[..., None] * b_im + f_im[..., None] * b_re
    eye = jnp.eye(S5_NG, dtype=F32)
    blk_b = lambda t: jnp.einsum("gph,gk->ghkp", t, eye).reshape(S5_C, S5_STATE)
    blk_c = lambda t: jnp.einsum("ghp,gk->gpkh", t, eye).reshape(S5_STATE, S5_C)
    a = jnp.stack([a_re.reshape(S5_STATE), a_im.reshape(S5_STATE)])
    b_mat = jnp.concatenate([blk_b(bb_re), blk_b(bb_im)], axis=1).astype(BF16)
    c_mat = jnp.concatenate([blk_c(c_re), -blk_c(c_im)], axis=0).astype(BF16)
    return a, b_mat, c_mat


def _s5_scan_kernel(uf_ref, ub_ref, a_ref, b_ref, c_ref, h0_ref, yf_ref, yb_ref, hl_ref, hs_f, hs_b, st_scr,
                    *, steps, bsz):
    ns = S5_STATE

    @pl.when(pl.program_id(0) == 0)
    def _():
        st_scr[...] = h0_ref[...]

    for d, (u_ref, hs) in enumerate(((uf_ref, hs_f), (ub_ref, hs_b))):
        u = u_ref[...].reshape(steps * bsz, S5_C).astype(BF16)
        hs[...] = _dot(u, b_ref[d])
    for d, hs in enumerate((hs_f, hs_b)):
        for s0 in range(0, ns, S5_STRIP):
            re, im = slice(s0, s0 + S5_STRIP), slice(ns + s0, ns + s0 + S5_STRIP)
            a_re, a_im = a_ref[d, 0:1, re], a_ref[d, 1:2, re]
            h_re, h_im = st_scr[d, :, re], st_scr[d, :, im]
            for j in range(steps):
                rows = pl.ds((steps - 1 - j if d == 1 else j) * bsz, bsz)
                h_re, h_im = (a_re * h_re - a_im * h_im + hs[rows, re], a_re * h_im + a_im * h_re + hs[rows, im])
                hs[rows, re] = h_re
                hs[rows, im] = h_im
            st_scr[d, :, re] = h_re
            st_scr[d, :, im] = h_im
    yf_ref[...] = _dot(hs_f[...].astype(BF16), c_ref[0]).reshape(steps, bsz, S5_C)
    yb_ref[...] = _dot(hs_b[...].astype(BF16), c_ref[1]).reshape(steps, bsz, S5_C)
    hl_ref[...] = st_scr[...]


def s5_scan(u_tm, a, b_mat, c_mat, h0):
    seq, bsz, _ = u_tm.shape
    steps = _tile(seq, 32)
    n = seq // steps
    full = lambda t: pl.BlockSpec(t.shape, lambda i: (0,) * t.ndim)
    fwd = pl.BlockSpec((steps, bsz, S5_C), lambda i: (i, 0, 0))
    bwd = pl.BlockSpec((steps, bsz, S5_C), lambda i: (n - 1 - i, 0, 0))
    y = jax.ShapeDtypeStruct((seq, bsz, S5_C), F32)
    return pl.pallas_call(
        functools.partial(_s5_scan_kernel, steps=steps, bsz=bsz), grid=(n,),
        in_specs=[fwd, bwd, full(a), full(b_mat), full(c_mat), full(h0)],
        out_specs=[fwd, bwd, full(h0)],
        out_shape=[y, y, jax.ShapeDtypeStruct(h0.shape, F32)],
        scratch_shapes=[pltpu.VMEM((steps * bsz, 2 * S5_STATE), F32), pltpu.VMEM((steps * bsz, 2 * S5_STATE), F32),
                        pltpu.VMEM(h0.shape, F32)],
        compiler_params=_params("arbitrary"), name="s5_scan")(u_tm, u_tm, a, b_mat, c_mat, h0)


def _s5_out_kernel(u_ref, yf_ref, yb_ref, d_ref, w_ref, b_ref, o_ref):
    y = d_ref[...] * u_ref[...] + yf_ref[...] + yb_ref[...]
    z = _dot(y.astype(BF16), w_ref[...]) + b_ref[...]
    o_ref[0] = (z[:, :S5_C] * jax.nn.sigmoid(z[:, S5_C:])).astype(BF16)


def s5_output(u, y_f, y_b, d_skip, glu_w, glu_b):
    seq, bsz = u.shape[0], u.shape[1] // S5_C
    tt = _tile(seq, 512)
    rows = pl.BlockSpec((tt, S5_C), lambda i, b: (i, b))
    full = lambda shape: pl.BlockSpec(shape, lambda i, b: (0,) * len(shape))
    return pl.pallas_call(
        _s5_out_kernel, grid=(seq // tt, bsz),
        in_specs=[rows, rows, rows, full((1, S5_C)), full((S5_C, 2 * S5_C)), full((1, 2 * S5_C))],
        out_specs=pl.BlockSpec((1, tt, S5_C), lambda i, b: (b, i, 0)),
        out_shape=jax.ShapeDtypeStruct((bsz, seq, S5_C), BF16),
        compiler_params=_params("parallel", "parallel"), name="s5_output")(
            u, y_f, y_b, d_skip.reshape(1, S5_C), glu_w.astype(BF16), glu_b.reshape(1, 2 * S5_C))


def s5_mixer(zs, zsc, lam_re, lam_im, log_dt, b_re, b_im, c_re, c_im, d_skip, glu_w, glu_b):
    bsz = zs.shape[1] // S5_C
    u = zs.reshape(-1, bsz, S5_C)
    uc = zsc.reshape(-1, bsz, S5_C)
    mats = [s5_matrices(lam_re[d], lam_im[d], log_dt[d], b_re[d], b_im[d], c_re[d], c_im[d]) for d in range(2)]
    a, b_mat, c_mat = [jnp.stack(t) for t in zip(*mats)]
    yc_f, yc_b, hc = s5_scan(uc, a, b_mat, c_mat, jnp.zeros((2, bsz, 2 * S5_STATE), F32))
    y_f, y_b, _ = s5_scan(u, a, b_mat, c_mat, hc)
    flat = lambda t: t.reshape(t.shape[0], bsz * S5_C)
    return (s5_output(zs, flat(y_f), flat(y_b), d_skip, glu_w, glu_b),
            s5_output(zsc, flat(yc_f), flat(yc_b), d_skip, glu_w, glu_b))


def rope_tables(seq_len, rotate):
    if not rotate:
        return jnp.ones((seq_len, MLA_ROPE), F32), jnp.zeros((seq_len, MLA_ROPE), F32)
    rows = seq_len // GRID_W
    row = jnp.broadcast_to(jnp.arange(rows, dtype=F32)[:, None], (rows, GRID_W)).reshape(seq_len)
    col = jnp.broadcast_to(jnp.arange(GRID_W, dtype=F32)[None, :], (rows, GRID_W)).reshape(seq_len)
    inv_freq = ROPE_BASE ** (-jnp.arange(ROPE_AXIS // 2, dtype=F32) / (ROPE_AXIS // 2))
    ang = jnp.stack([row[:, None] * inv_freq, col[:, None] * inv_freq], axis=1)
    full = lambda t: jnp.broadcast_to(t[:, :, None, :], (seq_len, 2, 2, ROPE_AXIS // 2)).reshape(seq_len, MLA_ROPE)
    return full(jnp.cos(ang)), full(jnp.sin(ang))


def mla_tables(seq_len, rotate):
    cos, sin = rope_tables(seq_len, rotate)
    head = lambda rope, fill: jnp.concatenate(
        [jnp.full((seq_len, MLA_NOPE), fill, F32), rope, jnp.zeros((seq_len, MLA_HEAD_PAD - MLA_NOPE - MLA_ROPE), F32)], 1)
    cos_q = jnp.tile(head(cos, 1.0), (1, MLA_HEADS))
    sin_q = jnp.tile(head(sin, 0.0), (1, MLA_HEADS))
    cs_k = jnp.concatenate([cos, sin, jnp.zeros((seq_len, 128 - 2 * MLA_ROPE), F32)], 1)
    return cos_q, sin_q, cs_k


def mla_weights(w_uq, w_ukv):
    hd = MLA_NOPE + MLA_ROPE
    zq = jnp.zeros((MLA_Q_LORA, MLA_HEAD_PAD - hd), F32)
    zn = jnp.zeros((MLA_Q_LORA, MLA_NOPE), F32)
    wq, wq_sw, wk, wv = [], [], [], []
    for h in range(MLA_HEADS):
        qh = w_uq[:, h * hd:(h + 1) * hd]
        wq += [qh, zq]
        wq_sw += [zn, _rope_swap(qh[:, MLA_NOPE:]), zq]
        kvh = w_ukv[:, h * (MLA_NOPE + MLA_V):(h + 1) * (MLA_NOPE + MLA_V)]
        wk += [kvh[:, :MLA_NOPE], jnp.zeros((MLA_KV_LORA, MLA_HEAD_PAD - MLA_NOPE), F32)]
        wv += [kvh[:, MLA_NOPE:]]
    place = np.zeros((128, MLA_HEADS * MLA_HEAD_PAD), np.float32)
    for h in range(MLA_HEADS):
        for j in range(MLA_ROPE):
            place[j, h * MLA_HEAD_PAD + MLA_NOPE + j] = 1.0
            place[MLA_ROPE + j, h * MLA_HEAD_PAD + MLA_NOPE + j] = 1.0
    cat = lambda ts: jnp.concatenate(ts, axis=1).astype(BF16)
    return cat(wq), cat(wq_sw), cat(wk), cat(wv), jnp.asarray(place).astype(BF16)


def _mla_prep_kernel(z_ref, qg_ref, kg_ref, wq_ref, wqs_ref, wk_ref, wv_ref, pl_ref, cq_ref, sq_ref, csk_ref,
                     q_ref, k_ref, v_ref):
    z = z_ref[0]
    norm = lambda t, g: (t * lax.rsqrt(jnp.mean(t * t, axis=-1, keepdims=True) + EPS) * g).astype(BF16)
    cq = norm(z[:, :MLA_Q_LORA], qg_ref[...])
    ckv = norm(z[:, MLA_Q_LORA:MLA_Q_LORA + MLA_KV_LORA], kg_ref[...])
    scale = (MLA_NOPE + MLA_ROPE) ** -0.5
    q = _dot(cq, wq_ref[...]) * cq_ref[...] + _dot(cq, wqs_ref[...]) * sq_ref[...]
    q_ref[0] = (q * scale).astype(BF16)
    kr = z[:, MLA_Q_LORA + MLA_KV_LORA:] * csk_ref[...]
    k_ref[0] = (_dot(ckv, wk_ref[...]) + _dot_exact_rhs(kr, pl_ref[...])).astype(BF16)
    v_ref[0] = _dot(ckv, wv_ref[...]).astype(BF16)


def mla_prep(zm, qn_g, kvn_g, weights, tables):
    bsz, seq, wd = zm.shape
    tm = _tile(seq, 1024)
    wq, wq_sw, wk, wv, place = weights
    cos_q, sin_q, cs_k = tables
    qw, vw = MLA_HEADS * MLA_HEAD_PAD, MLA_HEADS * MLA_V
    full = lambda t: pl.BlockSpec(t.shape, lambda b, i: (0,) * t.ndim)
    pos = lambda t: pl.BlockSpec((tm, t.shape[1]), lambda b, i: (i, 0))
    out = lambda w: pl.BlockSpec((1, tm, w), lambda b, i: (b, i, 0))
    qg, kg = qn_g.reshape(1, -1), kvn_g.reshape(1, -1)
    return pl.pallas_call(
        _mla_prep_kernel, grid=(bsz, seq // tm),
        in_specs=[pl.BlockSpec((1, tm, wd), lambda b, i: (b, i, 0)), full(qg), full(kg), full(wq), full(wq_sw),
                  full(wk), full(wv), full(place), pos(cos_q), pos(sin_q), pos(cs_k)],
        out_specs=[out(qw), out(qw), out(vw)],
        out_shape=[jax.ShapeDtypeStruct((bsz, seq, qw), BF16), jax.ShapeDtypeStruct((bsz, seq, qw), BF16),
                   jax.ShapeDtypeStruct((bsz, seq, vw), BF16)],
        compiler_params=_params("parallel", "parallel"), name="mla_prep")(
            zm, qg, kg, wq, wq_sw, wk, wv, place, cos_q, sin_q, cs_k)


def _mla_attn_kernel(q_ref, *refs, n_seg):
    k_refs, v_refs, o_ref = refs[:n_seg], refs[n_seg:2 * n_seg], refs[2 * n_seg]
    outs = []
    for h in range(MLA_HEADS):
        q = q_ref[0, :, h * MLA_HEAD_PAD:(h + 1) * MLA_HEAD_PAD]
        s = [lax.dot_general(q, k_ref[0, :, h * MLA_HEAD_PAD:(h + 1) * MLA_HEAD_PAD], NT_DIMS,
                             preferred_element_type=F32) for k_ref in k_refs]
        m = functools.reduce(jnp.maximum, [jnp.max(t, axis=-1, keepdims=True) for t in s])
        p = [jnp.exp(t - m) for t in s]
        den = functools.reduce(jnp.add, [jnp.sum(t, axis=-1, keepdims=True) for t in p])
        o = functools.reduce(jnp.add, [_dot(t.astype(BF16), v_ref[0, :, h * MLA_V:(h + 1) * MLA_V])
                                       for t, v_ref in zip(p, v_refs)])
        outs.append(o / den)
    o_ref[0] = jnp.concatenate(outs, axis=-1).astype(BF16)


def mla_attention(q, ks, vs):
    bsz, seq, qw = q.shape
    tq = _tile(seq, 1024)
    vw = MLA_HEADS * MLA_V
    seg = lambda t: pl.BlockSpec((1,) + t.shape[1:], lambda b, i: (b, 0, 0))
    return pl.pallas_call(
        functools.partial(_mla_attn_kernel, n_seg=len(ks)), grid=(bsz, seq // tq),
        in_specs=[pl.BlockSpec((1, tq, qw), lambda b, i: (b, i, 0))] + [seg(t) for t in ks] + [seg(t) for t in vs],
        out_specs=pl.BlockSpec((1, tq, vw), lambda b, i: (b, i, 0)),
        out_shape=jax.ShapeDtypeStruct((bsz, seq, vw), BF16),
        compiler_params=_params("parallel", "parallel"), name="mla_attention")(q, *ks, *vs)


def _outproj_kernel(a_ref, b_ref, c_ref, d_ref, w_ref, x_ref, g1_ref, sh_ref, sc_ref, g_ref, rw_ref,
                    xo_ref, h_ref, lg_ref):
    mix = None
    for j, o_ref in enumerate((a_ref, b_ref, c_ref, d_ref)):
        wd = o_ref.shape[-1]
        part = _dot(o_ref[0], w_ref[j * wd:(j + 1) * wd, :])
        mix = part if mix is None else mix + part
    x = x_ref[0] + g1_ref[0] * mix
    xo_ref[0] = x
    ms = jnp.mean(x * x, axis=-1, keepdims=True)
    h = x * lax.rsqrt(ms + EPS) * g_ref[...] * (1.0 + sc_ref[0]) + sh_ref[0]
    _store_row_slabs(h_ref, _pack_bf16_pairs(h))
    lg_ref[...] =lax.dot_general(rw_ref[...], h, NT_DIMS, precision=HIGHEST, preferred_element_type=F32)


def out_projection(parts, w_out, x, g1, shift, scale, g, router_w):
    bsz, seq, dm = x.shape
    tm = _tile(seq, 1024)
    nt = seq // tm
    ne = router_w.shape[1]
    groups = dm // 2 // 128
    rows = lambda w: pl.BlockSpec((1, tm, w), lambda b, i: (b, i, 0))
    vec = pl.BlockSpec((1, 1, dm), lambda b, i: (b, 0, 0))
    full = lambda shape: pl.BlockSpec(shape, lambda b, i: (0,) * len(shape))
    return pl.pallas_call(
        _outproj_kernel, grid=(bsz, nt),
        in_specs=[rows(p.shape[-1]) for p in parts] + [full(w_out.shape), rows(dm), vec, vec, vec, full((1, dm)),
                                                       full((ne, dm))],
        out_specs=[rows(dm), pl.BlockSpec((tm * groups, 128), lambda b, i: (b * nt + i, 0)),
                   pl.BlockSpec((ne, tm), lambda b, i: (0, b * nt + i))],
        out_shape=[jax.ShapeDtypeStruct((bsz, seq, dm), F32), jax.ShapeDtypeStruct((bsz * seq * groups, 128), jnp.int32),
                   jax.ShapeDtypeStruct((ne, bsz * seq), F32)],
        compiler_params=_params("parallel", "parallel"), name="out_projection")(
            *parts, w_out.astype(BF16), x, g1, shift, scale, g.reshape(1, dm), router_w.T)


def _route_kernel(lg_ref, b_ref, e_ref, w_ref):
    ne, tt = lg_ref.shape
    per = ne // N_GROUPS
    neg = -jnp.inf
    scores = jax.nn.sigmoid(lg_ref[...])
    biased = scores + b_ref[...]
    v3 = biased.reshape(N_GROUPS, per, tt)
    e_in = lax.broadcasted_iota(jnp.int32, v3.shape, 1).astype(F32)
    m1 = jnp.max(v3, axis=1, keepdims=True)
    i1 = jnp.min(jnp.where(v3 == m1, e_in, float(per)), axis=1, keepdims=True)
    m2 = jnp.max(jnp.where(e_in == i1, neg, v3), axis=1, keepdims=True)
    grp = (m1 + m2).reshape(N_GROUPS, tt)

    def pick(cur, count):
        ids = lax.broadcasted_iota(jnp.int32, cur.shape, 0).astype(F32)
        marks = jnp.zeros(cur.shape, F32)
        picked = []
        for _ in range(count):
            m = jnp.max(cur, axis=0, keepdims=True)
            first = jnp.min(jnp.where(cur == m, ids, float(cur.shape[0])), axis=0, keepdims=True)
            hit = ids == first
            marks = jnp.where(hit, 1.0, marks)
            cur = jnp.where(hit, neg, cur)
            picked.append(first)
        return marks, picked

    grp_on, _ = pick(grp, TOPK_GROUPS)
    exp_on = jnp.broadcast_to(grp_on.reshape(N_GROUPS, 1, tt), v3.shape).reshape(ne, tt)
    chosen, picked = pick(jnp.where(exp_on > 0.0, biased, neg), TOP_K)
    w = scores * chosen
    gate = w / jnp.sum(w, axis=0, keepdims=True) * ROUTED_SCALE
    ids = lax.broadcasted_iota(jnp.int32, gate.shape, 0).astype(F32)
    e_ref[...] = jnp.concatenate(picked, axis=0).astype(jnp.int32)
    w_ref[...] = jnp.concatenate([jnp.sum(jnp.where(ids == p, gate, 0.0), axis=0, keepdims=True) for p in picked], axis=0)


def route(logits_t, router_b):
    ne, n = logits_t.shape
    tt = _tile(n, 1024)
    out = pl.BlockSpec((TOP_K, tt), lambda i: (0, i))
    return pl.pallas_call(
        _route_kernel, grid=(n // tt,),
        in_specs=[pl.BlockSpec((ne, tt), lambda i: (0, i)), pl.BlockSpec((ne, 1), lambda i: (0, 0))],
        out_specs=[out, out],
        out_shape=[jax.ShapeDtypeStruct((TOP_K, n), jnp.int32), jax.ShapeDtypeStruct((TOP_K, n), F32)],
        compiler_params=_params("parallel"), name="route")(logits_t, router_b.reshape(ne, 1))


def moe_plan(eid, wgt, chunk, xg, yg):
    n = eid.shape[1]
    nc, na = n // chunk, chunk * TOP_K
    ns = na // MOE_ROWS + N_EXPERTS
    key = eid.T.reshape(nc, na) * na + jnp.arange(na, dtype=jnp.int32)
    skey, sw = lax.sort((key, wgt.T.reshape(nc, na)), dimension=1, num_keys=1)
    e_sorted, tok = skey // na, (skey % na) // TOP_K
    experts = jnp.arange(N_EXPERTS, dtype=jnp.int32)
    ends = jnp.sum(e_sorted[:, None, :] <= experts[None, :, None], axis=-1, dtype=jnp.int32)
    starts = jnp.concatenate([jnp.zeros((nc, 1), jnp.int32), ends[:, :-1]], axis=1)
    pad_end = jnp.cumsum(-(-(ends - starts) // MOE_ROWS) * MOE_ROWS, axis=1)
    pad_start = jnp.concatenate([jnp.zeros((nc, 1), jnp.int32), pad_end[:, :-1]], axis=1)
    first = jnp.arange(ns, dtype=jnp.int32) * MOE_ROWS
    exp = jnp.minimum(jnp.sum(pad_end[:, None, :] <= first[None, :, None], axis=-1, dtype=jnp.int32), N_EXPERTS - 1)
    active = first[None, :] < pad_end[:, -1:]
    take = lambda t, i: jnp.take_along_axis(t, i, axis=1)
    rank = first[None, :, None] + jnp.arange(MOE_ROWS, dtype=jnp.int32) - take(pad_start, exp)[..., None]
    real = (rank < take(ends - starts, exp)[..., None]) & active[..., None]
    src = jnp.clip(take(starts, exp)[..., None] + rank, 0, na - 1).reshape(nc, ns * MOE_ROWS)
    rows = lambda t: take(t, src).reshape(nc, ns, MOE_ROWS)
    flat = lambda t: t.reshape(nc * ns, 1, MOE_ROWS)
    gather_at = flat(jnp.where(real, rows(tok), 0) * xg)
    scatter_at = flat(jnp.where(real, rows(tok), chunk) * yg)
    weight = flat(jnp.where(real, rows(sw), 0.0))
    return gather_at, scatter_at, weight, exp.reshape(nc * ns), active.astype(jnp.int32).reshape(nc * ns)


def _moe_tick(x_ref, y_ref, idx_ref, w_ref, tick, wgu_ref, wd_ref, gather_to, mm_from, mm_to, scatter_from):
    bm = MOE_ROWS
    stride = bm + 1
    xg = mm_from.shape[0] // stride
    yg = mm_to.shape[0] // stride
    for mi in range(bm):
        src = pl.multiple_of(idx_ref[0, 0, 2 * tick * bm + mi], xg)
        gather_to[pl.ds(mi, xg, stride=stride), :] = x_ref[0, pl.ds(src, xg), :]
    low, high = _unpack_bf16_pairs([mm_from[j * stride:j * stride + bm, :] for j in range(xg)])
    half = low.shape[1]
    ff = wd_ref.shape[1]
    up = _dot(low, wgu_ref[0, :half, :]) + _dot(high, wgu_ref[0, half:, :])
    act = (_silu(up[:, :ff]) * up[:, ff:]).astype(BF16)
    out = _dot(act, wd_ref[0])
    for j in range(yg):
        mm_to[j * stride:j * stride + bm, :] = out[:, j * 128:(j + 1) * 128]
    for base in range(0, bm, MOE_BATCH):
        at = [pl.multiple_of(idx_ref[0, 0, (2 * tick + 1) * bm + mi], yg) for mi in range(base, base + MOE_BATCH)]
        new = [y_ref[0, pl.ds(i, yg), :] + w_ref[0, 0, tick * bm + mi] * scatter_from[pl.ds(mi, yg, stride=stride), :]
               for i, mi in zip(at, range(base, base + MOE_BATCH))]
        for i, v in zip(at, new):
            y_ref[0, pl.ds(i, yg), :] = v


def _moe_routed_kernel(exp_ref, act_ref, x_ref, idx_ref, w_ref, wgua_ref, wda_ref, wgub_ref, wdb_ref,
                       y_ref, xt0, xt1, ot0, ot1, *, ns):
    c, g = pl.program_id(0), pl.program_id(1)

    @pl.when(g == 0)
    def _():
        y_ref[...] = jnp.zeros_like(y_ref)

    @pl.when((c == 0) & (g == 0))
    def _():
        xt1[...] = jnp.zeros_like(xt1)
        ot0[...] = jnp.zeros_like(ot0)
        ot1[...] = jnp.zeros_like(ot1)

    def live(step):
        return (step >= 0) & (step < ns) & (act_ref[c * ns + jnp.clip(step, 0, ns - 1)] != 0)

    t = 2 * g

    @pl.when(live(t - 2) | live(t - 1) | live(t) | live(t + 1))
    def _():
        _moe_tick(x_ref, y_ref, idx_ref, w_ref, 0, wgua_ref, wda_ref,
                  gather_to=xt0, mm_from=xt1, mm_to=ot1, scatter_from=ot0)
        _moe_tick(x_ref, y_ref, idx_ref, w_ref, 1, wgub_ref, wdb_ref,
                  gather_to=xt1, mm_from=xt0, mm_to=ot0, scatter_from=ot1)


def moe_routed(slabs, eid, wgt, wgu, wd):
    n = eid.shape[1]
    xg = slabs.shape[0] // n
    yg = 2 * xg
    dm, ff = wd.shape[2], wd.shape[1]
    chunk = _tile(n, MOE_CHUNK)
    nc = n // chunk
    ns = chunk * TOP_K // MOE_ROWS + N_EXPERTS
    assert ns % 2 == 0
    steps = (ns + 2) // 2
    last = ns - 1
    gather_at, scatter_at, weight, exp, active = moe_plan(eid, wgt, chunk, xg, yg)
    tick = jnp.arange(steps, dtype=jnp.int32)[:, None] * 2 + jnp.arange(2, dtype=jnp.int32)[None, :]
    gathered = jnp.minimum(tick, last)
    scattered = jnp.where(tick >= 2, tick - 2, last)
    per_step = lambda t, at: jnp.take(t.reshape(nc, ns, MOE_ROWS), at, axis=1)
    idx = jnp.concatenate([per_step(gather_at, gathered[:, 0]), per_step(scatter_at, scattered[:, 0]),
                           per_step(gather_at, gathered[:, 1]), per_step(scatter_at, scattered[:, 1])], axis=-1)
    wts = jnp.concatenate([per_step(weight, scattered[:, 0]), per_step(weight, scattered[:, 1])], axis=-1)
    smem = lambda width: pl.BlockSpec((1, 1, width), lambda c, g, e, a: (c * steps + g, 0, 0), memory_space=pltpu.SMEM)
    computed = lambda tk: lambda g: jnp.clip(2 * g + tk - 1, 0, last)
    expert = lambda shape, step: pl.BlockSpec((1,) + shape, lambda c, g, e, a: (e[c * ns + step(g)], 0, 0))
    once = pl.Buffered(1)
    stage = lambda groups, dtype: pltpu.VMEM((-(-groups * (MOE_ROWS + 1) // 8) * 8, 128), dtype)
    return pl.pallas_call(
        functools.partial(_moe_routed_kernel, ns=ns),
        grid_spec=pltpu.PrefetchScalarGridSpec(
            num_scalar_prefetch=2, grid=(nc, steps),
            in_specs=[pl.BlockSpec((1, chunk * xg, 128), lambda c, g, e, a: (c, 0, 0), pipeline_mode=once),
                      smem(4 * MOE_ROWS), smem(2 * MOE_ROWS),
                      expert((dm, 2 * ff), computed(0)), expert((ff, dm), computed(0)),
                      expert((dm, 2 * ff), computed(1)), expert((ff, dm), computed(1))],
            out_specs=pl.BlockSpec((1, chunk * yg + 8, 128), lambda c, g, e, a: (c, 0, 0), pipeline_mode=once),
            scratch_shapes=[stage(xg, jnp.int32), stage(xg, jnp.int32), stage(yg, F32), stage(yg, F32)]),
        out_shape=jax.ShapeDtypeStruct((nc, chunk * yg + 8, 128), F32),
        compiler_params=_params("arbitrary", "arbitrary"), name="moe_routed")(
            exp, active, slabs.reshape(nc, chunk * xg, 128), idx.reshape(nc * steps, 1, 4 * MOE_ROWS),
            wts.reshape(nc * steps, 1, 2 * MOE_ROWS), wgu, wd, wgu, wd)


def _moe_finish_kernel(y_ref, h_ref, x_ref, g2_ref, wg_ref, wu_ref, wd_ref, fg_ref, o_ref, *, final_norm):
    tm = x_ref.shape[0]
    low, high = _unpack_bf16_pairs(_load_row_slabs(h_ref, tm, h_ref.shape[0] // tm))
    routed = jnp.concatenate(_load_row_slabs(y_ref, tm, y_ref.shape[1] // tm, lead=(0,)), axis=1)
    half = low.shape[1]
    up = lambda w_ref: _dot(low, w_ref[:half, :]) + _dot(high, w_ref[half:, :])
    shared = _dot((_silu(up(wg_ref)) * up(wu_ref)).astype(BF16), wd_ref[...])
    x = x_ref[...] + g2_ref[0] * (routed + shared)
    if final_norm:
        x = x * lax.rsqrt(jnp.mean(x * x, axis=-1, keepdims=True) + EPS) * fg_ref[...]
    o_ref[...] = x


def moe_finish(y, slabs, x, g2, sg, su, sd, final_g, final_norm):
    bsz, seq, dm = x.shape
    n = bsz * seq
    tm = _tile(seq, 1024)
    nt = seq // tm
    xg, yg = slabs.shape[0] // n, dm // 128
    per_chunk = (y.shape[1] - 8) // yg // tm
    rows = lambda w: pl.BlockSpec((tm, w), lambda i: (i, 0))
    full = lambda t: pl.BlockSpec(t.shape, lambda i: (0,) * t.ndim)
    sg, su, sd, fg = sg.astype(BF16), su.astype(BF16), sd.astype(BF16), final_g.reshape(1, dm)
    out = pl.pallas_call(
        functools.partial(_moe_finish_kernel, final_norm=final_norm), grid=(n // tm,),
        in_specs=[pl.BlockSpec((1, tm * yg, 128), lambda i: (i // per_chunk, i % per_chunk, 0)),
                  pl.BlockSpec((tm * xg, 128), lambda i: (i, 0)), rows(dm),
                  pl.BlockSpec((1, 1, dm), lambda i: (i // nt, 0, 0)), full(sg), full(su), full(sd), full(fg)],
        out_specs=rows(dm), out_shape=jax.ShapeDtypeStruct((n, dm), F32),
        compiler_params=_params("parallel"), name="moe_finish")(y, slabs, x.reshape(n, dm), g2, sg, su, sd, fg)
    return out.reshape(bsz, seq, dm)


def kernel(x, c, ctx, c_ctx, ada_w, ada_b, norm1_g, norm2_g, w_in, w_out, gla_w_a2, gla_b_a, gla_norm_g,
           conv_dw_w, conv_dw_b, conv_ln_g, conv_ln_b, conv_pw_w, conv_pw_b,
           s5_lam_re, s5_lam_im, s5_log_dt, s5_b_re, s5_b_im, s5_c_re, s5_c_im, s5_d, s5_glu_w, s5_glu_b,
           mla_qn_g, mla_kvn_g, mla_w_uq, mla_w_ukv,
           moe_router_w, moe_router_b, moe_w_gate, moe_w_up, moe_w_down,
           shared_w_gate, shared_w_up, shared_w_down, final_g):
    bsz, seq, dm = x.shape
    ctx_len = ctx.shape[1]
    depth = ada_w.shape[0]
    cc = jnp.concatenate([c, c_ctx[None], jnp.zeros((-(bsz + 1) % 8, dm), F32)], axis=0)
    tables = mla_tables(seq, True)
    tables_c = mla_tables(ctx_len, False)
    xc = ctx
    for l in range(depth):
        last = l == depth - 1
        mod = ada_mod(cc, ada_w[l], ada_b[l])
        lat = [t.reshape(bsz, 1, dm) for t in jnp.split(mod[:bsz], 6, axis=-1)]
        con = [jnp.broadcast_to(t.reshape(1, 1, dm), (bsz, 1, dm)) for t in jnp.split(mod[bsz], 6)]
        sh1, sc1, g1, sh2, sc2, g2 = lat
        sh1c, sc1c, g1c, sh2c, sc2c, g2c = con
        w_in_l = assemble_w_in(w_in[l])
        zg, zv, zs, zm = in_projection(x, sh1, sc1, norm1_g[l], w_in_l)
        zgc, zvc, zsc, zmc = in_projection(xc, sh1c, sc1c, norm1_g[l], w_in_l)
        conv_p = (conv_dw_w[l], conv_dw_b[l], conv_ln_g[l], conv_ln_b[l], conv_pw_w[l], conv_pw_b[l])
        o_gla, oc_gla = gla_mixer(zg, zgc, gla_w_a2[l], gla_b_a[l], gla_norm_g[l])
        o_conv = conformer_conv(zv, *conv_p)
        o_s5, oc_s5 = s5_mixer(zs, zsc, s5_lam_re[l], s5_lam_im[l], s5_log_dt[l], s5_b_re[l], s5_b_im[l],
                               s5_c_re[l], s5_c_im[l], s5_d[l], s5_glu_w[l], s5_glu_b[l])
        mla_w = mla_weights(mla_w_uq[l], mla_w_ukv[l])
        q, k, v = mla_prep(zm, mla_qn_g[l], mla_kvn_g[l], mla_w, tables)
        qc, kc, vc = mla_prep(zmc, mla_qn_g[l], mla_kvn_g[l], mla_w, tables_c)
        o_mla = mla_attention(q, [k, kc], [v, vc])
        experts = (jnp.concatenate([moe_w_gate[l], moe_w_up[l]], axis=2).astype(BF16), moe_w_down[l].astype(BF16))
        shared = (shared_w_gate[l], shared_w_up[l], shared_w_down[l])

        def ffn(parts, x_in, gate1, shift, scale, gate2, final_norm):
            x_mid, slabs, logits = out_projection(parts, w_out[l], x_in, gate1, shift, scale, norm2_g[l], moe_router_w[l])
            y = moe_routed(slabs, *route(logits, moe_router_b[l]), *experts)
            return moe_finish(y, slabs, x_mid, gate2, *shared, final_g, final_norm)

        if not last:
            oc_conv = conformer_conv(zvc, *conv_p)
            oc_mla = mla_attention(qc, [kc], [vc])
            xc = ffn([oc_gla, oc_conv, oc_s5, oc_mla], xc, g1c, sh2c, sc2c, g2c, False)
        x = ffn([o_gla, o_conv, o_s5, o_mla], x, g1, sh2, sc2, g2, last)
    return x
```

```python
import functools
import math

import numpy as np
import jax
import jax.numpy as jnp
from jax import lax
from jax.experimental import pallas as pl
from jax.experimental.pallas import tpu as pltpu

F32 = jnp.float32
BF16 = jnp.bfloat16
HIGHEST = lax.Precision.HIGHEST
EPS = 1e-6

GRID_W = 64

GLA_HEADS = 4
GLA_DK = 32
GLA_DV = 64
GLA_LOWRANK = 16
GLA_TAU = 16.0
GLA_CHUNK = 64
GLA_UNROLL = 4
GLA_FINISH_ROWS = 256

CONV_C = 256
CONV_K = 31
CONV_HALO = 16

S5_C = 256
S5_GROUP = 16
S5_NG = 16
S5_P = 64
S5_STATE = S5_NG * S5_P
S5_STRIP = 256

MLA_HEADS = 4
MLA_NOPE = 64
MLA_ROPE = 32
MLA_V = 64
MLA_Q_LORA = 256
MLA_KV_LORA = 128
MLA_HEAD_PAD = 128
ROPE_AXIS = MLA_ROPE // 2
ROPE_BASE = 10000.0

N_EXPERTS = 64
TOP_K = 8
N_GROUPS = 8
TOPK_GROUPS = 4
ROUTED_SCALE = 2.5
MOE_CHUNK = 4096
MOE_ROWS = 256
MOE_BATCH = 8

IN_SPLITS = (128, 128, 256, 256, 16, 16, 512, 256, 256, 128, 32)
W_GLA, W_CONV, W_S5, W_MLA = 1024, 512, 256, 512

VMEM_LIMIT = 48 * 1024 * 1024

NT_DIMS = (((1,), (1,)), ((), ()))
TN_DIMS = (((0,), (0,)), ((), ()))


def _params(*sem, vmem=VMEM_LIMIT):
    return pltpu.CompilerParams(dimension_semantics=sem, vmem_limit_bytes=vmem)


def _silu(v):
    return v * jax.nn.sigmoid(v)


def _dot(a, b, **kw):
    return jnp.dot(a, b, preferred_element_type=F32, **kw)


def _split_bf16(x):
    hi = x.astype(BF16)
    rest = x - hi.astype(F32)
    mid = rest.astype(BF16)
    return hi, mid, (rest - mid.astype(F32)).astype(BF16)


def _dot_exact_lhs(m, x):
    n = x.shape[1]
    prod = _dot(m, jnp.concatenate(_split_bf16(x), axis=1))
    return prod[:, :n] + prod[:, n:2 * n] + prod[:, 2 * n:]


def _dot_exact_rhs(x, m):
    n = x.shape[0]
    prod = _dot(jnp.concatenate(_split_bf16(x), axis=0), m)
    return prod[:n] + prod[n:2 * n] + prod[2 * n:]


def _tile(n, pref):
    return pref if n % pref == 0 else n


def _pack_bf16_pairs(h):
    bits = pltpu.bitcast(h.astype(BF16).astype(F32), jnp.int32)
    w = h.shape[-1] // 2
    return lax.shift_right_logical(bits[:, :w], 16) | bits[:, w:]


def _store_row_slabs(ref, value):
    rows, g = value.shape[0], value.shape[1] // 128
    for j in range(g):
        ref[pl.ds(j, rows, stride=g), :] = value[:, j * 128:(j + 1) * 128]


def _load_row_slabs(ref, rows, g, lead=()):
    return [ref[lead + (pl.ds(j, rows, stride=g), slice(None))] for j in range(g)]


def _unpack_bf16_pairs(words):
    low = jnp.concatenate([pltpu.bitcast(v << 16, F32) for v in words], axis=1)
    high = jnp.concatenate([pltpu.bitcast(v & -65536, F32) for v in words], axis=1)
    return low.astype(BF16), high.astype(BF16)


def _ada_kernel(c_ref, w_ref, b_ref, o_ref):
    o_ref[...] = _dot(_silu(c_ref[...]), w_ref[...], precision=HIGHEST) + b_ref[...]


def ada_mod(cc, w, b):
    rows, dm = cc.shape
    n = w.shape[1]
    tn = _tile(n, 512)
    return pl.pallas_call(
        _ada_kernel, grid=(n // tn,),
        in_specs=[pl.BlockSpec((rows, dm), lambda j: (0, 0)),
                  pl.BlockSpec((dm, tn), lambda j: (0, j)),
                  pl.BlockSpec((1, tn), lambda j: (0, j))],
        out_specs=pl.BlockSpec((rows, tn), lambda j: (0, j)),
        out_shape=jax.ShapeDtypeStruct((rows, n), F32),
        compiler_params=_params("arbitrary"), name="ada_mod")(cc, w, b.reshape(1, n))


def _inproj_kernel(x_ref, sh_ref, sc_ref, g_ref, w_ref, *o_refs):
    x = x_ref[0]
    ms = jnp.mean(x * x, axis=-1, keepdims=True)
    h = (x * lax.rsqrt(ms + EPS) * g_ref[...] * (1.0 + sc_ref[0]) + sh_ref[0]).astype(BF16)
    off = 0
    for o_ref in o_refs:
        w = o_ref.shape[-1]
        o_ref[0] = _dot(h, w_ref[:, off:off + w])
        off += w


def in_projection(x, shift, scale, g, w):
    bsz, seq, dm = x.shape
    tm = _tile(seq, 1024)
    widths = (W_GLA, W_CONV, W_S5, W_MLA)
    vec = pl.BlockSpec((1, 1, dm), lambda b, i: (b, 0, 0))
    return pl.pallas_call(
        _inproj_kernel, grid=(bsz, seq // tm),
        in_specs=[pl.BlockSpec((1, tm, dm), lambda b, i: (b, i, 0)), vec, vec,
                  pl.BlockSpec((1, dm), lambda b, i: (0, 0)),
                  pl.BlockSpec(w.shape, lambda b, i: (0, 0))],
        out_specs=[pl.BlockSpec((1, tm, wd), lambda b, i: (b, i, 0)) for wd in widths],
        out_shape=[jax.ShapeDtypeStruct((bsz, seq, wd), F32) for wd in widths],
        compiler_params=_params("parallel", "parallel"), name="in_projection")(x, shift, scale, g.reshape(1, dm), w)


def _rope_swap(t):
    s = t.reshape(t.shape[:-1] + (2, 2, ROPE_AXIS // 2))
    return jnp.stack([-s[..., 1, :], s[..., 0, :]], axis=-2).reshape(t.shape)


def assemble_w_in(w_in):
    edges = np.cumsum((0,) + IN_SPLITS)
    q, k, v, r, a_f, a_b, conv, s5, cq, ckv, kr = [w_in[:, edges[i]:edges[i + 1]] for i in range(len(IN_SPLITS))]
    pad = lambda t, w: jnp.pad(t, ((0, 0), (0, w - t.shape[1])))
    cols = [q, k, v, r, pad(a_f, 128), pad(a_b, 128), conv, s5, cq, ckv, pad(jnp.concatenate([kr, _rope_swap(kr)], 1), 128)]
    return jnp.concatenate(cols, axis=1).astype(BF16)


def _gla_kernel(z_ref, zc_ref, wa_ref, ba_ref, g_ref, o_ref, oc_ref, of_scr, ob_scr, dec_scr, st_scr, *, seq, ctx_len):
    c = GLA_CHUNK
    nh = GLA_HEADS
    kw = nh * GLA_DK
    vw = nh * GLA_DV
    iota = lambda shape, d: lax.broadcasted_iota(jnp.int32, shape, d)
    ra, ca = iota((nh * 2 * c, 2 * c), 0) & (2 * c - 1), iota((nh * 2 * c, 2 * c), 1)
    causal = ((ra < c) & (ca <= ra)) | ((ra >= c) & (ca >= ra))
    fwd_rows = iota((2 * c, 1), 0) < c
    k_of = lambda t: t >> int(math.log2(GLA_DK))
    v_of = lambda t: t >> int(math.log2(GLA_DV))
    k_head = [(k_of(iota((1, kw), 1)) == h).astype(F32) for h in range(nh)]
    v_head = [(v_of(iota((1, vw), 1)) == h).astype(F32) for h in range(nh)]
    st_rows = iota((2 * vw, kw), 0)
    st_mask = (v_of(st_rows & (vw - 1)) == k_of(iota((2 * vw, kw), 1))).astype(F32)
    st_fwd = st_rows < vw
    head_mean = ((v_of(iota((vw, vw), 0)) == v_of(iota((vw, vw), 1))).astype(F32) * (1.0 / GLA_DV)).astype(BF16)

    def decays(ref, n_rows, base):
        rows = GLA_FINISH_ROWS if n_rows % GLA_FINISH_ROWS == 0 else c
        rr, cc = iota((rows, rows), 0), iota((rows, rows), 1)
        same = (rr >> int(math.log2(c))) == (cc >> int(math.log2(c)))
        tri_f = (same & (cc <= rr)).astype(F32).astype(BF16)
        tri_b = (same & (cc >= rr)).astype(F32).astype(BF16)

        @pl.loop(0, n_rows // rows)
        def _(t):
            s = pl.multiple_of(t * rows, rows)
            a_low = ref[0, pl.ds(s, rows), 2 * kw + 2 * vw:2 * kw + 2 * vw + 256].astype(BF16)
            zl = _dot(a_low, wa_ref[...]) + ba_ref[...]
            la = (jnp.minimum(zl, 0.0) - jnp.log1p(jnp.exp(-jnp.abs(zl)))) / GLA_TAU
            at = pl.ds(pl.multiple_of(base + s, rows), rows)
            dec_scr[at, :kw] = _dot_exact_lhs(tri_f, la[:, :kw])
            dec_scr[at, kw:] = _dot_exact_lhs(tri_b, la[:, kw:])

    def pair(ref, sf, sb, base, st):
        blk_f, blk_b = ref[0, pl.ds(sf, c), :], ref[0, pl.ds(sb, c), :]
        blk = jnp.concatenate([blk_f, blk_b], axis=0)
        q = blk[:, 0:kw] * (GLA_DK ** -0.5)
        k = blk[:, kw:2 * kw]
        vb = blk[:, 2 * kw:2 * kw + vw].astype(BF16)
        b = jnp.concatenate([dec_scr[pl.ds(pl.multiple_of(base + sf, c), c), :kw],
                             dec_scr[pl.ds(pl.multiple_of(base + sb, c), c), kw:]], axis=0)
        b_end = jnp.where(fwd_rows, b[c - 1:c, :], b[c:c + 1, :])
        q_dec = q * jnp.exp(b)
        k_inv = (k * jnp.exp(-b)).astype(BF16)
        k_tail = (k * jnp.exp(b_end - b)).astype(BF16)
        qs = jnp.concatenate([q_dec * k_head[h] for h in range(nh)], axis=0).astype(BF16)
        att = lax.dot_general(qs, k_inv, NT_DIMS, preferred_element_type=F32)
        att = jnp.where(causal, att, 0.0).astype(BF16)
        o_all = _dot(att, vb)
        o = o_all[0:2 * c] * v_head[0]
        for h in range(1, nh):
            o = o + o_all[h * 2 * c:(h + 1) * 2 * c] * v_head[h]
        both = lax.dot_general(q_dec.astype(BF16), st.astype(BF16), NT_DIMS, preferred_element_type=F32)
        o = o + jnp.where(fwd_rows, both[:, :vw], both[:, vw:])
        v2 = jnp.concatenate([jnp.where(fwd_rows, vb, 0), jnp.where(fwd_rows, 0, vb)], axis=1)
        upd = lax.dot_general(v2, k_tail, TN_DIMS, preferred_element_type=F32)
        decay = jnp.where(st_fwd, jnp.exp(b[c - 1:c, :]), jnp.exp(b[c:c + 1, :]))
        return o, st * decay + upd * st_mask

    def scan(ref, n_chunks, base):
        unroll = GLA_UNROLL if n_chunks % GLA_UNROLL == 0 else 1

        @pl.loop(0, n_chunks // unroll)
        def _(it):
            st = st_scr[...]
            for u in range(unroll):
                n = it * unroll + u
                sf = pl.multiple_of(n * c, c)
                sb = pl.multiple_of((n_chunks - 1 - n) * c, c)
                o, st = pair(ref, sf, sb, base, st)
                of_scr[pl.ds(pl.multiple_of(base + sf, c), c), :] = o[:c]
                ob_scr[pl.ds(pl.multiple_of(base + sb, c), c), :] = o[c:]
            st_scr[...] = st

    def finish(ref, out_ref, n_rows, base):
        rows = GLA_FINISH_ROWS if n_rows % GLA_FINISH_ROWS == 0 else c

        @pl.loop(0, n_rows // rows)
        def _(t):
            s = pl.multiple_of(t * rows, rows)
            at = pl.ds(pl.multiple_of(base + s, rows), rows)
            o = of_scr[at, :] + ob_scr[at, :]
            ms = _dot_exact_rhs(o * o, head_mean)
            gate = ref[0, pl.ds(s, rows), 2 * kw + vw:2 * kw + 2 * vw]
            out_ref[0, pl.ds(s, rows), :] = (o * lax.rsqrt(ms + EPS) * g_ref[...] * _silu(gate)).astype(BF16)

    st_scr[...] = jnp.zeros_like(st_scr)
    decays(zc_ref, ctx_len, 0)
    decays(z_ref, seq, ctx_len)
    scan(zc_ref, ctx_len // c, 0)
    scan(z_ref, seq // c, ctx_len)
    finish(zc_ref, oc_ref, ctx_len, 0)
    finish(z_ref, o_ref, seq, ctx_len)


def gla_mixer(zg, zgc, w_a2, b_a, norm_g):
    bsz, seq, wd = zg.shape
    ctx_len = zgc.shape[1]
    vw = GLA_HEADS * GLA_DV
    kw = GLA_HEADS * GLA_DK
    wa = jnp.pad(w_a2, ((0, 0), (0, 128 - GLA_LOWRANK), (0, 0)))
    zero = jnp.zeros_like(wa[0])
    wa = jnp.block([[wa[0], zero], [zero, wa[1]]]).astype(BF16)
    full = lambda shape: pl.BlockSpec(shape, lambda b: (0,) * len(shape))
    return pl.pallas_call(
        functools.partial(_gla_kernel, seq=seq, ctx_len=ctx_len), grid=(bsz,),
        in_specs=[pl.BlockSpec((1, seq, wd), lambda b: (b, 0, 0)),
                  pl.BlockSpec((1, ctx_len, wd), lambda b: (b, 0, 0)),
                  full((256, 2 * kw)), full((1, 2 * kw)), full((1, vw))],
        out_specs=[pl.BlockSpec((1, seq, vw), lambda b: (b, 0, 0)),
                   pl.BlockSpec((1, ctx_len, vw), lambda b: (b, 0, 0))],
        out_shape=[jax.ShapeDtypeStruct((bsz, seq, vw), BF16), jax.ShapeDtypeStruct((bsz, ctx_len, vw), BF16)],
        scratch_shapes=[pltpu.VMEM((seq + ctx_len, vw), F32), pltpu.VMEM((seq + ctx_len, vw), F32),
                        pltpu.VMEM((seq + ctx_len, 2 * kw), F32), pltpu.VMEM((2 * vw, kw), F32)],
        compiler_params=_params("parallel"), name="gla_mixer")(
            zg, zgc, wa, b_a.reshape(1, 2 * kw), norm_g.reshape(1, vw))


def _conv_kernel(u_ref, dw_ref, dwb_ref, lng_ref, lnb_ref, pw_ref, pwb_ref, o_ref, h_scr, *, seq, rows):
    halo = CONV_HALO
    h_scr[0:halo, :] = jnp.zeros((halo, CONV_C), F32)
    h_scr[halo + seq:2 * halo + seq, :] = jnp.zeros((halo, CONV_C), F32)

    @pl.loop(0, seq // rows)
    def _(t):
        s = pl.multiple_of(t * rows, rows)
        u = u_ref[0, pl.ds(s, rows), :]
        h_scr[pl.ds(pl.multiple_of(halo + s, 8), rows), :] = u[:, :CONV_C] * jax.nn.sigmoid(u[:, CONV_C:])

    @pl.loop(0, seq // rows)
    def _(t):
        s = pl.multiple_of(t * rows, rows)
        win = h_scr[pl.ds(s, rows + 2 * halo), :]
        first = halo - CONV_K // 2
        acc = jnp.broadcast_to(dwb_ref[...], (rows, CONV_C))
        for r in range(8):
            taps = [k for k in range(CONV_K) if (first + k) % 8 == r]
            shifted = win if r == 0 else pltpu.roll(win, win.shape[0] - r, axis=0)
            for k in taps:
                at = 8 * ((first + k) // 8)
                acc = acc + shifted[at:at + rows] * dw_ref[k:k + 1, :]
        mu = jnp.mean(acc, axis=-1, keepdims=True)
        var = jnp.mean(jnp.square(acc - mu), axis=-1, keepdims=True)
        y = _silu((acc - mu) * lax.rsqrt(var + EPS) * lng_ref[...] + lnb_ref[...])
        o_ref[0, pl.ds(s, rows), :] = (_dot(y.astype(BF16), pw_ref[...]) + pwb_ref[...]).astype(BF16)


def conformer_conv(u, dw_w, dw_b, ln_g, ln_b, pw_w, pw_b):
    bsz, seq, _ = u.shape
    rows = _tile(seq, 128)
    full = lambda shape: pl.BlockSpec(shape, lambda b: (0,) * len(shape))
    row = lambda t: t.reshape(1, CONV_C)
    return pl.pallas_call(
        functools.partial(_conv_kernel, seq=seq, rows=rows), grid=(bsz,),
        in_specs=[pl.BlockSpec((1, seq, 2 * CONV_C), lambda b: (b, 0, 0)),
                  full((CONV_K + 1, CONV_C)), full((1, CONV_C)), full((1, CONV_C)), full((1, CONV_C)),
                  full((CONV_C, CONV_C)), full((1, CONV_C))],
        out_specs=pl.BlockSpec((1, seq, CONV_C), lambda b: (b, 0, 0)),
        out_shape=jax.ShapeDtypeStruct((bsz, seq, CONV_C), BF16),
        scratch_shapes=[pltpu.VMEM((seq + 2 * CONV_HALO, CONV_C), F32)],
        compiler_params=_params("parallel"), name="conformer_conv")(
            u, jnp.pad(dw_w, ((0, 1), (0, 0))), row(dw_b), row(ln_g), row(ln_b), pw_w.astype(BF16), row(pw_b))


def s5_matrices(lam_re, lam_im, log_dt, b_re, b_im, c_re, c_im):
    dt = jnp.exp(log_dt)[:, None]
    mag = jnp.exp(lam_re * dt)
    a_re, a_im = mag * jnp.cos(lam_im * dt), mag * jnp.sin(lam_im * dt)
    den = lam_re * lam_re + lam_im * lam_im
    f_re = ((a_re - 1.0) * lam_re + a_im * lam_im) / den
    f_im = (a_im * lam_re - (a_re - 1.0) * lam_im) / den
    bb_re = f_re[..., None] * b_re - f_im[..., None] * b_im
    bb_im = f_re[..., None] * b_im + f_im[..., None] * b_re
    eye = jnp.eye(S5_NG, dtype=F32)
    blk_b = lambda t: jnp.einsum("gph,gk->ghkp", t, eye).reshape(S5_C, S5_STATE)
    blk_c = lambda t: jnp.einsum("ghp,gk->gpkh", t, eye).reshape(S5_STATE, S5_C)
    a = jnp.stack([a_re.reshape(S5_STATE), a_im.reshape(S5_STATE)])
    b_mat = jnp.concatenate([blk_b(bb_re), blk_b(bb_im)], axis=1).astype(BF16)
    c_mat = jnp.concatenate([blk_c(c_re), -blk_c(c_im)], axis=0).astype(BF16)
    return a, b_mat, c_mat


def _s5_scan_kernel(uf_ref, ub_ref, a_ref, b_ref, c_ref, h0_ref, yf_ref, yb_ref, hl_ref, hs_f, hs_b, st_scr,
                    *, steps, bsz):
    ns = S5_STATE

    @pl.when(pl.program_id(0) == 0)
    def _():
        st_scr[...] = h0_ref[...]

    for d, (u_ref, hs) in enumerate(((uf_ref, hs_f), (ub_ref, hs_b))):
        u = u_ref[...].reshape(steps * bsz, S5_C).astype(BF16)
        hs[...] = _dot(u, b_ref[d])
    for d, hs in enumerate((hs_f, hs_b)):
        for s0 in range(0, ns, S5_STRIP):
            re, im = slice(s0, s0 + S5_STRIP), slice(ns + s0, ns + s0 + S5_STRIP)
            a_re, a_im = a_ref[d, 0:1, re], a_ref[d, 1:2, re]
            h_re, h_im = st_scr[d, :, re], st_scr[d, :, im]
            for j in range(steps):
                rows = pl.ds((steps - 1 - j if d == 1 else j) * bsz, bsz)
                h_re, h_im = (a_re * h_re - a_im * h_im + hs[rows, re], a_re * h_im + a_im * h_re + hs[rows, im])
                hs[rows, re] = h_re
                hs[rows, im] = h_im
            st_scr[d, :, re] = h_re
            st_scr[d, :, im] = h_im
    yf_ref[...] = _dot(hs_f[...].astype(BF16), c_ref[0]).reshape(steps, bsz, S5_C)
    yb_ref[...] = _dot(hs_b[...].astype(BF16), c_ref[1]).reshape(steps, bsz, S5_C)
    hl_ref[...] = st_scr[...]


def s5_scan(u_tm, a, b_mat, c_mat, h0):
    seq, bsz, _ = u_tm.shape
    steps = _tile(seq, 32)
    n = seq // steps
    full = lambda t: pl.BlockSpec(t.shape, lambda i: (0,) * t.ndim)
    fwd = pl.BlockSpec((steps, bsz, S5_C), lambda i: (i, 0, 0))
    bwd = pl.BlockSpec((steps, bsz, S5_C), lambda i: (n - 1 - i, 0, 0))
    y = jax.ShapeDtypeStruct((seq, bsz, S5_C), F32)
    return pl.pallas_call(
        functools.partial(_s5_scan_kernel, steps=steps, bsz=bsz), grid=(n,),
        in_specs=[fwd, bwd, full(a), full(b_mat), full(c_mat), full(h0)],
        out_specs=[fwd, bwd, full(h0)],
        out_shape=[y, y, jax.ShapeDtypeStruct(h0.shape, F32)],
        scratch_shapes=[pltpu.VMEM((steps * bsz, 2 * S5_STATE), F32), pltpu.VMEM((steps * bsz, 2 * S5_STATE), F32),
                        pltpu.VMEM(h0.shape, F32)],
        compiler_params=_params("arbitrary"), name="s5_scan")(u_tm, u_tm, a, b_mat, c_mat, h0)


def _s5_out_kernel(u_ref, yf_ref, yb_ref, d_ref, w_ref, b_ref, o_ref):
    y = d_ref[...] * u_ref[...] + yf_ref[...] + yb_ref[...]
    z = _dot(y.astype(BF16), w_ref[...]) + b_ref[...]
    o_ref[...] = (z[:, :S5_C] * jax.nn.sigmoid(z[:, S5_C:])).astype(BF16)


def s5_output(u, y_f, y_b, d_skip, glu_w, glu_b):
    n = u.shape[0]
    tm = _tile(n, 1024)
    rows = pl.BlockSpec((tm, S5_C), lambda i: (i, 0))
    full = lambda shape: pl.BlockSpec(shape, lambda i: (0,) * len(shape))
    return pl.pallas_call(
        _s5_out_kernel, grid=(n // tm,),
        in_specs=[rows, rows, rows, full((1, S5_C)), full((S5_C, 2 * S5_C)), full((1, 2 * S5_C))],
        out_specs=rows, out_shape=jax.ShapeDtypeStruct((n, S5_C), BF16),
        compiler_params=_params("parallel"), name="s5_output")(
            u, y_f, y_b, d_skip.reshape(1, S5_C), glu_w.astype(BF16), glu_b.reshape(1, 2 * S5_C))


def s5_mixer(zs, zsc, lam_re, lam_im, log_dt, b_re, b_im, c_re, c_im, d_skip, glu_w, glu_b):
    bsz, seq, _ = zs.shape
    ctx_len = zsc.shape[1]
    u = jnp.transpose(zs, (1, 0, 2))
    uc = jnp.transpose(zsc, (1, 0, 2))
    mats = [s5_matrices(lam_re[d], lam_im[d], log_dt[d], b_re[d], b_im[d], c_re[d], c_im[d]) for d in range(2)]
    a, b_mat, c_mat = [jnp.stack(t) for t in zip(*mats)]
    yc_f, yc_b, hc = s5_scan(uc, a, b_mat, c_mat, jnp.zeros((2, bsz, 2 * S5_STATE), F32))
    y_f, y_b, _ = s5_scan(u, a, b_mat, c_mat, hc)
    ys, ycs = (y_f, y_b), (yc_f, yc_b)
    flat = lambda t: t.reshape(-1, S5_C)
    o = s5_output(flat(u), flat(ys[0]), flat(ys[1]), d_skip, glu_w, glu_b).reshape(seq, bsz, S5_C)
    oc = s5_output(flat(uc), flat(ycs[0]), flat(ycs[1]), d_skip, glu_w, glu_b).reshape(ctx_len, bsz, S5_C)
    return jnp.transpose(o, (1, 0, 2)), jnp.transpose(oc, (1, 0, 2))


def rope_tables(seq_len, rotate):
    if not rotate:
        return jnp.ones((seq_len, MLA_ROPE), F32), jnp.zeros((seq_len, MLA_ROPE), F32)
    rows = seq_len // GRID_W
    row = jnp.broadcast_to(jnp.arange(rows, dtype=F32)[:, None], (rows, GRID_W)).reshape(seq_len)
    col = jnp.broadcast_to(jnp.arange(GRID_W, dtype=F32)[None, :], (rows, GRID_W)).reshape(seq_len)
    inv_freq = ROPE_BASE ** (-jnp.arange(ROPE_AXIS // 2, dtype=F32) / (ROPE_AXIS // 2))
    ang = jnp.stack([row[:, None] * inv_freq, col[:, None] * inv_freq], axis=1)
    full = lambda t: jnp.broadcast_to(t[:, :, None, :], (seq_len, 2, 2, ROPE_AXIS // 2)).reshape(seq_len, MLA_ROPE)
    return full(jnp.cos(ang)), full(jnp.sin(ang))


def mla_tables(seq_len, rotate):
    cos, sin = rope_tables(seq_len, rotate)
    head = lambda rope, fill: jnp.concatenate(
        [jnp.full((seq_len, MLA_NOPE), fill, F32), rope, jnp.zeros((seq_len, MLA_HEAD_PAD - MLA_NOPE - MLA_ROPE), F32)], 1)
    cos_q = jnp.tile(head(cos, 1.0), (1, MLA_HEADS))
    sin_q = jnp.tile(head(sin, 0.0), (1, MLA_HEADS))
    cs_k = jnp.concatenate([cos, sin, jnp.zeros((seq_len, 128 - 2 * MLA_ROPE), F32)], 1)
    return cos_q, sin_q, cs_k


def mla_weights(w_uq, w_ukv):
    hd = MLA_NOPE + MLA_ROPE
    zq = jnp.zeros((MLA_Q_LORA, MLA_HEAD_PAD - hd), F32)
    zn = jnp.zeros((MLA_Q_LORA, MLA_NOPE), F32)
    wq, wq_sw, wk, wv = [], [], [], []
    for h in range(MLA_HEADS):
        qh = w_uq[:, h * hd:(h + 1) * hd]
        wq += [qh, zq]
        wq_sw += [zn, _rope_swap(qh[:, MLA_NOPE:]), zq]
        kvh = w_ukv[:, h * (MLA_NOPE + MLA_V):(h + 1) * (MLA_NOPE + MLA_V)]
        wk += [kvh[:, :MLA_NOPE], jnp.zeros((MLA_KV_LORA, MLA_HEAD_PAD - MLA_NOPE), F32)]
        wv += [kvh[:, MLA_NOPE:]]
    place = np.zeros((128, MLA_HEADS * MLA_HEAD_PAD), np.float32)
    for h in range(MLA_HEADS):
        for j in range(MLA_ROPE):
            place[j, h * MLA_HEAD_PAD + MLA_NOPE + j] = 1.0
            place[MLA_ROPE + j, h * MLA_HEAD_PAD + MLA_NOPE + j] = 1.0
    cat = lambda ts: jnp.concatenate(ts, axis=1).astype(BF16)
    return cat(wq), cat(wq_sw), cat(wk), cat(wv), jnp.asarray(place).astype(BF16)


def _mla_prep_kernel(z_ref, qg_ref, kg_ref, wq_ref, wqs_ref, wk_ref, wv_ref, pl_ref, cq_ref, sq_ref, csk_ref,
                     q_ref, k_ref, v_ref):
    z = z_ref[0]
    norm = lambda t, g: (t * lax.rsqrt(jnp.mean(t * t, axis=-1, keepdims=True) + EPS) * g).astype(BF16)
    cq = norm(z[:, :MLA_Q_LORA], qg_ref[...])
    ckv = norm(z[:, MLA_Q_LORA:MLA_Q_LORA + MLA_KV_LORA], kg_ref[...])
    scale = (MLA_NOPE + MLA_ROPE) ** -0.5
    q = _dot(cq, wq_ref[...]) * cq_ref[...] + _dot(cq, wqs_ref[...]) * sq_ref[...]
    q_ref[0] = (q * scale).astype(BF16)
    kr = z[:, MLA_Q_LORA + MLA_KV_LORA:] * csk_ref[...]
    k_ref[0] = (_dot(ckv, wk_ref[...]) + _dot_exact_rhs(kr, pl_ref[...])).astype(BF16)
    v_ref[0] = _dot(ckv, wv_ref[...]).astype(BF16)


def mla_prep(zm, qn_g, kvn_g, weights, tables):
    bsz, seq, wd = zm.shape
    tm = _tile(seq, 1024)
    wq, wq_sw, wk, wv, place = weights
    cos_q, sin_q, cs_k = tables
    qw, vw = MLA_HEADS * MLA_HEAD_PAD, MLA_HEADS * MLA_V
    full = lambda t: pl.BlockSpec(t.shape, lambda b, i: (0,) * t.ndim)
    pos = lambda t: pl.BlockSpec((tm, t.shape[1]), lambda b, i: (i, 0))
    out = lambda w: pl.BlockSpec((1, tm, w), lambda b, i: (b, i, 0))
    qg, kg = qn_g.reshape(1, -1), kvn_g.reshape(1, -1)
    return pl.pallas_call(
        _mla_prep_kernel, grid=(bsz, seq // tm),
        in_specs=[pl.BlockSpec((1, tm, wd), lambda b, i: (b, i, 0)), full(qg), full(kg), full(wq), full(wq_sw),
                  full(wk), full(wv), full(place), pos(cos_q), pos(sin_q), pos(cs_k)],
        out_specs=[out(qw), out(qw), out(vw)],
        out_shape=[jax.ShapeDtypeStruct((bsz, seq, qw), BF16), jax.ShapeDtypeStruct((bsz, seq, qw), BF16),
                   jax.ShapeDtypeStruct((bsz, seq, vw), BF16)],
        compiler_params=_params("parallel", "parallel"), name="mla_prep")(
            zm, qg, kg, wq, wq_sw, wk, wv, place, cos_q, sin_q, cs_k)


def _mla_attn_kernel(q_ref, *refs, n_seg):
    k_refs, v_refs, o_ref = refs[:n_seg], refs[n_seg:2 * n_seg], refs[2 * n_seg]
    outs = []
    for h in range(MLA_HEADS):
        q = q_ref[0, :, h * MLA_HEAD_PAD:(h + 1) * MLA_HEAD_PAD]
        s = [lax.dot_general(q, k_ref[0, :, h * MLA_HEAD_PAD:(h + 1) * MLA_HEAD_PAD], NT_DIMS,
                             preferred_element_type=F32) for k_ref in k_refs]
        m = functools.reduce(jnp.maximum, [jnp.max(t, axis=-1, keepdims=True) for t in s])
        p = [jnp.exp(t - m) for t in s]
        den = functools.reduce(jnp.add, [jnp.sum(t, axis=-1, keepdims=True) for t in p])
        o = functools.reduce(jnp.add, [_dot(t.astype(BF16), v_ref[0, :, h * MLA_V:(h + 1) * MLA_V])
                                       for t, v_ref in zip(p, v_refs)])
        outs.append(o / den)
    o_ref[0] = jnp.concatenate(outs, axis=-1).astype(BF16)


def mla_attention(q, ks, vs):
    bsz, seq, qw = q.shape
    tq = _tile(seq, 1024)
    vw = MLA_HEADS * MLA_V
    seg = lambda t: pl.BlockSpec((1,) + t.shape[1:], lambda b, i: (b, 0, 0))
    return pl.pallas_call(
        functools.partial(_mla_attn_kernel, n_seg=len(ks)), grid=(bsz, seq // tq),
        in_specs=[pl.BlockSpec((1, tq, qw), lambda b, i: (b, i, 0))] + [seg(t) for t in ks] + [seg(t) for t in vs],
        out_specs=pl.BlockSpec((1, tq, vw), lambda b, i: (b, i, 0)),
        out_shape=jax.ShapeDtypeStruct((bsz, seq, vw), BF16),
        compiler_params=_params("parallel", "parallel"), name="mla_attention")(q, *ks, *vs)


def _outproj_kernel(a_ref, b_ref, c_ref, d_ref, w_ref, x_ref, g1_ref, sh_ref, sc_ref, g_ref, rw_ref,
                    xo_ref, h_ref, lg_ref):
    mix = None
    for j, o_ref in enumerate((a_ref, b_ref, c_ref, d_ref)):
        wd = o_ref.shape[-1]
        part = _dot(o_ref[0], w_ref[j * wd:(j + 1) * wd, :])
        mix = part if mix is None else mix + part
    x = x_ref[0] + g1_ref[0] * mix
    xo_ref[0] = x
    ms = jnp.mean(x * x, axis=-1, keepdims=True)
    h = x * lax.rsqrt(ms + EPS) * g_ref[...] * (1.0 + sc_ref[0]) + sh_ref[0]
    _store_row_slabs(h_ref, _pack_bf16_pairs(h))
    lg_ref[...] =lax.dot_general(rw_ref[...], h, NT_DIMS, precision=HIGHEST, preferred_element_type=F32)


def out_projection(parts, w_out, x, g1, shift, scale, g, router_w):
    bsz, seq, dm = x.shape
    tm = _tile(seq, 1024)
    nt = seq // tm
    ne = router_w.shape[1]
    groups = dm // 2 // 128
    rows = lambda w: pl.BlockSpec((1, tm, w), lambda b, i: (b, i, 0))
    vec = pl.BlockSpec((1, 1, dm), lambda b, i: (b, 0, 0))
    full = lambda shape: pl.BlockSpec(shape, lambda b, i: (0,) * len(shape))
    return pl.pallas_call(
        _outproj_kernel, grid=(bsz, nt),
        in_specs=[rows(p.shape[-1]) for p in parts] + [full(w_out.shape), rows(dm), vec, vec, vec, full((1, dm)),
                                                       full((ne, dm))],
        out_specs=[rows(dm), pl.BlockSpec((tm * groups, 128), lambda b, i: (b * nt + i, 0)),
                   pl.BlockSpec((ne, tm), lambda b, i: (0, b * nt + i))],
        out_shape=[jax.ShapeDtypeStruct((bsz, seq, dm), F32), jax.ShapeDtypeStruct((bsz * seq * groups, 128), jnp.int32),
                   jax.ShapeDtypeStruct((ne, bsz * seq), F32)],
        compiler_params=_params("parallel", "parallel"), name="out_projection")(
            *parts, w_out.astype(BF16), x, g1, shift, scale, g.reshape(1, dm), router_w.T)


def _route_kernel(lg_ref, b_ref, e_ref, w_ref):
    ne, tt = lg_ref.shape
    per = ne // N_GROUPS
    neg = -jnp.inf
    scores = jax.nn.sigmoid(lg_ref[...])
    biased = scores + b_ref[...]
    v3 = biased.reshape(N_GROUPS, per, tt)
    e_in = lax.broadcasted_iota(jnp.int32, v3.shape, 1).astype(F32)
    m1 = jnp.max(v3, axis=1, keepdims=True)
    i1 = jnp.min(jnp.where(v3 == m1, e_in, float(per)), axis=1, keepdims=True)
    m2 = jnp.max(jnp.where(e_in == i1, neg, v3), axis=1, keepdims=True)
    grp = (m1 + m2).reshape(N_GROUPS, tt)

    def pick(cur, count):
        ids = lax.broadcasted_iota(jnp.int32, cur.shape, 0).astype(F32)
        marks = jnp.zeros(cur.shape, F32)
        picked = []
        for _ in range(count):
            m = jnp.max(cur, axis=0, keepdims=True)
            first = jnp.min(jnp.where(cur == m, ids, float(cur.shape[0])), axis=0, keepdims=True)
            hit = ids == first
            marks = jnp.where(hit, 1.0, marks)
            cur = jnp.where(hit, neg, cur)
            picked.append(first)
        return marks, picked

    grp_on, _ = pick(grp, TOPK_GROUPS)
    exp_on = jnp.broadcast_to(grp_on.reshape(N_GROUPS, 1, tt), v3.shape).reshape(ne, tt)
    chosen, picked = pick(jnp.where(exp_on > 0.0, biased, neg), TOP_K)
    w = scores * chosen
    gate = w / jnp.sum(w, axis=0, keepdims=True) * ROUTED_SCALE
    ids = lax.broadcasted_iota(jnp.int32, gate.shape, 0).astype(F32)
    e_ref[...] = jnp.concatenate(picked, axis=0).astype(jnp.int32)
    w_ref[...] = jnp.concatenate([jnp.sum(jnp.where(ids == p, gate, 0.0), axis=0, keepdims=True) for p in picked], axis=0)


def route(logits_t, router_b):
    ne, n = logits_t.shape
    tt = _tile(n, 1024)
    out = pl.BlockSpec((TOP_K, tt), lambda i: (0, i))
    return pl.pallas_call(
        _route_kernel, grid=(n // tt,),
        in_specs=[pl.BlockSpec((ne, tt), lambda i: (0, i)), pl.BlockSpec((ne, 1), lambda i: (0, 0))],
        out_specs=[out, out],
        out_shape=[jax.ShapeDtypeStruct((TOP_K, n), jnp.int32), jax.ShapeDtypeStruct((TOP_K, n), F32)],
        compiler_params=_params("parallel"), name="route")(logits_t, router_b.reshape(ne, 1))


def moe_plan(eid, wgt, chunk, xg, yg):
    n = eid.shape[1]
    nc, na = n // chunk, chunk * TOP_K
    ns = na // MOE_ROWS + N_EXPERTS
    key = eid.T.reshape(nc, na) * na + jnp.arange(na, dtype=jnp.int32)
    skey, sw = lax.sort((key, wgt.T.reshape(nc, na)), dimension=1, num_keys=1)
    e_sorted, tok = skey // na, (skey % na) // TOP_K
    experts = jnp.arange(N_EXPERTS, dtype=jnp.int32)
    ends = jnp.sum(e_sorted[:, None, :] <= experts[None, :, None], axis=-1, dtype=jnp.int32)
    starts = jnp.concatenate([jnp.zeros((nc, 1), jnp.int32), ends[:, :-1]], axis=1)
    pad_end = jnp.cumsum(-(-(ends - starts) // MOE_ROWS) * MOE_ROWS, axis=1)
    pad_start = jnp.concatenate([jnp.zeros((nc, 1), jnp.int32), pad_end[:, :-1]], axis=1)
    first = jnp.arange(ns, dtype=jnp.int32) * MOE_ROWS
    exp = jnp.minimum(jnp.sum(pad_end[:, None, :] <= first[None, :, None], axis=-1, dtype=jnp.int32), N_EXPERTS - 1)
    active = first[None, :] < pad_end[:, -1:]
    take = lambda t, i: jnp.take_along_axis(t, i, axis=1)
    rank = first[None, :, None] + jnp.arange(MOE_ROWS, dtype=jnp.int32) - take(pad_start, exp)[..., None]
    real = (rank < take(ends - starts, exp)[..., None]) & active[..., None]
    src = jnp.clip(take(starts, exp)[..., None] + rank, 0, na - 1).reshape(nc, ns * MOE_ROWS)
    rows = lambda t: take(t, src).reshape(nc, ns, MOE_ROWS)
    flat = lambda t: t.reshape(nc * ns, 1, MOE_ROWS)
    gather_at = flat(jnp.where(real, rows(tok), 0) * xg)
    scatter_at = flat(jnp.where(real, rows(tok), chunk) * yg)
    weight = flat(jnp.where(real, rows(sw), 0.0))
    return gather_at, scatter_at, weight, exp.reshape(nc * ns), active.astype(jnp.int32).reshape(nc * ns)


def _moe_tick(x_ref, y_ref, idx_ref, w_ref, tick, wgu_ref, wd_ref, gather_to, mm_from, mm_to, scatter_from):
    bm = MOE_ROWS
    stride = bm + 1
    xg = mm_from.shape[0] // stride
    yg = mm_to.shape[0] // stride
    for mi in range(bm):
        src = pl.multiple_of(idx_ref[0, 0, 2 * tick * bm + mi], xg)
        gather_to[pl.ds(mi, xg, stride=stride), :] = x_ref[0, pl.ds(src, xg), :]
    low, high = _unpack_bf16_pairs([mm_from[j * stride:j * stride + bm, :] for j in range(xg)])
    half = low.shape[1]
    ff = wd_ref.shape[1]
    up = _dot(low, wgu_ref[0, :half, :]) + _dot(high, wgu_ref[0, half:, :])
    act = (_silu(up[:, :ff]) * up[:, ff:]).astype(BF16)
    out = _dot(act, wd_ref[0])
    for j in range(yg):
        mm_to[j * stride:j * stride + bm, :] = out[:, j * 128:(j + 1) * 128]
    for base in range(0, bm, MOE_BATCH):
        at = [pl.multiple_of(idx_ref[0, 0, (2 * tick + 1) * bm + mi], yg) for mi in range(base, base + MOE_BATCH)]
        new = [y_ref[0, pl.ds(i, yg), :] + w_ref[0, 0, tick * bm + mi] * scatter_from[pl.ds(mi, yg, stride=stride), :]
               for i, mi in zip(at, range(base, base + MOE_BATCH))]
        for i, v in zip(at, new):
            y_ref[0, pl.ds(i, yg), :] = v


def _moe_routed_kernel(exp_ref, act_ref, x_ref, idx_ref, w_ref, wgua_ref, wda_ref, wgub_ref, wdb_ref,
                       y_ref, xt0, xt1, ot0, ot1, *, ns):
    c, g = pl.program_id(0), pl.program_id(1)

    @pl.when(g == 0)
    def _():
        y_ref[...] = jnp.zeros_like(y_ref)

    @pl.when((c == 0) & (g == 0))
    def _():
        xt1[...] = jnp.zeros_like(xt1)
        ot0[...] = jnp.zeros_like(ot0)
        ot1[...] = jnp.zeros_like(ot1)

    def live(step):
        return (step >= 0) & (step < ns) & (act_ref[c * ns + jnp.clip(step, 0, ns - 1)] != 0)

    t = 2 * g

    @pl.when(live(t - 2) | live(t - 1) | live(t) | live(t + 1))
    def _():
        _moe_tick(x_ref, y_ref, idx_ref, w_ref, 0, wgua_ref, wda_ref,
                  gather_to=xt0, mm_from=xt1, mm_to=ot1, scatter_from=ot0)
        _moe_tick(x_ref, y_ref, idx_ref, w_ref, 1, wgub_ref, wdb_ref,
                  gather_to=xt1, mm_from=xt0, mm_to=ot0, scatter_from=ot1)


def moe_routed(slabs, eid, wgt, wgu, wd):
    n = eid.shape[1]
    xg = slabs.shape[0] // n
    yg = 2 * xg
    dm, ff = wd.shape[2], wd.shape[1]
    chunk = _tile(n, MOE_CHUNK)
    nc = n // chunk
    ns = chunk * TOP_K // MOE_ROWS + N_EXPERTS
    assert ns % 2 == 0
    steps = (ns + 2) // 2
    last = ns - 1
    gather_at, scatter_at, weight, exp, active = moe_plan(eid, wgt, chunk, xg, yg)
    tick = jnp.arange(steps, dtype=jnp.int32)[:, None] * 2 + jnp.arange(2, dtype=jnp.int32)[None, :]
    gathered = jnp.minimum(tick, last)
    scattered = jnp.where(tick >= 2, tick - 2, last)
    per_step = lambda t, at: jnp.take(t.reshape(nc, ns, MOE_ROWS), at, axis=1)
    idx = jnp.concatenate([per_step(gather_at, gathered[:, 0]), per_step(scatter_at, scattered[:, 0]),
                           per_step(gather_at, gathered[:, 1]), per_step(scatter_at, scattered[:, 1])], axis=-1)
    wts = jnp.concatenate([per_step(weight, scattered[:, 0]), per_step(weight, scattered[:, 1])], axis=-1)
    smem = lambda width: pl.BlockSpec((1, 1, width), lambda c, g, e, a: (c * steps + g, 0, 0), memory_space=pltpu.SMEM)
    computed = lambda tk: lambda g: jnp.clip(2 * g + tk - 1, 0, last)
    expert = lambda shape, step: pl.BlockSpec((1,) + shape, lambda c, g, e, a: (e[c * ns + step(g)], 0, 0))
    once = pl.Buffered(1)
    stage = lambda groups, dtype: pltpu.VMEM((-(-groups * (MOE_ROWS + 1) // 8) * 8, 128), dtype)
    return pl.pallas_call(
        functools.partial(_moe_routed_kernel, ns=ns),
        grid_spec=pltpu.PrefetchScalarGridSpec(
            num_scalar_prefetch=2, grid=(nc, steps),
            in_specs=[pl.BlockSpec((1, chunk * xg, 128), lambda c, g, e, a: (c, 0, 0), pipeline_mode=once),
                      smem(4 * MOE_ROWS), smem(2 * MOE_ROWS),
                      expert((dm, 2 * ff), computed(0)), expert((ff, dm), computed(0)),
                      expert((dm, 2 * ff), computed(1)), expert((ff, dm), computed(1))],
            out_specs=pl.BlockSpec((1, chunk * yg + 8, 128), lambda c, g, e, a: (c, 0, 0), pipeline_mode=once),
            scratch_shapes=[stage(xg, jnp.int32), stage(xg, jnp.int32), stage(yg, F32), stage(yg, F32)]),
        out_shape=jax.ShapeDtypeStruct((nc, chunk * yg + 8, 128), F32),
        compiler_params=_params("arbitrary", "arbitrary"), name="moe_routed")(
            exp, active, slabs.reshape(nc, chunk * xg, 128), idx.reshape(nc * steps, 1, 4 * MOE_ROWS),
            wts.reshape(nc * steps, 1, 2 * MOE_ROWS), wgu, wd, wgu, wd)


def _moe_finish_kernel(y_ref, h_ref, x_ref, g2_ref, wg_ref, wu_ref, wd_ref, fg_ref, o_ref, *, final_norm):
    tm = x_ref.shape[0]
    low, high = _unpack_bf16_pairs(_load_row_slabs(h_ref, tm, h_ref.shape[0] // tm))
    routed = jnp.concatenate(_load_row_slabs(y_ref, tm, y_ref.shape[1] // tm, lead=(0,)), axis=1)
    half = low.shape[1]
    up = lambda w_ref: _dot(low, w_ref[:half, :]) + _dot(high, w_ref[half:, :])
    shared = _dot((_silu(up(wg_ref)) * up(wu_ref)).astype(BF16), wd_ref[...])
    x = x_ref[...] + g2_ref[0] * (routed + shared)
    if final_norm:
        x = x * lax.rsqrt(jnp.mean(x * x, axis=-1, keepdims=True) + EPS) * fg_ref[...]
    o_ref[...] = x


def moe_finish(y, slabs, x, g2, sg, su, sd, final_g, final_norm):
    bsz, seq, dm = x.shape
    n = bsz * seq
    tm = _tile(seq, 1024)
    nt = seq // tm
    xg, yg = slabs.shape[0] // n, dm // 128
    per_chunk = (y.shape[1] - 8) // yg // tm
    rows = lambda w: pl.BlockSpec((tm, w), lambda i: (i, 0))
    full = lambda t: pl.BlockSpec(t.shape, lambda i: (0,) * t.ndim)
    sg, su, sd, fg = sg.astype(BF16), su.astype(BF16), sd.astype(BF16), final_g.reshape(1, dm)
    out = pl.pallas_call(
        functools.partial(_moe_finish_kernel, final_norm=final_norm), grid=(n // tm,),
        in_specs=[pl.BlockSpec((1, tm * yg, 128), lambda i: (i // per_chunk, i % per_chunk, 0)),
                  pl.BlockSpec((tm * xg, 128), lambda i: (i, 0)), rows(dm),
                  pl.BlockSpec((1, 1, dm), lambda i: (i // nt, 0, 0)), full(sg), full(su), full(sd), full(fg)],
        out_specs=rows(dm), out_shape=jax.ShapeDtypeStruct((n, dm), F32),
        compiler_params=_params("parallel"), name="moe_finish")(y, slabs, x.reshape(n, dm), g2, sg, su, sd, fg)
    return out.reshape(bsz, seq, dm)


def kernel(x, c, ctx, c_ctx, ada_w, ada_b, norm1_g, norm2_g, w_in, w_out, gla_w_a2, gla_b_a, gla_norm_g,
           conv_dw_w, conv_dw_b, conv_ln_g, conv_ln_b, conv_pw_w, conv_pw_b,
           s5_lam_re, s5_lam_im, s5_log_dt, s5_b_re, s5_b_im, s5_c_re, s5_c_im, s5_d, s5_glu_w, s5_glu_b,
           mla_qn_g, mla_kvn_g, mla_w_uq, mla_w_ukv,
           moe_router_w, moe_router_b, moe_w_gate, moe_w_up, moe_w_down,
           shared_w_gate, shared_w_up, shared_w_down, final_g):
    bsz, seq, dm = x.shape
    ctx_len = ctx.shape[1]
    depth = ada_w.shape[0]
    cc = jnp.concatenate([c, c_ctx[None], jnp.zeros((-(bsz + 1) % 8, dm), F32)], axis=0)
    tables = mla_tables(seq, True)
    tables_c = mla_tables(ctx_len, False)
    xc = ctx
    for l in range(depth):
        last = l == depth - 1
        mod = ada_mod(cc, ada_w[l], ada_b[l])
        lat = [t.reshape(bsz, 1, dm) for t in jnp.split(mod[:bsz], 6, axis=-1)]
        con = [jnp.broadcast_to(t.reshape(1, 1, dm), (bsz, 1, dm)) for t in jnp.split(mod[bsz], 6)]
        sh1, sc1, g1, sh2, sc2, g2 = lat
        sh1c, sc1c, g1c, sh2c, sc2c, g2c = con
        w_in_l = assemble_w_in(w_in[l])
        zg, zv, zs, zm = in_projection(x, sh1, sc1, norm1_g[l], w_in_l)
        zgc, zvc, zsc, zmc = in_projection(xc, sh1c, sc1c, norm1_g[l], w_in_l)
        conv_p = (conv_dw_w[l], conv_dw_b[l], conv_ln_g[l], conv_ln_b[l], conv_pw_w[l], conv_pw_b[l])
        o_gla, oc_gla = gla_mixer(zg, zgc, gla_w_a2[l], gla_b_a[l], gla_norm_g[l])
        o_conv = conformer_conv(zv, *conv_p)
        o_s5, oc_s5 = s5_mixer(zs, zsc, s5_lam_re[l], s5_lam_im[l], s5_log_dt[l], s5_b_re[l], s5_b_im[l],
                               s5_c_re[l], s5_c_im[l], s5_d[l], s5_glu_w[l], s5_glu_b[l])
        mla_w = mla_weights(mla_w_uq[l], mla_w_ukv[l])
        q, k, v = mla_prep(zm, mla_qn_g[l], mla_kvn_g[l], mla_w, tables)
        qc, kc, vc = mla_prep(zmc, mla_qn_g[l], mla_kvn_g[l], mla_w, tables_c)
        o_mla = mla_attention(q, [k, kc], [v, vc])
        experts = (jnp.concatenate([moe_w_gate[l], moe_w_up[l]], axis=2).astype(BF16), moe_w_down[l].astype(BF16))
        shared = (shared_w_gate[l], shared_w_up[l], shared_w_down[l])

        def ffn(parts, x_in, gate1, shift, scale, gate2, final_norm):
            x_mid, slabs, logits = out_projection(parts, w_out[l], x_in, gate1, shift, scale, norm2_g[l], moe_router_w[l])
            y = moe_routed(slabs, *route(logits, moe_router_b[l]), *experts)
            return moe_finish(y, slabs, x_mid, gate2, *shared, final_g, final_norm)

        if not last:
            oc_conv = conformer_conv(zvc, *conv_p)
            oc_mla = mla_attention(qc, [kc], [vc])
            xc = ffn([oc_gla, oc_conv, oc_s5, oc_mla], xc, g1c, sh2c, sc2c, g2c, False)
        x = ffn([o_gla, o_conv, o_s5, o_mla], x, g1, sh2, sc2, g2, last)
    return x
```

```python
import functools
import math

import numpy as np
import jax
import jax.numpy as jnp
from jax import lax
from jax.experimental import pallas as pl
from jax.experimental.pallas import tpu as pltpu

F32 = jnp.float32
BF16 = jnp.bfloat16
HIGHEST = lax.Precision.HIGHEST
EPS = 1e-6

GRID_W = 64

GLA_HEADS = 4
GLA_DK = 32
GLA_DV = 64
GLA_LOWRANK = 16
GLA_TAU = 16.0
GLA_CHUNK = 64
GLA_UNROLL = 8
GLA_FINISH_ROWS = 256

CONV_C = 256
CONV_K = 31
CONV_HALO = 16

S5_C = 256
S5_GROUP = 16
S5_NG = 16
S5_P = 64
S5_STATE = S5_NG * S5_P
S5_STRIP = 256

MLA_HEADS = 4
MLA_NOPE = 64
MLA_ROPE = 32
MLA_V = 64
MLA_Q_LORA = 256
MLA_KV_LORA = 128
MLA_HEAD_PAD = 128
ROPE_AXIS = MLA_ROPE // 2
ROPE_BASE = 10000.0

N_EXPERTS = 64
TOP_K = 8
N_GROUPS = 8
TOPK_GROUPS = 4
ROUTED_SCALE = 2.5
MOE_CHUNK = 4096
MOE_ROWS = 256
MOE_BATCH = 8

IN_SPLITS = (128, 128, 256, 256, 16, 16, 512, 256, 256, 128, 32)
W_GLA, W_CONV, W_S5, W_MLA = 1024, 512, 256, 512

VMEM_LIMIT = 48 * 1024 * 1024

NT_DIMS = (((1,), (1,)), ((), ()))
TN_DIMS = (((0,), (0,)), ((), ()))


def _params(*sem, vmem=VMEM_LIMIT):
    return pltpu.CompilerParams(dimension_semantics=sem, vmem_limit_bytes=vmem)


def _silu(v):
    return v * jax.nn.sigmoid(v)


def _dot(a, b, **kw):
    return jnp.dot(a, b, preferred_element_type=F32, **kw)


def _split_bf16(x):
    hi = x.astype(BF16)
    rest = x - hi.astype(F32)
    mid = rest.astype(BF16)
    return hi, mid, (rest - mid.astype(F32)).astype(BF16)


def _dot_exact_lhs(m, x):
    n = x.shape[1]
    prod = _dot(m, jnp.concatenate(_split_bf16(x), axis=1))
    return prod[:, :n] + prod[:, n:2 * n] + prod[:, 2 * n:]


def _dot_exact_rhs(x, m):
    n = x.shape[0]
    prod = _dot(jnp.concatenate(_split_bf16(x), axis=0), m)
    return prod[:n] + prod[n:2 * n] + prod[2 * n:]


def _tile(n, pref):
    return pref if n % pref == 0 else n


def _pack_bf16_pairs(h):
    bits = pltpu.bitcast(h.astype(BF16).astype(F32), jnp.int32)
    w = h.shape[-1] // 2
    return lax.shift_right_logical(bits[:, :w], 16) | bits[:, w:]


def _store_row_slabs(ref, value):
    rows, g = value.shape[0], value.shape[1] // 128
    for j in range(g):
        ref[pl.ds(j, rows, stride=g), :] = value[:, j * 128:(j + 1) * 128]


def _load_row_slabs(ref, rows, g, lead=()):
    return [ref[lead + (pl.ds(j, rows, stride=g), slice(None))] for j in range(g)]


def _unpack_bf16_pairs(words):
    low = jnp.concatenate([pltpu.bitcast(v << 16, F32) for v in words], axis=1)
    high = jnp.concatenate([pltpu.bitcast(v & -65536, F32) for v in words], axis=1)
    return low.astype(BF16), high.astype(BF16)


def _ada_kernel(c_ref, w_ref, b_ref, o_ref):
    o_ref[...] = _dot(_silu(c_ref[...]), w_ref[...], precision=HIGHEST) + b_ref[...]


def ada_mod(cc, w, b):
    rows, dm = cc.shape
    n = w.shape[1]
    tn = _tile(n, 512)
    return pl.pallas_call(
        _ada_kernel, grid=(n // tn,),
        in_specs=[pl.BlockSpec((rows, dm), lambda j: (0, 0)),
                  pl.BlockSpec((dm, tn), lambda j: (0, j)),
                  pl.BlockSpec((1, tn), lambda j: (0, j))],
        out_specs=pl.BlockSpec((rows, tn), lambda j: (0, j)),
        out_shape=jax.ShapeDtypeStruct((rows, n), F32),
        compiler_params=_params("arbitrary"), name="ada_mod")(cc, w, b.reshape(1, n))


def _inproj_kernel(x_ref, sh_ref, sc_ref, g_ref, w_ref, *o_refs):
    x = x_ref[0]
    ms = jnp.mean(x * x, axis=-1, keepdims=True)
    h = (x * lax.rsqrt(ms + EPS) * g_ref[...] * (1.0 + sc_ref[0]) + sh_ref[0]).astype(BF16)
    off = 0
    for o_ref in o_refs:
        w = o_ref.shape[-1]
        o_ref[0] = _dot(h, w_ref[:, off:off + w])
        off += w


def in_projection(x, shift, scale, g, w):
    bsz, seq, dm = x.shape
    tm = _tile(seq, 1024)
    widths = (W_GLA, W_CONV, W_S5, W_MLA)
    vec = pl.BlockSpec((1, 1, dm), lambda b, i: (b, 0, 0))
    return pl.pallas_call(
        _inproj_kernel, grid=(bsz, seq // tm),
        in_specs=[pl.BlockSpec((1, tm, dm), lambda b, i: (b, i, 0)), vec, vec,
                  pl.BlockSpec((1, dm), lambda b, i: (0, 0)),
                  pl.BlockSpec(w.shape, lambda b, i: (0, 0))],
        out_specs=[pl.BlockSpec((1, tm, wd), lambda b, i: (b, i, 0)) for wd in widths],
        out_shape=[jax.ShapeDtypeStruct((bsz, seq, wd), F32) for wd in widths],
        compiler_params=_params("parallel", "parallel"), name="in_projection")(x, shift, scale, g.reshape(1, dm), w)


def _rope_swap(t):
    s = t.reshape(t.shape[:-1] + (2, 2, ROPE_AXIS // 2))
    return jnp.stack([-s[..., 1, :], s[..., 0, :]], axis=-2).reshape(t.shape)


def assemble_w_in(w_in):
    edges = np.cumsum((0,) + IN_SPLITS)
    q, k, v, r, a_f, a_b, conv, s5, cq, ckv, kr = [w_in[:, edges[i]:edges[i + 1]] for i in range(len(IN_SPLITS))]
    pad = lambda t, w: jnp.pad(t, ((0, 0), (0, w - t.shape[1])))
    cols = [q, k, v, r, pad(a_f, 128), pad(a_b, 128), conv, s5, cq, ckv, pad(jnp.concatenate([kr, _rope_swap(kr)], 1), 128)]
    return jnp.concatenate(cols, axis=1).astype(BF16)


def _gla_kernel(z_ref, zc_ref, wa_ref, ba_ref, g_ref, o_ref, oc_ref, of_scr, ob_scr, dec_scr, st_scr, *, seq, ctx_len):
    c = GLA_CHUNK
    nh = GLA_HEADS
    kw = nh * GLA_DK
    vw = nh * GLA_DV
    iota = lambda shape, d: lax.broadcasted_iota(jnp.int32, shape, d)
    ra, ca = iota((nh * 2 * c, 2 * c), 0) & (2 * c - 1), iota((nh * 2 * c, 2 * c), 1)
    causal = ((ra < c) & (ca <= ra)) | ((ra >= c) & (ca >= ra))
    fwd_rows = iota((2 * c, 1), 0) < c
    k_of = lambda t: t >> int(math.log2(GLA_DK))
    v_of = lambda t: t >> int(math.log2(GLA_DV))
    k_head = [(k_of(iota((1, kw), 1)) == h).astype(F32) for h in range(nh)]
    v_head = [(v_of(iota((1, vw), 1)) == h).astype(F32) for h in range(nh)]
    st_rows = iota((2 * vw, kw), 0)
    st_mask = (v_of(st_rows & (vw - 1)) == k_of(iota((2 * vw, kw), 1))).astype(F32)
    st_fwd = st_rows < vw
    head_mean = ((v_of(iota((vw, vw), 0)) == v_of(iota((vw, vw), 1))).astype(F32) * (1.0 / GLA_DV)).astype(BF16)

    def decays(ref, n_rows, base):
        rows = GLA_FINISH_ROWS if n_rows % GLA_FINISH_ROWS == 0 else c
        rr, cc = iota((rows, rows), 0), iota((rows, rows), 1)
        same = (rr >> int(math.log2(c))) == (cc >> int(math.log2(c)))
        tri_f = (same & (cc <= rr)).astype(F32).astype(BF16)
        tri_b = (same & (cc >= rr)).astype(F32).astype(BF16)

        @pl.loop(0, n_rows // rows)
        def _(t):
            s = pl.multiple_of(t * rows, rows)
            a_low = ref[0, pl.ds(s, rows), 2 * kw + 2 * vw:2 * kw + 2 * vw + 256].astype(BF16)
            zl = _dot(a_low, wa_ref[...]) + ba_ref[...]
            la = (jnp.minimum(zl, 0.0) - jnp.log1p(jnp.exp(-jnp.abs(zl)))) / GLA_TAU
            at = pl.ds(pl.multiple_of(base + s, rows), rows)
            dec_scr[at, :kw] = _dot_exact_lhs(tri_f, la[:, :kw])
            dec_scr[at, kw:] = _dot_exact_lhs(tri_b, la[:, kw:])

    def pair(ref, sf, sb, base, st):
        blk_f, blk_b = ref[0, pl.ds(sf, c), :], ref[0, pl.ds(sb, c), :]
        blk = jnp.concatenate([blk_f, blk_b], axis=0)
        q = blk[:, 0:kw] * (GLA_DK ** -0.5)
        k = blk[:, kw:2 * kw]
        vb = blk[:, 2 * kw:2 * kw + vw].astype(BF16)
        b = jnp.concatenate([dec_scr[pl.ds(pl.multiple_of(base + sf, c), c), :kw],
                             dec_scr[pl.ds(pl.multiple_of(base + sb, c), c), kw:]], axis=0)
        b_end = jnp.where(fwd_rows, b[c - 1:c, :], b[c:c + 1, :])
        q_dec = q * jnp.exp(b)
        k_inv = (k * jnp.exp(-b)).astype(BF16)
        k_tail = (k * jnp.exp(b_end - b)).astype(BF16)
        qs = jnp.concatenate([q_dec * k_head[h] for h in range(nh)], axis=0).astype(BF16)
        att = lax.dot_general(qs, k_inv, NT_DIMS, preferred_element_type=F32)
        att = jnp.where(causal, att, 0.0).astype(BF16)
        o_all = _dot(att, vb)
        o = o_all[0:2 * c] * v_head[0]
        for h in range(1, nh):
            o = o + o_all[h * 2 * c:(h + 1) * 2 * c] * v_head[h]
        both = lax.dot_general(q_dec.astype(BF16), st.astype(BF16), NT_DIMS, preferred_element_type=F32)
        o = o + jnp.where(fwd_rows, both[:, :vw], both[:, vw:])
        v2 = jnp.concatenate([jnp.where(fwd_rows, vb, 0), jnp.where(fwd_rows, 0, vb)], axis=1)
        upd = lax.dot_general(v2, k_tail, TN_DIMS, preferred_element_type=F32)
        decay = jnp.where(st_fwd, jnp.exp(b[c - 1:c, :]), jnp.exp(b[c:c + 1, :]))
        return o, st * decay + upd * st_mask

    def scan(ref, n_chunks, base):
        unroll = GLA_UNROLL if n_chunks % GLA_UNROLL == 0 else 1

        @pl.loop(0, n_chunks // unroll)
        def _(it):
            st = st_scr[...]
            for u in range(unroll):
                n = it * unroll + u
                sf = pl.multiple_of(n * c, c)
                sb = pl.multiple_of((n_chunks - 1 - n) * c, c)
                o, st = pair(ref, sf, sb, base, st)
                of_scr[pl.ds(pl.multiple_of(base + sf, c), c), :] = o[:c]
                ob_scr[pl.ds(pl.multiple_of(base + sb, c), c), :] = o[c:]
            st_scr[...] = st

    def finish(ref, out_ref, n_rows, base):
        rows = GLA_FINISH_ROWS if n_rows % GLA_FINISH_ROWS == 0 else c

        @pl.loop(0, n_rows // rows)
        def _(t):
            s = pl.multiple_of(t * rows, rows)
            at = pl.ds(pl.multiple_of(base + s, rows), rows)
            o = of_scr[at, :] + ob_scr[at, :]
            ms = _dot_exact_rhs(o * o, head_mean)
            gate = ref[0, pl.ds(s, rows), 2 * kw + vw:2 * kw + 2 * vw]
            out_ref[0, pl.ds(s, rows), :] = (o * lax.rsqrt(ms + EPS) * g_ref[...] * _silu(gate)).astype(BF16)

    st_scr[...] = jnp.zeros_like(st_scr)
    decays(zc_ref, ctx_len, 0)
    decays(z_ref, seq, ctx_len)
    scan(zc_ref, ctx_len // c, 0)
    scan(z_ref, seq // c, ctx_len)
    finish(zc_ref, oc_ref, ctx_len, 0)
    finish(z_ref, o_ref, seq, ctx_len)


def gla_mixer(zg, zgc, w_a2, b_a, norm_g):
    bsz, seq, wd = zg.shape
    ctx_len = zgc.shape[1]
    vw = GLA_HEADS * GLA_DV
    kw = GLA_HEADS * GLA_DK
    wa = jnp.pad(w_a2, ((0, 0), (0, 128 - GLA_LOWRANK), (0, 0)))
    zero = jnp.zeros_like(wa[0])
    wa = jnp.block([[wa[0], zero], [zero, wa[1]]]).astype(BF16)
    full = lambda shape: pl.BlockSpec(shape, lambda b: (0,) * len(shape))
    return pl.pallas_call(
        functools.partial(_gla_kernel, seq=seq, ctx_len=ctx_len), grid=(bsz,),
        in_specs=[pl.BlockSpec((1, seq, wd), lambda b: (b, 0, 0)),
                  pl.BlockSpec((1, ctx_len, wd), lambda b: (b, 0, 0)),
                  full((256, 2 * kw)), full((1, 2 * kw)), full((1, vw))],
        out_specs=[pl.BlockSpec((1, seq, vw), lambda b: (b, 0, 0)),
                   pl.BlockSpec((1, ctx_len, vw), lambda b: (b, 0, 0))],
        out_shape=[jax.ShapeDtypeStruct((bsz, seq, vw), BF16), jax.ShapeDtypeStruct((bsz, ctx_len, vw), BF16)],
        scratch_shapes=[pltpu.VMEM((seq + ctx_len, vw), F32), pltpu.VMEM((seq + ctx_len, vw), F32),
                        pltpu.VMEM((seq + ctx_len, 2 * kw), F32), pltpu.VMEM((2 * vw, kw), F32)],
        compiler_params=_params("parallel"), name="gla_mixer")(
            zg, zgc, wa, b_a.reshape(1, 2 * kw), norm_g.reshape(1, vw))


def _conv_kernel(u_ref, dw_ref, dwb_ref, lng_ref, lnb_ref, pw_ref, pwb_ref, o_ref, h_scr, *, seq, rows):
    halo = CONV_HALO
    h_scr[0:halo, :] = jnp.zeros((halo, CONV_C), F32)
    h_scr[halo + seq:2 * halo + seq, :] = jnp.zeros((halo, CONV_C), F32)

    @pl.loop(0, seq // rows)
    def _(t):
        s = pl.multiple_of(t * rows, rows)
        u = u_ref[0, pl.ds(s, rows), :]
        h_scr[pl.ds(pl.multiple_of(halo + s, 8), rows), :] = u[:, :CONV_C] * jax.nn.sigmoid(u[:, CONV_C:])

    @pl.loop(0, seq // rows)
    def _(t):
        s = pl.multiple_of(t * rows, rows)
        win = h_scr[pl.ds(s, rows + 2 * halo), :]
        first = halo - CONV_K // 2
        acc = jnp.broadcast_to(dwb_ref[...], (rows, CONV_C))
        for r in range(8):
            taps = [k for k in range(CONV_K) if (first + k) % 8 == r]
            shifted = win if r == 0 else pltpu.roll(win, win.shape[0] - r, axis=0)
            for k in taps:
                at = 8 * ((first + k) // 8)
                acc = acc + shifted[at:at + rows] * dw_ref[k:k + 1, :]
        mu = jnp.mean(acc, axis=-1, keepdims=True)
        var = jnp.mean(jnp.square(acc - mu), axis=-1, keepdims=True)
        y = _silu((acc - mu) * lax.rsqrt(var + EPS) * lng_ref[...] + lnb_ref[...])
        o_ref[0, pl.ds(s, rows), :] = (_dot(y.astype(BF16), pw_ref[...]) + pwb_ref[...]).astype(BF16)


def conformer_conv(u, dw_w, dw_b, ln_g, ln_b, pw_w, pw_b):
    bsz, seq, _ = u.shape
    rows = _tile(seq, 128)
    full = lambda shape: pl.BlockSpec(shape, lambda b: (0,) * len(shape))
    row = lambda t: t.reshape(1, CONV_C)
    return pl.pallas_call(
        functools.partial(_conv_kernel, seq=seq, rows=rows), grid=(bsz,),
        in_specs=[pl.BlockSpec((1, seq, 2 * CONV_C), lambda b: (b, 0, 0)),
                  full((CONV_K + 1, CONV_C)), full((1, CONV_C)), full((1, CONV_C)), full((1, CONV_C)),
                  full((CONV_C, CONV_C)), full((1, CONV_C))],
        out_specs=pl.BlockSpec((1, seq, CONV_C), lambda b: (b, 0, 0)),
        out_shape=jax.ShapeDtypeStruct((bsz, seq, CONV_C), BF16),
        scratch_shapes=[pltpu.VMEM((seq + 2 * CONV_HALO, CONV_C), F32)],
        compiler_params=_params("parallel"), name="conformer_conv")(
            u, jnp.pad(dw_w, ((0, 1), (0, 0))), row(dw_b), row(ln_g), row(ln_b), pw_w.astype(BF16), row(pw_b))


def s5_matrices(lam_re, lam_im, log_dt, b_re, b_im, c_re, c_im):
    dt = jnp.exp(log_dt)[:, None]
    mag = jnp.exp(lam_re * dt)
    a_re, a_im = mag * jnp.cos(lam_im * dt), mag * jnp.sin(lam_im * dt)
    den = lam_re * lam_re + lam_im * lam_im
    f_re = ((a_re - 1.0) * lam_re + a_im * lam_im) / den
    f_im = (a_im * lam_re - (a_re - 1.0) * lam_im) / den
    bb_re = f_re[..., None] * b_re - f_im[..., None] * b_im
    bb_im = f_re[..., None] * b_im + f_im[..., None] * b_re
    eye = jnp.eye(S5_NG, dtype=F32)
    blk_b = lambda t: jnp.einsum("gph,gk->ghkp", t, eye).reshape(S5_C, S5_STATE)
    blk_c = lambda t: jnp.einsum("ghp,gk->gpkh", t, eye).reshape(S5_STATE, S5_C)
    a = jnp.stack([a_re.reshape(S5_STATE), a_im.reshape(S5_STATE)])
    b_mat = jnp.concatenate([blk_b(bb_re), blk_b(bb_im)], axis=1).astype(BF16)
    c_mat = jnp.concatenate([blk_c(c_re), -blk_c(c_im)], axis=0).astype(BF16)
    return a, b_mat, c_mat


def _s5_scan_kernel(uf_ref, ub_ref, a_ref, b_ref, c_ref, h0_ref, yf_ref, yb_ref, hl_ref, hs_f, hs_b, st_scr,
                    *, steps, bsz):
    ns = S5_STATE

    @pl.when(pl.program_id(0) == 0)
    def _():
        st_scr[...] = h0_ref[...]

    for d, (u_ref, hs) in enumerate(((uf_ref, hs_f), (ub_ref, hs_b))):
        u = u_ref[...].reshape(steps * bsz, S5_C).astype(BF16)
        hs[...] = _dot(u, b_ref[d])
    for d, hs in enumerate((hs_f, hs_b)):
        for s0 in range(0, ns, S5_STRIP):
            re, im = slice(s0, s0 + S5_STRIP), slice(ns + s0, ns + s0 + S5_STRIP)
            a_re, a_im = a_ref[d, 0:1, re], a_ref[d, 1:2, re]
            h_re, h_im = st_scr[d, :, re], st_scr[d, :, im]
            for j in range(steps):
                rows = pl.ds((steps - 1 - j if d == 1 else j) * bsz, bsz)
                h_re, h_im = (a_re * h_re - a_im * h_im + hs[rows, re], a_re * h_im + a_im * h_re + hs[rows, im])
                hs[rows, re] = h_re
                hs[rows, im] = h_im
            st_scr[d, :, re] = h_re
            st_scr[d, :, im] = h_im
    yf_ref[...] = _dot(hs_f[...].astype(BF16), c_ref[0]).reshape(steps, bsz, S5_C)
    yb_ref[...] = _dot(hs_b[...].astype(BF16), c_ref[1]).reshape(steps, bsz, S5_C)
    hl_ref[...] = st_scr[...]


def s5_scan(u_tm, a, b_mat, c_mat, h0):
    seq, bsz, _ = u_tm.shape
    steps = _tile(seq, 32)
    n = seq // steps
    full = lambda t: pl.BlockSpec(t.shape, lambda i: (0,) * t.ndim)
    fwd = pl.BlockSpec((steps, bsz, S5_C), lambda i: (i, 0, 0))
    bwd = pl.BlockSpec((steps, bsz, S5_C), lambda i: (n - 1 - i, 0, 0))
    y = jax.ShapeDtypeStruct((seq, bsz, S5_C), F32)
    return pl.pallas_call(
        functools.partial(_s5_scan_kernel, steps=steps, bsz=bsz), grid=(n,),
        in_specs=[fwd, bwd, full(a), full(b_mat), full(c_mat), full(h0)],
        out_specs=[fwd, bwd, full(h0)],
        out_shape=[y, y, jax.ShapeDtypeStruct(h0.shape, F32)],
        scratch_shapes=[pltpu.VMEM((steps * bsz, 2 * S5_STATE), F32), pltpu.VMEM((steps * bsz, 2 * S5_STATE), F32),
                        pltpu.VMEM(h0.shape, F32)],
        compiler_params=_params("arbitrary"), name="s5_scan")(u_tm, u_tm, a, b_mat, c_mat, h0)


def _s5_out_kernel(u_ref, yf_ref, yb_ref, d_ref, w_ref, b_ref, o_ref):
    y = d_ref[...] * u_ref[...] + yf_ref[...] + yb_ref[...]
    z = _dot(y.astype(BF16), w_ref[...]) + b_ref[...]
    o_ref[...] = (z[:, :S5_C] * jax.nn.sigmoid(z[:, S5_C:])).astype(BF16)


def s5_output(u, y_f, y_b, d_skip, glu_w, glu_b):
    n = u.shape[0]
    tm = _tile(n, 1024)
    rows = pl.BlockSpec((tm, S5_C), lambda i: (i, 0))
    full = lambda shape: pl.BlockSpec(shape, lambda i: (0,) * len(shape))
    return pl.pallas_call(
        _s5_out_kernel, grid=(n // tm,),
        in_specs=[rows, rows, rows, full((1, S5_C)), full((S5_C, 2 * S5_C)), full((1, 2 * S5_C))],
        out_specs=rows, out_shape=jax.ShapeDtypeStruct((n, S5_C), BF16),
        compiler_params=_params("parallel"), name="s5_output")(
            u, y_f, y_b, d_skip.reshape(1, S5_C), glu_w.astype(BF16), glu_b.reshape(1, 2 * S5_C))


def s5_mixer(zs, zsc, lam_re, lam_im, log_dt, b_re, b_im, c_re, c_im, d_skip, glu_w, glu_b):
    bsz, seq, _ = zs.shape
    ctx_len = zsc.shape[1]
    u = jnp.transpose(zs, (1, 0, 2))
    uc = jnp.transpose(zsc, (1, 0, 2))
    mats = [s5_matrices(lam_re[d], lam_im[d], log_dt[d], b_re[d], b_im[d], c_re[d], c_im[d]) for d in range(2)]
    a, b_mat, c_mat = [jnp.stack(t) for t in zip(*mats)]
    yc_f, yc_b, hc = s5_scan(uc, a, b_mat, c_mat, jnp.zeros((2, bsz, 2 * S5_STATE), F32))
    y_f, y_b, _ = s5_scan(u, a, b_mat, c_mat, hc)
    ys, ycs = (y_f, y_b), (yc_f, yc_b)
    flat = lambda t: t.reshape(-1, S5_C)
    o = s5_output(flat(u), flat(ys[0]), flat(ys[1]), d_skip, glu_w, glu_b).reshape(seq, bsz, S5_C)
    oc = s5_output(flat(uc), flat(ycs[0]), flat(ycs[1]), d_skip, glu_w, glu_b).reshape(ctx_len, bsz, S5_C)
    return jnp.transpose(o, (1, 0, 2)), jnp.transpose(oc, (1, 0, 2))


def rope_tables(seq_len, rotate):
    if not rotate:
        return jnp.ones((seq_len, MLA_ROPE), F32), jnp.zeros((seq_len, MLA_ROPE), F32)
    rows = seq_len // GRID_W
    row = jnp.broadcast_to(jnp.arange(rows, dtype=F32)[:, None], (rows, GRID_W)).reshape(seq_len)
    col = jnp.broadcast_to(jnp.arange(GRID_W, dtype=F32)[None, :], (rows, GRID_W)).reshape(seq_len)
    inv_freq = ROPE_BASE ** (-jnp.arange(ROPE_AXIS // 2, dtype=F32) / (ROPE_AXIS // 2))
    ang = jnp.stack([row[:, None] * inv_freq, col[:, None] * inv_freq], axis=1)
    full = lambda t: jnp.broadcast_to(t[:, :, None, :], (seq_len, 2, 2, ROPE_AXIS // 2)).reshape(seq_len, MLA_ROPE)
    return full(jnp.cos(ang)), full(jnp.sin(ang))


def mla_tables(seq_len, rotate):
    cos, sin = rope_tables(seq_len, rotate)
    head = lambda rope, fill: jnp.concatenate(
        [jnp.full((seq_len, MLA_NOPE), fill, F32), rope, jnp.zeros((seq_len, MLA_HEAD_PAD - MLA_NOPE - MLA_ROPE), F32)], 1)
    cos_q = jnp.tile(head(cos, 1.0), (1, MLA_HEADS))
    sin_q = jnp.tile(head(sin, 0.0), (1, MLA_HEADS))
    cs_k = jnp.concatenate([cos, sin, jnp.zeros((seq_len, 128 - 2 * MLA_ROPE), F32)], 1)
    return cos_q, sin_q, cs_k


def mla_weights(w_uq, w_ukv):
    hd = MLA_NOPE + MLA_ROPE
    zq = jnp.zeros((MLA_Q_LORA, MLA_HEAD_PAD - hd), F32)
    zn = jnp.zeros((MLA_Q_LORA, MLA_NOPE), F32)
    wq, wq_sw, wk, wv = [], [], [], []
    for h in range(MLA_HEADS):
        qh = w_uq[:, h * hd:(h + 1) * hd]
        wq += [qh, zq]
        wq_sw += [zn, _rope_swap(qh[:, MLA_NOPE:]), zq]
        kvh = w_ukv[:, h * (MLA_NOPE + MLA_V):(h + 1) * (MLA_NOPE + MLA_V)]
        wk += [kvh[:, :MLA_NOPE], jnp.zeros((MLA_KV_LORA, MLA_HEAD_PAD - MLA_NOPE), F32)]
        wv += [kvh[:, MLA_NOPE:]]
    place = np.zeros((128, MLA_HEADS * MLA_HEAD_PAD), np.float32)
    for h in range(MLA_HEADS):
        for j in range(MLA_ROPE):
            place[j, h * MLA_HEAD_PAD + MLA_NOPE + j] = 1.0
            place[MLA_ROPE + j, h * MLA_HEAD_PAD + MLA_NOPE + j] = 1.0
    cat = lambda ts: jnp.concatenate(ts, axis=1).astype(BF16)
    return cat(wq), cat(wq_sw), cat(wk), cat(wv), jnp.asarray(place).astype(BF16)


def _mla_prep_kernel(z_ref, qg_ref, kg_ref, wq_ref, wqs_ref, wk_ref, wv_ref, pl_ref, cq_ref, sq_ref, csk_ref,
                     q_ref, k_ref, v_ref):
    z = z_ref[0]
    norm = lambda t, g: (t * lax.rsqrt(jnp.mean(t * t, axis=-1, keepdims=True) + EPS) * g).astype(BF16)
    cq = norm(z[:, :MLA_Q_LORA], qg_ref[...])
    ckv = norm(z[:, MLA_Q_LORA:MLA_Q_LORA + MLA_KV_LORA], kg_ref[...])
    scale = (MLA_NOPE + MLA_ROPE) ** -0.5
    q = _dot(cq, wq_ref[...]) * cq_ref[...] + _dot(cq, wqs_ref[...]) * sq_ref[...]
    q_ref[0] = (q * scale).astype(BF16)
    kr = z[:, MLA_Q_LORA + MLA_KV_LORA:] * csk_ref[...]
    k_ref[0] = (_dot(ckv, wk_ref[...]) + _dot_exact_rhs(kr, pl_ref[...])).astype(BF16)
    v_ref[0] = _dot(ckv, wv_ref[...]).astype(BF16)


def mla_prep(zm, qn_g, kvn_g, weights, tables):
    bsz, seq, wd = zm.shape
    tm = _tile(seq, 1024)
    wq, wq_sw, wk, wv, place = weights
    cos_q, sin_q, cs_k = tables
    qw, vw = MLA_HEADS * MLA_HEAD_PAD, MLA_HEADS * MLA_V
    full = lambda t: pl.BlockSpec(t.shape, lambda b, i: (0,) * t.ndim)
    pos = lambda t: pl.BlockSpec((tm, t.shape[1]), lambda b, i: (i, 0))
    out = lambda w: pl.BlockSpec((1, tm, w), lambda b, i: (b, i, 0))
    qg, kg = qn_g.reshape(1, -1), kvn_g.reshape(1, -1)
    return pl.pallas_call(
        _mla_prep_kernel, grid=(bsz, seq // tm),
        in_specs=[pl.BlockSpec((1, tm, wd), lambda b, i: (b, i, 0)), full(qg), full(kg), full(wq), full(wq_sw),
                  full(wk), full(wv), full(place), pos(cos_q), pos(sin_q), pos(cs_k)],
        out_specs=[out(qw), out(qw), out(vw)],
        out_shape=[jax.ShapeDtypeStruct((bsz, seq, qw), BF16), jax.ShapeDtypeStruct((bsz, seq, qw), BF16),
                   jax.ShapeDtypeStruct((bsz, seq, vw), BF16)],
        compiler_params=_params("parallel", "parallel"), name="mla_prep")(
            zm, qg, kg, wq, wq_sw, wk, wv, place, cos_q, sin_q, cs_k)


def _mla_attn_kernel(q_ref, *refs, n_seg):
    k_refs, v_refs, o_ref = refs[:n_seg], refs[n_seg:2 * n_seg], refs[2 * n_seg]
    outs = []
    for h in range(MLA_HEADS):
        q = q_ref[0, :, h * MLA_HEAD_PAD:(h + 1) * MLA_HEAD_PAD]
        s = [lax.dot_general(q, k_ref[0, :, h * MLA_HEAD_PAD:(h + 1) * MLA_HEAD_PAD], NT_DIMS,
                             preferred_element_type=F32) for k_ref in k_refs]
        m = functools.reduce(jnp.maximum, [jnp.max(t, axis=-1, keepdims=True) for t in s])
        p = [jnp.exp(t - m) for t in s]
        den = functools.reduce(jnp.add, [jnp.sum(t, axis=-1, keepdims=True) for t in p])
        o = functools.reduce(jnp.add, [_dot(t.astype(BF16), v_ref[0, :, h * MLA_V:(h + 1) * MLA_V])
                                       for t, v_ref in zip(p, v_refs)])
        outs.append(o / den)
    o_ref[0] = jnp.concatenate(outs, axis=-1).astype(BF16)


def mla_attention(q, ks, vs):
    bsz, seq, qw = q.shape
    tq = _tile(seq, 1024)
    vw = MLA_HEADS * MLA_V
    seg = lambda t: pl.BlockSpec((1,) + t.shape[1:], lambda b, i: (b, 0, 0))
    return pl.pallas_call(
        functools.partial(_mla_attn_kernel, n_seg=len(ks)), grid=(bsz, seq // tq),
        in_specs=[pl.BlockSpec((1, tq, qw), lambda b, i: (b, i, 0))] + [seg(t) for t in ks] + [seg(t) for t in vs],
        out_specs=pl.BlockSpec((1, tq, vw), lambda b, i: (b, i, 0)),
        out_shape=jax.ShapeDtypeStruct((bsz, seq, vw), BF16),
        compiler_params=_params("parallel", "parallel"), name="mla_attention")(q, *ks, *vs)


def _outproj_kernel(a_ref, b_ref, c_ref, d_ref, w_ref, x_ref, g1_ref, sh_ref, sc_ref, g_ref, rw_ref,
                    xo_ref, h_ref, lg_ref):
    mix = None
    for j, o_ref in enumerate((a_ref, b_ref, c_ref, d_ref)):
        wd = o_ref.shape[-1]
        part = _dot(o_ref[0], w_ref[j * wd:(j + 1) * wd, :])
        mix = part if mix is None else mix + part
    x = x_ref[0] + g1_ref[0] * mix
    xo_ref[0] = x
    ms = jnp.mean(x * x, axis=-1, keepdims=True)
    h = x * lax.rsqrt(ms + EPS) * g_ref[...] * (1.0 + sc_ref[0]) + sh_ref[0]
    _store_row_slabs(h_ref, _pack_bf16_pairs(h))
    lg_ref[...] =lax.dot_general(rw_ref[...], h, NT_DIMS, precision=HIGHEST, preferred_element_type=F32)


def out_projection(parts, w_out, x, g1, shift, scale, g, router_w):
    bsz, seq, dm = x.shape
    tm = _tile(seq, 1024)
    nt = seq // tm
    ne = router_w.shape[1]
    groups = dm // 2 // 128
    rows = lambda w: pl.BlockSpec((1, tm, w), lambda b, i: (b, i, 0))
    vec = pl.BlockSpec((1, 1, dm), lambda b, i: (b, 0, 0))
    full = lambda shape: pl.BlockSpec(shape, lambda b, i: (0,) * len(shape))
    return pl.pallas_call(
        _outproj_kernel, grid=(bsz, nt),
        in_specs=[rows(p.shape[-1]) for p in parts] + [full(w_out.shape), rows(dm), vec, vec, vec, full((1, dm)),
                                                       full((ne, dm))],
        out_specs=[rows(dm), pl.BlockSpec((tm * groups, 128), lambda b, i: (b * nt + i, 0)),
                   pl.BlockSpec((ne, tm), lambda b, i: (0, b * nt + i))],
        out_shape=[jax.ShapeDtypeStruct((bsz, seq, dm), F32), jax.ShapeDtypeStruct((bsz * seq * groups, 128), jnp.int32),
                   jax.ShapeDtypeStruct((ne, bsz * seq), F32)],
        compiler_params=_params("parallel", "parallel"), name="out_projection")(
            *parts, w_out.astype(BF16), x, g1, shift, scale, g.reshape(1, dm), router_w.T)


def _route_kernel(lg_ref, b_ref, e_ref, w_ref):
    ne, tt = lg_ref.shape
    per = ne // N_GROUPS
    neg = -jnp.inf
    scores = jax.nn.sigmoid(lg_ref[...])
    biased = scores + b_ref[...]
    v3 = biased.reshape(N_GROUPS, per, tt)
    e_in = lax.broadcasted_iota(jnp.int32, v3.shape, 1).astype(F32)
    m1 = jnp.max(v3, axis=1, keepdims=True)
    i1 = jnp.min(jnp.where(v3 == m1, e_in, float(per)), axis=1, keepdims=True)
    m2 = jnp.max(jnp.where(e_in == i1, neg, v3), axis=1, keepdims=True)
    grp = (m1 + m2).reshape(N_GROUPS, tt)

    def pick(cur, count):
        ids = lax.broadcasted_iota(jnp.int32, cur.shape, 0).astype(F32)
        marks = jnp.zeros(cur.shape, F32)
        picked = []
        for _ in range(count):
            m = jnp.max(cur, axis=0, keepdims=True)
            first = jnp.min(jnp.where(cur == m, ids, float(cur.shape[0])), axis=0, keepdims=True)
            hit = ids == first
            marks = jnp.where(hit, 1.0, marks)
            cur = jnp.where(hit, neg, cur)
            picked.append(first)
        return marks, picked

    grp_on, _ = pick(grp, TOPK_GROUPS)
    exp_on = jnp.broadcast_to(grp_on.reshape(N_GROUPS, 1, tt), v3.shape).reshape(ne, tt)
    chosen, picked = pick(jnp.where(exp_on > 0.0, biased, neg), TOP_K)
    w = scores * chosen
    gate = w / jnp.sum(w, axis=0, keepdims=True) * ROUTED_SCALE
    ids = lax.broadcasted_iota(jnp.int32, gate.shape, 0).astype(F32)
    e_ref[...] = jnp.concatenate(picked, axis=0).astype(jnp.int32)
    w_ref[...] = jnp.concatenate([jnp.sum(jnp.where(ids == p, gate, 0.0), axis=0, keepdims=True) for p in picked], axis=0)


def route(logits_t, router_b):
    ne, n = logits_t.shape
    tt = _tile(n, 1024)
    out = pl.BlockSpec((TOP_K, tt), lambda i: (0, i))
    return pl.pallas_call(
        _route_kernel, grid=(n // tt,),
        in_specs=[pl.BlockSpec((ne, tt), lambda i: (0, i)), pl.BlockSpec((ne, 1), lambda i: (0, 0))],
        out_specs=[out, out],
        out_shape=[jax.ShapeDtypeStruct((TOP_K, n), jnp.int32), jax.ShapeDtypeStruct((TOP_K, n), F32)],
        compiler_params=_params("parallel"), name="route")(logits_t, router_b.reshape(ne, 1))


def moe_plan(eid, wgt, chunk, xg, yg):
    n = eid.shape[1]
    nc, na = n // chunk, chunk * TOP_K
    ns = na // MOE_ROWS + N_EXPERTS
    key = eid.T.reshape(nc, na) * na + jnp.arange(na, dtype=jnp.int32)
    skey, sw = lax.sort((key, wgt.T.reshape(nc, na)), dimension=1, num_keys=1)
    e_sorted, tok = skey // na, (skey % na) // TOP_K
    experts = jnp.arange(N_EXPERTS, dtype=jnp.int32)
    ends = jnp.sum(e_sorted[:, None, :] <= experts[None, :, None], axis=-1, dtype=jnp.int32)
    starts = jnp.concatenate([jnp.zeros((nc, 1), jnp.int32), ends[:, :-1]], axis=1)
    pad_end = jnp.cumsum(-(-(ends - starts) // MOE_ROWS) * MOE_ROWS, axis=1)
    pad_start = jnp.concatenate([jnp.zeros((nc, 1), jnp.int32), pad_end[:, :-1]], axis=1)
    first = jnp.arange(ns, dtype=jnp.int32) * MOE_ROWS
    exp = jnp.minimum(jnp.sum(pad_end[:, None, :] <= first[None, :, None], axis=-1, dtype=jnp.int32), N_EXPERTS - 1)
    active = first[None, :] < pad_end[:, -1:]
    take = lambda t, i: jnp.take_along_axis(t, i, axis=1)
    rank = first[None, :, None] + jnp.arange(MOE_ROWS, dtype=jnp.int32) - take(pad_start, exp)[..., None]
    real = (rank < take(ends - starts, exp)[..., None]) & active[..., None]
    src = jnp.clip(take(starts, exp)[..., None] + rank, 0, na - 1).reshape(nc, ns * MOE_ROWS)
    rows = lambda t: take(t, src).reshape(nc, ns, MOE_ROWS)
    flat = lambda t: t.reshape(nc * ns, 1, MOE_ROWS)
    gather_at = flat(jnp.where(real, rows(tok), 0) * xg)
    scatter_at = flat(jnp.where(real, rows(tok), chunk) * yg)
    weight = flat(jnp.where(real, rows(sw), 0.0))
    return gather_at, scatter_at, weight, exp.reshape(nc * ns), active.astype(jnp.int32).reshape(nc * ns)


def _moe_tick(x_ref, y_ref, idx_ref, w_ref, tick, wgu_ref, wd_ref, gather_to, mm_from, mm_to, scatter_from):
    bm = MOE_ROWS
    stride = bm + 1
    xg = mm_from.shape[0] // stride
    yg = mm_to.shape[0] // stride
    for mi in range(bm):
        src = pl.multiple_of(idx_ref[0, 0, 2 * tick * bm + mi], xg)
        gather_to[pl.ds(mi, xg, stride=stride), :] = x_ref[0, pl.ds(src, xg), :]
    low, high = _unpack_bf16_pairs([mm_from[j * stride:j * stride + bm, :] for j in range(xg)])
    half = low.shape[1]
    ff = wd_ref.shape[1]
    up = _dot(low, wgu_ref[0, :half, :]) + _dot(high, wgu_ref[0, half:, :])
    act = (_silu(up[:, :ff]) * up[:, ff:]).astype(BF16)
    out = _dot(act, wd_ref[0])
    for j in range(yg):
        mm_to[j * stride:j * stride + bm, :] = out[:, j * 128:(j + 1) * 128]
    for base in range(0, bm, MOE_BATCH):
        at = [pl.multiple_of(idx_ref[0, 0, (2 * tick + 1) * bm + mi], yg) for mi in range(base, base + MOE_BATCH)]
        new = [y_ref[0, pl.ds(i, yg), :] + w_ref[0, 0, tick * bm + mi] * scatter_from[pl.ds(mi, yg, stride=stride), :]
               for i, mi in zip(at, range(base, base + MOE_BATCH))]
        for i, v in zip(at, new):
            y_ref[0, pl.ds(i, yg), :] = v


def _moe_routed_kernel(exp_ref, act_ref, x_ref, idx_ref, w_ref, wgua_ref, wda_ref, wgub_ref, wdb_ref,
                       y_ref, xt0, xt1, ot0, ot1, *, ns):
    c, g = pl.program_id(0), pl.program_id(1)

    @pl.when(g == 0)
    def _():
        y_ref[...] = jnp.zeros_like(y_ref)

    @pl.when((c == 0) & (g == 0))
    def _():
        xt1[...] = jnp.zeros_like(xt1)
        ot0[...] = jnp.zeros_like(ot0)
        ot1[...] = jnp.zeros_like(ot1)

    def live(step):
        return (step >= 0) & (step < ns) & (act_ref[c * ns + jnp.clip(step, 0, ns - 1)] != 0)

    t = 2 * g

    @pl.when(live(t - 2) | live(t - 1) | live(t) | live(t + 1))
    def _():
        _moe_tick(x_ref, y_ref, idx_ref, w_ref, 0, wgua_ref, wda_ref,
                  gather_to=xt0, mm_from=xt1, mm_to=ot1, scatter_from=ot0)
        _moe_tick(x_ref, y_ref, idx_ref, w_ref, 1, wgub_ref, wdb_ref,
                  gather_to=xt1, mm_from=xt0, mm_to=ot0, scatter_from=ot1)


def moe_routed(slabs, eid, wgt, wgu, wd):
    n = eid.shape[1]
    xg = slabs.shape[0] // n
    yg = 2 * xg
    dm, ff = wd.shape[2], wd.shape[1]
    chunk = _tile(n, MOE_CHUNK)
    nc = n // chunk
    ns = chunk * TOP_K // MOE_ROWS + N_EXPERTS
    assert ns % 2 == 0
    steps = (ns + 2) // 2
    last = ns - 1
    gather_at, scatter_at, weight, exp, active = moe_plan(eid, wgt, chunk, xg, yg)
    tick = jnp.arange(steps, dtype=jnp.int32)[:, None] * 2 + jnp.arange(2, dtype=jnp.int32)[None, :]
    gathered = jnp.minimum(tick, last)
    scattered = jnp.where(tick >= 2, tick - 2, last)
    per_step = lambda t, at: jnp.take(t.reshape(nc, ns, MOE_ROWS), at, axis=1)
    idx = jnp.concatenate([per_step(gather_at, gathered[:, 0]), per_step(scatter_at, scattered[:, 0]),
                           per_step(gather_at, gathered[:, 1]), per_step(scatter_at, scattered[:, 1])], axis=-1)
    wts = jnp.concatenate([per_step(weight, scattered[:, 0]), per_step(weight, scattered[:, 1])], axis=-1)
    smem = lambda width: pl.BlockSpec((1, 1, width), lambda c, g, e, a: (c * steps + g, 0, 0), memory_space=pltpu.SMEM)
    computed = lambda tk: lambda g: jnp.clip(2 * g + tk - 1, 0, last)
    expert = lambda shape, step: pl.BlockSpec((1,) + shape, lambda c, g, e, a: (e[c * ns + step(g)], 0, 0))
    once = pl.Buffered(1)
    stage = lambda groups, dtype: pltpu.VMEM((-(-groups * (MOE_ROWS + 1) // 8) * 8, 128), dtype)
    return pl.pallas_call(
        functools.partial(_moe_routed_kernel, ns=ns),
        grid_spec=pltpu.PrefetchScalarGridSpec(
            num_scalar_prefetch=2, grid=(nc, steps),
            in_specs=[pl.BlockSpec((1, chunk * xg, 128), lambda c, g, e, a: (c, 0, 0), pipeline_mode=once),
                      smem(4 * MOE_ROWS), smem(2 * MOE_ROWS),
                      expert((dm, 2 * ff), computed(0)), expert((ff, dm), computed(0)),
                      expert((dm, 2 * ff), computed(1)), expert((ff, dm), computed(1))],
            out_specs=pl.BlockSpec((1, chunk * yg + 8, 128), lambda c, g, e, a: (c, 0, 0), pipeline_mode=once),
            scratch_shapes=[stage(xg, jnp.int32), stage(xg, jnp.int32), stage(yg, F32), stage(yg, F32)]),
        out_shape=jax.ShapeDtypeStruct((nc, chunk * yg + 8, 128), F32),
        compiler_params=_params("arbitrary", "arbitrary"), name="moe_routed")(
            exp, active, slabs.reshape(nc, chunk * xg, 128), idx.reshape(nc * steps, 1, 4 * MOE_ROWS),
            wts.reshape(nc * steps, 1, 2 * MOE_ROWS), wgu, wd, wgu, wd)


def _moe_finish_kernel(y_ref, h_ref, x_ref, g2_ref, wg_ref, wu_ref, wd_ref, fg_ref, o_ref, *, final_norm):
    tm = x_ref.shape[0]
    low, high = _unpack_bf16_pairs(_load_row_slabs(h_ref, tm, h_ref.shape[0] // tm))
    routed = jnp.concatenate(_load_row_slabs(y_ref, tm, y_ref.shape[1] // tm, lead=(0,)), axis=1)
    half = low.shape[1]
    up = lambda w_ref: _dot(low, w_ref[:half, :]) + _dot(high, w_ref[half:, :])
    shared = _dot((_silu(up(wg_ref)) * up(wu_ref)).astype(BF16), wd_ref[...])
    x = x_ref[...] + g2_ref[0] * (routed + shared)
    if final_norm:
        x = x * lax.rsqrt(jnp.mean(x * x, axis=-1, keepdims=True) + EPS) * fg_ref[...]
    o_ref[...] = x


def moe_finish(y, slabs, x, g2, sg, su, sd, final_g, final_norm):
    bsz, seq, dm = x.shape
    n = bsz * seq
    tm = _tile(seq, 1024)
    nt = seq // tm
    xg, yg = slabs.shape[0] // n, dm // 128
    per_chunk = (y.shape[1] - 8) // yg // tm
    rows = lambda w: pl.BlockSpec((tm, w), lambda i: (i, 0))
    full = lambda t: pl.BlockSpec(t.shape, lambda i: (0,) * t.ndim)
    sg, su, sd, fg = sg.astype(BF16), su.astype(BF16), sd.astype(BF16), final_g.reshape(1, dm)
    out = pl.pallas_call(
        functools.partial(_moe_finish_kernel, final_norm=final_norm), grid=(n // tm,),
        in_specs=[pl.BlockSpec((1, tm * yg, 128), lambda i: (i // per_chunk, i % per_chunk, 0)),
                  pl.BlockSpec((tm * xg, 128), lambda i: (i, 0)), rows(dm),
                  pl.BlockSpec((1, 1, dm), lambda i: (i // nt, 0, 0)), full(sg), full(su), full(sd), full(fg)],
        out_specs=rows(dm), out_shape=jax.ShapeDtypeStruct((n, dm), F32),
        compiler_params=_params("parallel"), name="moe_finish")(y, slabs, x.reshape(n, dm), g2, sg, su, sd, fg)
    return out.reshape(bsz, seq, dm)


def kernel(x, c, ctx, c_ctx, ada_w, ada_b, norm1_g, norm2_g, w_in, w_out, gla_w_a2, gla_b_a, gla_norm_g,
           conv_dw_w, conv_dw_b, conv_ln_g, conv_ln_b, conv_pw_w, conv_pw_b,
           s5_lam_re, s5_lam_im, s5_log_dt, s5_b_re, s5_b_im, s5_c_re, s5_c_im, s5_d, s5_glu_w, s5_glu_b,
           mla_qn_g, mla_kvn_g, mla_w_uq, mla_w_ukv,
           moe_router_w, moe_router_b, moe_w_gate, moe_w_up, moe_w_down,
           shared_w_gate, shared_w_up, shared_w_down, final_g):
    bsz, seq, dm = x.shape
    ctx_len = ctx.shape[1]
    depth = ada_w.shape[0]
    cc = jnp.concatenate([c, c_ctx[None], jnp.zeros((-(bsz + 1) % 8, dm), F32)], axis=0)
    tables = mla_tables(seq, True)
    tables_c = mla_tables(ctx_len, False)
    xc = ctx
    for l in range(depth):
        last = l == depth - 1
        mod = ada_mod(cc, ada_w[l], ada_b[l])
        lat = [t.reshape(bsz, 1, dm) for t in jnp.split(mod[:bsz], 6, axis=-1)]
        con = [jnp.broadcast_to(t.reshape(1, 1, dm), (bsz, 1, dm)) for t in jnp.split(mod[bsz], 6)]
        sh1, sc1, g1, sh2, sc2, g2 = lat
        sh1c, sc1c, g1c, sh2c, sc2c, g2c = con
        w_in_l = assemble_w_in(w_in[l])
        zg, zv, zs, zm = in_projection(x, sh1, sc1, norm1_g[l], w_in_l)
        zgc, zvc, zsc, zmc = in_projection(xc, sh1c, sc1c, norm1_g[l], w_in_l)
        conv_p = (conv_dw_w[l], conv_dw_b[l], conv_ln_g[l], conv_ln_b[l], conv_pw_w[l], conv_pw_b[l])
        o_gla, oc_gla = gla_mixer(zg, zgc, gla_w_a2[l], gla_b_a[l], gla_norm_g[l])
        o_conv = conformer_conv(zv, *conv_p)
        o_s5, oc_s5 = s5_mixer(zs, zsc, s5_lam_re[l], s5_lam_im[l], s5_log_dt[l], s5_b_re[l], s5_b_im[l],
                               s5_c_re[l], s5_c_im[l], s5_d[l], s5_glu_w[l], s5_glu_b[l])
        mla_w = mla_weights(mla_w_uq[l], mla_w_ukv[l])
        q, k, v = mla_prep(zm, mla_qn_g[l], mla_kvn_g[l], mla_w, tables)
        qc, kc, vc = mla_prep(zmc, mla_qn_g[l], mla_kvn_g[l], mla_w, tables_c)
        o_mla = mla_attention(q, [k, kc], [v, vc])
        experts = (jnp.concatenate([moe_w_gate[l], moe_w_up[l]], axis=2).astype(BF16), moe_w_down[l].astype(BF16))
        shared = (shared_w_gate[l], shared_w_up[l], shared_w_down[l])

        def ffn(parts, x_in, gate1, shift, scale, gate2, final_norm):
            x_mid, slabs, logits = out_projection(parts, w_out[l], x_in, gate1, shift, scale, norm2_g[l], moe_router_w[l])
            y = moe_routed(slabs, *route(logits, moe_router_b[l]), *experts)
            return moe_finish(y, slabs, x_mid, gate2, *shared, final_g, final_norm)

        if not last:
            oc_conv = conformer_conv(zvc, *conv_p)
            oc_mla = mla_attention(qc, [kc], [vc])
            xc = ffn([oc_gla, oc_conv, oc_s5, oc_mla], xc, g1c, sh2c, sc2c, g2c, False)
        x = ffn([o_gla, o_conv, o_s5, o_mla], x, g1, sh2, sc2, g2, last)
    return x
```
